```python
import math
import jax, jax.numpy as jnp
from jax import lax
import numpy as np

D_MODEL = 1024
BATCH = 2
SEQ = 8192
DEPTH = 2

GRID_W = 64
CTX_LEN = 256
HEAD_DIM = 64
ROPE_THETA = 10000.0
NORM_EPS = 1e-6
NEG_INF = -1e30

A_HEADS = 8
A_KV_HEADS = 2
A_WINDOW = 128
A_BLOCK = 128
B_HEADS = 8
NA_ROWS = 8
NA_COLS = 16
NA_COL_BLOCK = 16
NA_BAND = 32
C_HEADS = 4
C_VDIM = 2 * HEAD_DIM
C_BLOCK = 128
PEER_HEADS = 8
PEER_KEYS = 128
PEER_TOPK = 16
PEER_QDIM = 256
N_EXPERTS = PEER_KEYS * PEER_KEYS
PEER_CHUNK = 128

A_Q_W = A_HEADS * HEAD_DIM
A_KV_W = A_KV_HEADS * HEAD_DIM
B_W = B_HEADS * HEAD_DIM
C_QK_W = C_HEADS * 2 * HEAD_DIM
C_V_W = C_HEADS * C_VDIM
IN_SPLITS = (A_Q_W, A_KV_W, A_KV_W, B_W, B_W, B_W, C_QK_W, C_QK_W, C_V_W, D_MODEL, D_MODEL, D_MODEL)
IN_W = 6912

kernel_name = "hybrid_dit_gqa_natten_diffattn_peer"


def rms_norm(x, g):
    xf = x.astype(jnp.float32)
    y = xf * lax.rsqrt(jnp.mean(xf * xf, axis=-1, keepdims=True) + NORM_EPS)
    return (y * g.astype(jnp.float32)).astype(x.dtype)


def split_cols(p):
    idx = np.cumsum(np.array(IN_SPLITS))[:-1].tolist()
    return jnp.split(p, idx, axis=-1)


def axial_rope(n):
    t = jnp.arange(n, dtype=jnp.int32)
    row = (t // GRID_W).astype(jnp.float32)
    col = (t % GRID_W).astype(jnp.float32)
    n_freq = HEAD_DIM // 4
    inv = ROPE_THETA ** (-jnp.arange(n_freq, dtype=jnp.float32) / n_freq)
    ang = jnp.concatenate([row[:, None] * inv, col[:, None] * inv], axis=-1)
    return jnp.cos(ang), jnp.sin(ang)


def apply_rope(x, cos, sin):
    half = HEAD_DIM // 2
    xf = x.astype(jnp.float32)
    x1, x2 = xf[..., :half], xf[..., half:]
    cs, sn = cos[None, :, None, :], sin[None, :, None, :]
    return jnp.concatenate([x1 * cs - x2 * sn, x2 * cs + x1 * sn], axis=-1).astype(x.dtype)


def project_heads(h, w, rope):
    p = split_cols(h @ w)
    B, n = h.shape[:2]
    aq = p[0].reshape(B, n, A_HEADS, HEAD_DIM)
    ak = p[1].reshape(B, n, A_KV_HEADS, HEAD_DIM)
    av = p[2].reshape(B, n, A_KV_HEADS, HEAD_DIM)
    bq = p[3].reshape(B, n, B_HEADS, HEAD_DIM)
    bk = p[4].reshape(B, n, B_HEADS, HEAD_DIM)
    bv = p[5].reshape(B, n, B_HEADS, HEAD_DIM)
    cq = p[6].reshape(B, n, C_HEADS * 2, HEAD_DIM)
    ck = p[7].reshape(B, n, C_HEADS * 2, HEAD_DIM)
    cv = p[8].reshape(B, n, C_HEADS, C_VDIM)
    if rope is not None:
        cos, sin = rope
        aq, ak = apply_rope(aq, cos, sin), apply_rope(ak, cos, sin)
        cq, ck = apply_rope(cq, cos, sin), apply_rope(ck, cos, sin)
    cq = cq.reshape(B, n, C_HEADS, 2, HEAD_DIM)
    ck = ck.reshape(B, n, C_HEADS, 2, HEAD_DIM)
    return {"aq": aq, "ak": ak, "av": av, "bq": bq, "bk": bk, "bv": bv,
            "cq": cq, "ck": ck, "cv": cv, "ga": p[9], "gb": p[10], "gc": p[11]}


def context_attention(q, k, v, sink=None):
    B, L, H, D = q.shape
    G = k.shape[2]
    R = H // G
    qg = q.reshape(B, L, G, R, D)
    s = jnp.einsum('blgrd,bmgd->bgrlm', qg, k).astype(jnp.float32) * (D ** -0.5)
    if sink is not None:
        sk = jnp.broadcast_to(sink.reshape(G, R, 1, 1).astype(jnp.float32), s.shape[:-1] + (1,))
        s = jnp.concatenate([s, sk], axis=-1)
    p = jax.nn.softmax(s, axis=-1)[..., :L].astype(v.dtype)
    return jnp.einsum('bgrlm,bmgd->blgrd', p, v).reshape(B, L, H * D)


def window_gqa_latent(q, k, v, kc, vc, sink):
    B, N = q.shape[:2]
    L = kc.shape[1]
    nb = N // A_BLOCK
    G, R = A_KV_HEADS, A_HEADS // A_KV_HEADS
    scale = HEAD_DIM ** -0.5
    qb = q.reshape(B, nb, A_BLOCK, G, R, HEAD_DIM)

    def band(t):
        tp = jnp.pad(t, ((0, 0), (A_BLOCK, A_BLOCK), (0, 0), (0, 0))).reshape(B, nb + 2, A_BLOCK, G, HEAD_DIM)
        return jnp.concatenate([tp[:, :-2], tp[:, 1:-1], tp[:, 2:]], axis=2)

    kb, vb = band(k), band(v)
    nk = 3 * A_BLOCK
    qpos = jnp.arange(nb)[:, None] * A_BLOCK + jnp.arange(A_BLOCK)[None, :]
    kpos = jnp.arange(nb)[:, None] * A_BLOCK - A_BLOCK + jnp.arange(nk)[None, :]
    valid = ((jnp.abs(qpos[:, :, None] - kpos[:, None, :]) <= A_WINDOW)
             & (kpos[:, None, :] >= 0) & (kpos[:, None, :] < N))
    s_loc = jnp.einsum('bnqgrd,bnkgd->bgrnqk', qb, kb).astype(jnp.float32) * scale
    s_loc = jnp.where(valid, s_loc, NEG_INF)
    s_ctx = jnp.einsum('bnqgrd,blgd->bgrnql', qb, kc).astype(jnp.float32) * scale
    sk = jnp.broadcast_to(sink.reshape(G, R, 1, 1, 1).astype(jnp.float32), s_loc.shape[:-1] + (1,))
    p = jax.nn.softmax(jnp.concatenate([s_loc, s_ctx, sk], axis=-1), axis=-1).astype(v.dtype)
    o = (jnp.einsum('bgrnqk,bnkgd->bnqgrd', p[..., :nk], vb)
         + jnp.einsum('bgrnql,blgd->bnqgrd', p[..., nk:nk + L], vc))
    return o.reshape(B, N, A_HEADS * HEAD_DIM)


def na_column_tables():
    n_cb = GRID_W // NA_COL_BLOCK
    band0 = np.clip(np.arange(n_cb) * NA_COL_BLOCK - NA_COLS // 2, 0, GRID_W - NA_BAND)
    kcol = band0[:, None] + np.arange(NA_BAND)[None, :]
    qcol = np.arange(n_cb)[:, None] * NA_COL_BLOCK + np.arange(NA_COL_BLOCK)[None, :]
    cstart = np.clip(qcol - NA_COLS // 2, 0, GRID_W - NA_COLS)
    valid = (kcol[:, None, :] >= cstart[:, :, None]) & (kcol[:, None, :] < cstart[:, :, None] + NA_COLS)
    dc = np.clip(kcol[:, None, :] - qcol[:, :, None], -(NA_COLS - 1), NA_COLS - 1) + NA_COLS - 1
    return jnp.asarray(kcol), jnp.asarray(valid), jnp.asarray(dc)


def neighborhood_latent(q, k, v, kc, vc, rpb):
    B, N = q.shape[:2]
    rows = N // GRID_W
    kr = min(NA_ROWS, rows)
    n_cb = GRID_W // NA_COL_BLOCK
    H, D = B_HEADS, HEAD_DIM
    scale = D ** -0.5
    kcol, valid, dc = na_column_tables()
    qg = q.reshape(B, rows, GRID_W, H, D)
    kg = k.reshape(B, rows, GRID_W, H, D)
    vg = v.reshape(B, rows, GRID_W, H, D)
    nloc = kr * NA_BAND

    def row_step(r):
        rs = jnp.clip(r - kr // 2, 0, rows - kr)
        kw = lax.dynamic_slice_in_dim(kg, rs, kr, axis=1)[:, :, kcol]
        vw = lax.dynamic_slice_in_dim(vg, rs, kr, axis=1)[:, :, kcol]
        qr = lax.dynamic_index_in_dim(qg, r, axis=1, keepdims=False).reshape(B, n_cb, NA_COL_BLOCK, H, D)
        s = jnp.einsum('bnqhd,bknjhd->bhnqkj', qr, kw).astype(jnp.float32) * scale
        dr = rs + jnp.arange(kr) - r + NA_ROWS - 1
        bias = rpb[:, dr[None, None, :, None], dc[:, :, None, :]].astype(jnp.float32)
        s = jnp.where(valid[:, :, None, :], s + bias, NEG_INF)
        s_c = jnp.einsum('bnqhd,blhd->bhnql', qr, kc).astype(jnp.float32) * scale
        logits = jnp.concatenate([s.reshape(B, H, n_cb, NA_COL_BLOCK, nloc), s_c], axis=-1)
        p = jax.nn.softmax(logits, axis=-1).astype(v.dtype)
        p_loc = p[..., :nloc].reshape(B, H, n_cb, NA_COL_BLOCK, kr, NA_BAND)
        o = (jnp.einsum('bhnqkj,bknjhd->bnqhd', p_loc, vw)
             + jnp.einsum('bhnql,blhd->bnqhd', p[..., nloc:], vc))
        return o.reshape(B, GRID_W, H * D)

    o = lax.map(row_step, jnp.arange(rows, dtype=jnp.int32))
    return jnp.moveaxis(o, 0, 1).reshape(B, N, H * D)


def diff_lambda(lam, lam_init):
    lf = lam.astype(jnp.float32)
    return jnp.exp(jnp.sum(lf[0] * lf[1])) - jnp.exp(jnp.sum(lf[2] * lf[3])) + lam_init


def diff_attention_block(qb, k, v, lam):
    s = jnp.einsum('bqhmd,bkhmd->bhmqk', qb, k).astype(jnp.float32) * (HEAD_DIM ** -0.5)
    p = jax.nn.softmax(s, axis=-1)
    a = (p[:, :, 0] - lam * p[:, :, 1]).astype(v.dtype)
    return jnp.einsum('bhqk,bkhd->bqhd', a, v)


def diff_attention_latent(q, k, v, kc, vc, lam):
    B, N = q.shape[:2]
    nb = N // C_BLOCK
    kall = jnp.concatenate([k, kc], axis=1)
    vall = jnp.concatenate([v, vc], axis=1)
    qb = jnp.moveaxis(q.reshape(B, nb, C_BLOCK, C_HEADS, 2, HEAD_DIM), 1, 0)
    o = lax.map(lambda t: diff_attention_block(t, kall, vall, lam), qb)
    return jnp.moveaxis(o, 0, 1).reshape(B, N, C_HEADS, C_VDIM)


def diff_head_norm(o, g, lam_init):
    B, n = o.shape[:2]
    y = rms_norm(o, g.reshape(C_HEADS, C_VDIM)) * (1.0 - lam_init)
    return y.reshape(B, n, C_V_W)


def merge_branches(hd, ya, yb, yc, wa, wb, wc, wo):
    m = (jax.nn.sigmoid(hd["ga"]) * (ya @ wa) + jax.nn.sigmoid(hd["gb"]) * (yb @ wb)
         + jax.nn.sigmoid(hd["gc"]) * (yc @ wc))
    return m @ wo


def peer_ffn(h, wq, keys, u, v):
    shp = h.shape
    t = h.reshape(-1, PEER_CHUNK, shp[-1])

    def chunk(tc):
        q = (tc @ wq).reshape(PEER_CHUNK, PEER_HEADS, 2, PEER_QDIM // 2)
        s = jnp.einsum('thpd,hpkd->thpk', q, keys).astype(jnp.float32)
        sv, si = lax.top_k(s, PEER_TOPK)
        cand_s = (sv[:, :, 0, :, None] + sv[:, :, 1, None, :]).reshape(PEER_CHUNK, PEER_HEADS, PEER_TOPK * PEER_TOPK)
        cand_i = (si[:, :, 0, :, None] * PEER_KEYS + si[:, :, 1, None, :]).reshape(PEER_CHUNK, PEER_HEADS, PEER_TOPK * PEER_TOPK)
        top_s, pos = lax.top_k(cand_s, PEER_TOPK)
        e = jnp.take_along_axis(cand_i, pos, axis=-1)
        g = jax.nn.softmax(top_s, axis=-1)
        act = jax.nn.gelu(jnp.einsum('thkd,td->thk', u[e], tc).astype(jnp.float32), approximate=False)
        return jnp.einsum('thk,thkd->td', (g * act).astype(tc.dtype), v[e])

    return lax.map(chunk, t).reshape(shp)


def setup_inputs(seed: int = 0) -> dict:
    key = jax.random.key(seed)
    ks = jax.random.split(key, 24)
    D = D_MODEL

    def nrm(k, shape, s):
        return jax.random.normal(k, shape, jnp.float32) * s

    return {
        "x": nrm(ks[0], (BATCH, SEQ, D), 1.0),
        "c": nrm(ks[1], (BATCH, D), 1.0),
        "ctx": nrm(ks[2], (BATCH, CTX_LEN, D), 1.0),
        "c_ctx": nrm(ks[3], (D,), 1.0),
        "w_mod": nrm(ks[4], (DEPTH, D, 6 * D), 0.5 * D ** -0.5),
        "b_mod": nrm(ks[5], (DEPTH, 6 * D), 0.01),
        "g_norm1": 1.0 + nrm(ks[6], (DEPTH, D), 0.05),
        "g_norm2": 1.0 + nrm(ks[7], (DEPTH, D), 0.05),
        "w_in": nrm(ks[8], (DEPTH, D, IN_W), D ** -0.5),
        "a_sink": nrm(ks[9], (DEPTH, A_HEADS), 0.5),
        "b_rpb": nrm(ks[10], (DEPTH, B_HEADS, 2 * NA_ROWS - 1, 2 * NA_COLS - 1), 0.1),
        "c_lambda": nrm(ks[11], (DEPTH, 4, HEAD_DIM), 0.1),
        "c_subln": 1.0 + nrm(ks[12], (DEPTH, C_V_W), 0.05),
        "w_branch_a": nrm(ks[13], (DEPTH, A_Q_W, D), A_Q_W ** -0.5),
        "w_branch_b": nrm(ks[14], (DEPTH, B_W, D), B_W ** -0.5),
        "w_branch_c": nrm(ks[15], (DEPTH, C_V_W, D), C_V_W ** -0.5),
        "w_out": nrm(ks[16], (DEPTH, D, D), D ** -0.5),
        "peer_wq": nrm(ks[17], (DEPTH, D, PEER_HEADS * PEER_QDIM), D ** -0.5),
        "peer_keys": nrm(ks[18], (DEPTH, PEER_HEADS, 2, PEER_KEYS, PEER_QDIM // 2), (PEER_QDIM // 2) ** -0.5),
        "peer_u": nrm(ks[19], (DEPTH, N_EXPERTS, D), D ** -0.5),
        "peer_v": nrm(ks[20], (DEPTH, N_EXPERTS, D), PEER_HEADS ** -0.5),
        "g_final": 1.0 + nrm(ks[21], (D,), 0.05),
    }


def reference(x, c, ctx, c_ctx, w_mod, b_mod, g_norm1, g_norm2, w_in, a_sink, b_rpb, c_lambda,
              c_subln, w_branch_a, w_branch_b, w_branch_c, w_out, peer_wq, peer_keys, peer_u,
              peer_v, g_final):
    N = x.shape[1]
    rope = axial_rope(N)
    xc = ctx
    for l in range(DEPTH):
        last = l == DEPTH - 1
        lam_init = 0.8 - 0.6 * math.exp(-0.3 * l)
        lam = diff_lambda(c_lambda[l], lam_init)
        mod = (jax.nn.silu(c) @ w_mod[l] + b_mod[l])[:, None, :]
        mod_c = jax.nn.silu(c_ctx) @ w_mod[l] + b_mod[l]
        sh1, sc1, gt1, sh2, sc2, gt2 = jnp.split(mod, 6, axis=-1)
        sh1c, sc1c, gt1c, sh2c, sc2c, gt2c = jnp.split(mod_c, 6, axis=-1)

        h = rms_norm(x, g_norm1[l]) * (1.0 + sc1) + sh1
        hc = rms_norm(xc, g_norm1[l]) * (1.0 + sc1c) + sh1c
        lat = project_heads(h, w_in[l], rope)
        con = project_heads(hc, w_in[l], None)
        ya = window_gqa_latent(lat["aq"], lat["ak"], lat["av"], con["ak"], con["av"], a_sink[l])
        yb = neighborhood_latent(lat["bq"], lat["bk"], lat["bv"], con["bk"], con["bv"], b_rpb[l])
        yc = diff_head_norm(diff_attention_latent(lat["cq"], lat["ck"], lat["cv"], con["ck"], con["cv"], lam),
                            c_subln[l], lam_init)
        x = x + gt1 * merge_branches(lat, ya, yb, yc, w_branch_a[l], w_branch_b[l], w_branch_c[l], w_out[l])

        h2 = rms_norm(x, g_norm2[l]) * (1.0 + sc2) + sh2
        x = x + gt2 * peer_ffn(h2, peer_wq[l], peer_keys[l], peer_u[l], peer_v[l])

        if not last:
            ya_c = context_attention(con["aq"], con["ak"], con["av"], a_sink[l])
            yb_c = context_attention(con["bq"], con["bk"], con["bv"])
            yc_c = diff_head_norm(diff_attention_block(con["cq"], con["ck"], con["cv"], lam), c_subln[l], lam_init)
            xc = xc + gt1c * merge_branches(con, ya_c, yb_c, yc_c, w_branch_a[l], w_branch_b[l], w_branch_c[l], w_out[l])
            h2c = rms_norm(xc, g_norm2[l]) * (1.0 + sc2c) + sh2c
            xc = xc + gt2c * peer_ffn(h2c, peer_wq[l], peer_keys[l], peer_u[l], peer_v[l])
    return rms_norm(x, g_final)
```

```python
import functools
import math

import jax
import jax.numpy as jnp
import numpy as np
from jax import lax
from jax.experimental import pallas as pl
from jax.experimental.pallas import tpu as pltpu

F32 = jnp.float32
BF16 = jnp.bfloat16
MXU_DTYPE = BF16

LANES = 128
VMEM_LIMIT_BYTES = 56 * 1024 * 1024

HEAD_DIM = 64
GRID_W = 64
ROPE_THETA = 10000.0
NORM_EPS = 1e-6
NEG_INF = -1e30
A_HEADS, A_KV_HEADS, A_WINDOW, A_BLOCK = 8, 2, 128, 128
B_HEADS, NA_ROWS, NA_COLS = 8, 8, 16
C_HEADS = 4
PEER_HEADS, PEER_KEYS, PEER_TOPK = 8, 128, 16
D_MODEL = 1024
IN_SPLITS = (512, 128, 128, 512, 512, 512, 512, 512, 512, 1024, 1024, 1024)
IN_OFFS = tuple(int(v) for v in np.cumsum((0,) + IN_SPLITS))

TM_PROJ = 512
TQ_C = 512
TK_C = 1024
B_ROWS = 8
B_KROWS = 4
TT_PEER = 512
TE_PEER = 1024
N_PT = 30


def _nt(a, b):
    return lax.dot_general(a, b, (((1,), (1,)), ((), ())), preferred_element_type=F32)


def _nn(a, b):
    return jnp.dot(a, b, preferred_element_type=F32)


def _params(sem):
    return pltpu.CompilerParams(dimension_semantics=sem, vmem_limit_bytes=VMEM_LIMIT_BYTES)


def _resident(shape):
    nd = len(shape)
    return pl.BlockSpec(shape, lambda *_: (0,) * nd, pipeline_mode=pl.Buffered(1))


def _mod_kernel(c_ref, w_ref, b_ref, o_ref):
    c = c_ref[...]
    s = c * jax.nn.sigmoid(c)
    w = w_ref[0]
    s_hi = s.astype(MXU_DTYPE)
    s_lo = (s - s_hi.astype(F32)).astype(MXU_DTYPE)
    w_hi = w.astype(MXU_DTYPE)
    w_lo = (w - w_hi.astype(F32)).astype(MXU_DTYPE)
    acc = _nn(s_hi, w_hi) + _nn(s_lo, w_hi) + _nn(s_hi, w_lo)
    o_ref[0] = acc + b_ref[0]


def _modulation(cvec, w_mod, b_mod):
    depth, d, n6 = w_mod.shape
    tn = 1536
    return pl.pallas_call(
        _mod_kernel,
        out_shape=jax.ShapeDtypeStruct((depth, 8, n6), F32),
        grid=(depth, n6 // tn),
        in_specs=[pl.BlockSpec((8, d), lambda l, j: (0, 0)),
                  pl.BlockSpec((1, d, tn), lambda l, j: (l, 0, j)),
                  pl.BlockSpec((1, 1, tn), lambda l, j: (l, 0, j))],
        out_specs=pl.BlockSpec((1, 8, tn), lambda l, j: (l, 0, j)),
        compiler_params=_params(("parallel", "parallel")),
        name="modulation",
    )(cvec, w_mod, b_mod.reshape(depth, 1, n6))


def _rms_mod(x, g, sc, sh):
    y = x * lax.rsqrt(jnp.mean(x * x, axis=-1, keepdims=True) + NORM_EPS)
    return (y * g) * (1.0 + sc) + sh


def _in_proj_kernel(x_ref, g_ref, sc_ref, sh_ref, w_ref, cos_ref, sin_ref,
                    aq_ref, ak_ref, aks_ref, av_ref, avs_ref, bq_ref, bk_ref, bv_ref,
                    cq_ref, ck_ref, cv_ref, ga_ref, gb_ref, gc_ref):
    h = _rms_mod(x_ref[0], g_ref[...], sc_ref[0], sh_ref[0]).astype(MXU_DTYPE)
    cos = cos_ref[...]
    sin = sin_ref[...]
    lane = lax.broadcasted_iota(jnp.int32, (1, LANES), 1)
    first_half = (lane % HEAD_DIM) < (HEAD_DIM // 2)
    scale = HEAD_DIM ** -0.5

    def proj(seg, j):
        c0 = IN_OFFS[seg] + j * LANES
        return _nn(h, w_ref[:, c0:c0 + LANES])

    def rope(v):
        rot = jnp.where(first_half, pltpu.roll(v, LANES - 32, 1), pltpu.roll(v, 32, 1))
        return v * cos + rot * sin

    for j in range(4):
        aq_ref[0, :, j * LANES:(j + 1) * LANES] = (rope(proj(0, j)) * scale).astype(aq_ref.dtype)
        bq_ref[0, :, j * LANES:(j + 1) * LANES] = (proj(3, j) * scale).astype(bq_ref.dtype)
        bk_ref[0, :, j * LANES:(j + 1) * LANES] = proj(4, j).astype(bk_ref.dtype)
        bv_ref[0, :, j * LANES:(j + 1) * LANES] = proj(5, j).astype(bv_ref.dtype)
        cq_ref[0, :, j * LANES:(j + 1) * LANES] = (rope(proj(6, j)) * scale).astype(cq_ref.dtype)
        ck_ref[0, :, j * LANES:(j + 1) * LANES] = rope(proj(7, j)).astype(ck_ref.dtype)
        cv_ref[0, :, j * LANES:(j + 1) * LANES] = proj(8, j).astype(cv_ref.dtype)
    ak = rope(proj(1, 0))
    av = proj(2, 0)
    ak_ref[0] = ak.astype(ak_ref.dtype)
    aks_ref[0] = pltpu.roll(ak, HEAD_DIM, 1).astype(aks_ref.dtype)
    av_ref[0] = av.astype(av_ref.dtype)
    avs_ref[0] = pltpu.roll(av, HEAD_DIM, 1).astype(avs_ref.dtype)
    for seg, ref in ((9, ga_ref), (10, gb_ref), (11, gc_ref)):
        for j in range(D_MODEL // LANES):
            ref[0, :, j * LANES:(j + 1) * LANES] = jax.nn.sigmoid(proj(seg, j)).astype(ref.dtype)


def _in_proj(x, g1, mod3, mod_row, w_in, cos_t, sin_t):
    bsz, n, d = x.shape
    tm = min(TM_PROJ, n)
    tok = lambda w: pl.BlockSpec((1, tm, w), lambda b, i: (b, i, 0))
    widths = (512, 128, 128, 128, 128, 512, 512, 512, 512, 512, 512, 1024, 1024, 1024)
    return pl.pallas_call(
        _in_proj_kernel,
        out_shape=[jax.ShapeDtypeStruct((bsz, n, w), MXU_DTYPE) for w in widths],
        grid=(bsz, n // tm),
        in_specs=[tok(d),
                  pl.BlockSpec((1, d), lambda b, i: (0, 0)),
                  pl.BlockSpec((1, 1, d), lambda b, i: (mod_row(b), 0, 1)),
                  pl.BlockSpec((1, 1, d), lambda b, i: (mod_row(b), 0, 0)),
                  _resident(w_in.shape),
                  pl.BlockSpec((tm, LANES), lambda b, i: (i, 0)),
                  pl.BlockSpec((tm, LANES), lambda b, i: (i, 0))],
        out_specs=[tok(w) for w in widths],
        compiler_params=_params(("parallel", "parallel")),
        name="in_proj",
    )(x, g1, mod3, mod3, w_in, cos_t, sin_t)


def _lane_lo():
    return lax.broadcasted_iota(jnp.int32, (1, LANES), 1) < HEAD_DIM


def _softmax_pv(s_list, v_list, extra_logit=None):
    m = functools.reduce(jnp.maximum, [jnp.max(s, axis=-1, keepdims=True) for s in s_list])
    if extra_logit is not None:
        m = jnp.maximum(m, extra_logit)
    l = 0.0 if extra_logit is None else jnp.exp(extra_logit - m)
    o = 0.0
    for s, v in zip(s_list, v_list):
        p = jnp.exp(s - m)
        l = l + jnp.sum(p, axis=-1, keepdims=True)
        o = o + _nn(p.astype(v.dtype), v)
    return o / l


def _attn_a_kernel(sink_ref, q_ref, kp_ref, kc_ref, kn_ref, ksp_ref, ksc_ref, ksn_ref,
                   vp_ref, vc_ref, vn_ref, vsp_ref, vsc_ref, vsn_ref,
                   xk_ref, xks_ref, xv_ref, xvs_ref, o_ref, *, n_tokens):
    i = pl.program_id(1)
    lo = _lane_lo()
    cat = lambda refs: jnp.concatenate([r[0] for r in refs], axis=0)
    k_loc, ks_loc = cat((kp_ref, kc_ref, kn_ref)), cat((ksp_ref, ksc_ref, ksn_ref))
    v_loc, vs_loc = cat((vp_ref, vc_ref, vn_ref)), cat((vsp_ref, vsc_ref, vsn_ref))
    zero = jnp.zeros((), k_loc.dtype)
    k_of = {(0, 0): (jnp.where(lo, k_loc, zero), jnp.where(lo, xk_ref[0], zero)),
            (0, 1): (jnp.where(lo, zero, ks_loc), jnp.where(lo, zero, xks_ref[0])),
            (1, 0): (jnp.where(lo, ks_loc, zero), jnp.where(lo, xks_ref[0], zero)),
            (1, 1): (jnp.where(lo, zero, k_loc), jnp.where(lo, zero, xk_ref[0]))}
    v_of = {(0, 0): (v_loc, xv_ref[0]), (0, 1): (vs_loc, xvs_ref[0]),
            (1, 0): (vs_loc, xvs_ref[0]), (1, 1): (v_loc, xv_ref[0])}
    qpos = i * A_BLOCK + lax.broadcasted_iota(jnp.int32, (A_BLOCK, 1), 0)
    kpos = (i - 1) * A_BLOCK + lax.broadcasted_iota(jnp.int32, (1, 3 * A_BLOCK), 1)
    dist = qpos - kpos
    valid = (jnp.maximum(dist, -dist) <= A_WINDOW) & (kpos >= 0) & (kpos < n_tokens)
    for hp in range(A_HEADS // 2):
        qp = q_ref[0, :, hp * LANES:(hp + 1) * LANES]
        outs = []
        for half in range(2):
            h = 2 * hp + half
            g = h // (A_HEADS // A_KV_HEADS)
            (kl, kx), (vl, vx) = k_of[(g, half)], v_of[(g, half)]
            s_loc = jnp.where(valid, _nt(qp, kl), NEG_INF)
            outs.append(_softmax_pv([s_loc, _nt(qp, kx)], [vl, vx], extra_logit=sink_ref[h]))
        o_ref[0, :, hp * LANES:(hp + 1) * LANES] = jnp.where(lo, outs[0], outs[1]).astype(o_ref.dtype)


def _attn_a(sink, aq, ak, aks, av, avs, xak, xaks, xav, xavs):
    bsz, n, _ = aq.shape
    nb = n // A_BLOCK
    ctx = xak.shape[1]
    prev = pl.BlockSpec((1, A_BLOCK, LANES), lambda b, i: (b, jnp.maximum(i - 1, 0), 0))
    cur = pl.BlockSpec((1, A_BLOCK, LANES), lambda b, i: (b, i, 0))
    nxt = pl.BlockSpec((1, A_BLOCK, LANES), lambda b, i: (b, jnp.minimum(i + 1, nb - 1), 0))
    cx = pl.BlockSpec((1, ctx, LANES), lambda b, i: (b, 0, 0))
    return pl.pallas_call(
        functools.partial(_attn_a_kernel, n_tokens=n),
        out_shape=jax.ShapeDtypeStruct(aq.shape, MXU_DTYPE),
        grid=(bsz, nb),
        in_specs=[pl.BlockSpec(memory_space=pltpu.SMEM),
                  pl.BlockSpec((1, A_BLOCK, 512), lambda b, i: (b, i, 0)),
                  prev, cur, nxt, prev, cur, nxt, prev, cur, nxt, prev, cur, nxt, cx, cx, cx, cx],
        out_specs=pl.BlockSpec((1, A_BLOCK, 512), lambda b, i: (b, i, 0)),
        compiler_params=_params(("parallel", "parallel")),
        name="attn_window",
    )(sink, aq, ak, ak, ak, aks, aks, aks, av, av, av, avs, avs, avs, xak, xaks, xav, xavs)


def _attn_b_kernel(rpb_ref, q_ref, k0_ref, k1_ref, k2_ref, k3_ref, v0_ref, v1_ref, v2_ref, v3_ref,
                   xk_ref, xv_ref, o_ref, pt_ref, *, n_rows):
    pair, i = pl.program_id(0), pl.program_id(2)
    n_a = 2 * NA_ROWS - 1
    n_b = 2 * NA_COLS - 1

    @pl.when((pl.program_id(1) == 0) & (i == 0))
    def _build_bias_tables():
        qc = lax.broadcasted_iota(jnp.int32, (GRID_W, LANES), 0)
        ln = lax.broadcasted_iota(jnp.int32, (GRID_W, LANES), 1)
        kc = ln % GRID_W
        hi = ln >= GRID_W
        cstart = jnp.clip(qc - NA_COLS // 2, 0, GRID_W - NA_COLS)
        col_ok = (kc >= cstart) & (kc < cstart + NA_COLS)
        d = kc - qc + (NA_COLS - 1)
        for hh in range(2):
            h = pair * 2 + hh

            def body(ai, carry):
                a = ai - 8
                a_ok, a1_ok = (a >= 0) & (a < n_a), (a + 1 >= 0) & (a + 1 < n_a)
                ra, ra1 = h * n_a + jnp.clip(a, 0, n_a - 1), h * n_a + jnp.clip(a + 1, 0, n_a - 1)
                t = jnp.full((GRID_W, LANES), NEG_INF, F32)
                for b in range(n_b):
                    va = jnp.where(a_ok, rpb_ref[ra, b], NEG_INF)
                    va1 = jnp.where(a1_ok, rpb_ref[ra1, b], NEG_INF)
                    t = jnp.where(d == b, jnp.where(hi, va1, va), t)
                pt_ref[hh, ai] = jnp.where(col_ok, t, NEG_INF)
                return carry

            lax.fori_loop(0, N_PT, body, 0)

    lo = _lane_lo()
    r0 = i * B_ROWS
    kb0 = jnp.clip(2 * i - 1, 0, n_rows // B_KROWS - 4)
    krow_lane = lax.broadcasted_iota(jnp.int32, (1, B_KROWS * GRID_W), 1) // GRID_W
    k_refs, v_refs = (k0_ref, k1_ref, k2_ref, k3_ref), (v0_ref, v1_ref, v2_ref, v3_ref)
    qp = q_ref[0]
    zero = jnp.zeros((), qp.dtype)
    outs = []
    for hh in range(2):
        keep = lambda t: jnp.where(lo, t, zero) if hh == 0 else jnp.where(lo, zero, t)
        s_tiles = []
        for j in range(4):
            s = _nt(qp, keep(k_refs[j][0, 0]))
            kr0 = (kb0 + j) * B_KROWS
            rows = []
            for qr in range(B_ROWS):
                r = r0 + qr
                rs = jnp.clip(r - NA_ROWS // 2, 0, n_rows - NA_ROWS)
                a0 = kr0 - r + (NA_ROWS - 1)
                bias = jnp.concatenate([pt_ref[hh, a0 + 8], pt_ref[hh, a0 + 10]], axis=1)
                row_ok = (kr0 + krow_lane >= rs) & (kr0 + krow_lane < rs + NA_ROWS)
                rows.append(jnp.where(row_ok, s[qr * GRID_W:(qr + 1) * GRID_W] + bias, NEG_INF))
            s_tiles.append(jnp.concatenate(rows, axis=0))
        s_tiles.append(_nt(qp, keep(xk_ref[0])))
        outs.append(_softmax_pv(s_tiles, [v_refs[j][0, 0] for j in range(4)] + [xv_ref[0]]))
    o_ref[0] = jnp.where(lo, outs[0], outs[1]).astype(o_ref.dtype)


def _attn_b(rpb, bq, bk, bv, xbk, xbv):
    bsz, n, _ = bq.shape
    n_rows = n // GRID_W
    tq = B_ROWS * GRID_W
    tkb = B_KROWS * GRID_W
    ctx = xbk.shape[1]
    bk4 = bk.reshape(bsz, n // tkb, tkb, 512)
    bv4 = bv.reshape(bsz, n // tkb, tkb, 512)
    nkb = n // tkb

    def kspec(j):
        return pl.BlockSpec((1, 1, tkb, LANES),
                            lambda p, b, i: (b, jnp.clip(2 * i - 1, 0, nkb - 4) + j, 0, p))

    cx = pl.BlockSpec((1, ctx, LANES), lambda p, b, i: (b, 0, p))
    return pl.pallas_call(
        functools.partial(_attn_b_kernel, n_rows=n_rows),
        out_shape=jax.ShapeDtypeStruct(bq.shape, MXU_DTYPE),
        grid=(B_HEADS // 2, bsz, n // tq),
        in_specs=[pl.BlockSpec(memory_space=pltpu.SMEM),
                  pl.BlockSpec((1, tq, LANES), lambda p, b, i: (b, i, p)),
                  kspec(0), kspec(1), kspec(2), kspec(3), kspec(0), kspec(1), kspec(2), kspec(3),
                  cx, cx],
        out_specs=pl.BlockSpec((1, tq, LANES), lambda p, b, i: (b, i, p)),
        scratch_shapes=[pltpu.VMEM((2, N_PT, GRID_W, LANES), F32)],
        compiler_params=_params(("arbitrary", "arbitrary", "arbitrary")),
        name="attn_neighbourhood",
    )(rpb.reshape(B_HEADS * (2 * NA_ROWS - 1), 2 * NA_COLS - 1), bq,
      bk4, bk4, bk4, bk4, bv4, bv4, bv4, bv4, xbk, xbv)


def _diff_lambda(lam_ref, lam_init):
    lam = lam_ref[...]
    a = jnp.sum(lam[0:1] * lam[1:2], axis=-1, keepdims=True)
    b = jnp.sum(lam[2:3] * lam[3:4], axis=-1, keepdims=True)
    return jnp.exp(a) - jnp.exp(b) + lam_init


def _head_norm(o, g, lam_init):
    y = o * lax.rsqrt(jnp.mean(o * o, axis=-1, keepdims=True) + NORM_EPS)
    return (y * g) * (1.0 - lam_init)


def _attn_c_kernel(lam_ref, g_ref, q_ref, k_ref, v_ref, xk_ref, xv_ref, o_ref, k1_ref, k2_ref,
                   *, lam_init, n_tokens):
    lo = _lane_lo()

    @pl.when(pl.program_id(2) == 0)
    def _split_keys():
        zero = jnp.zeros((), k1_ref.dtype)
        k1_ref[0:n_tokens] = jnp.where(lo, k_ref[0], zero)
        k2_ref[0:n_tokens] = jnp.where(lo, zero, k_ref[0])
        k1_ref[n_tokens:] = jnp.where(lo, xk_ref[0], zero)
        k2_ref[n_tokens:] = jnp.where(lo, zero, xk_ref[0])

    q = q_ref[0]
    tq = q.shape[0]

    def update(carry, ka, kb, v):
        new = []
        for (m, l, acc), kk in zip(carry, (ka, kb)):
            s = _nt(q, kk)
            m_new = jnp.maximum(m, jnp.max(s, axis=-1, keepdims=True))
            alpha = jnp.exp(m - m_new)
            p = jnp.exp(s - m_new)
            new.append((m_new, alpha * l + jnp.sum(p, axis=-1, keepdims=True),
                        alpha * acc + _nn(p.astype(v.dtype), v)))
        return tuple(new)

    def body(c, carry):
        off = pl.multiple_of(c * TK_C, TK_C)
        return update(carry, k1_ref[pl.ds(off, TK_C), :], k2_ref[pl.ds(off, TK_C), :],
                      v_ref[0, pl.ds(off, TK_C), :])

    init = tuple((jnp.full((tq, 1), -jnp.inf, F32), jnp.zeros((tq, 1), F32),
                  jnp.zeros((tq, LANES), F32)) for _ in range(2))
    carry = lax.fori_loop(0, n_tokens // TK_C, body, init)
    (_, l1, acc1), (_, l2, acc2) = update(carry, k1_ref[n_tokens:], k2_ref[n_tokens:], xv_ref[0])
    o = acc1 / l1 - _diff_lambda(lam_ref, lam_init) * (acc2 / l2)
    o_ref[0] = _head_norm(o, g_ref[0], lam_init).astype(o_ref.dtype)


def _attn_c(c_lambda, subln3, cq, ck, cv, xck, xcv, lam_init):
    bsz, n, _ = cq.shape
    ctx = xck.shape[1]
    tq = min(TQ_C, n)
    full = pl.BlockSpec((1, n, LANES), lambda b, h, i: (b, 0, h))
    cx = pl.BlockSpec((1, ctx, LANES), lambda b, h, i: (b, 0, h))
    return pl.pallas_call(
        functools.partial(_attn_c_kernel, lam_init=lam_init, n_tokens=n),
        out_shape=jax.ShapeDtypeStruct(cq.shape, MXU_DTYPE),
        grid=(bsz, C_HEADS, n // tq),
        in_specs=[pl.BlockSpec((4, HEAD_DIM), lambda b, h, i: (0, 0)),
                  pl.BlockSpec((1, 1, LANES), lambda b, h, i: (h, 0, 0)),
                  pl.BlockSpec((1, tq, LANES), lambda b, h, i: (b, i, h)),
                  full, full, cx, cx],
        out_specs=pl.BlockSpec((1, tq, LANES), lambda b, h, i: (b, i, h)),
        scratch_shapes=[pltpu.VMEM((n + ctx, LANES), MXU_DTYPE),
                        pltpu.VMEM((n + ctx, LANES), MXU_DTYPE)],
        compiler_params=_params(("arbitrary", "arbitrary", "arbitrary")),
        name="attn_differential",
    )(c_lambda, subln3, cq, ck, cv, xck, xcv)


def _ctx_attn_kernel(sink_ref, lam_ref, g_ref, aq_ref, ak_ref, aks_ref, av_ref, avs_ref,
                     bq_ref, bk_ref, bv_ref, cq_ref, ck_ref, cv_ref, ya_ref, yb_ref, yc_ref,
                     *, lam_init):
    lo = _lane_lo()
    zero = jnp.zeros((), ak_ref.dtype)
    sel = lambda t, half: jnp.where(lo, t, zero) if half == 0 else jnp.where(lo, zero, t)
    k_of = {(0, 0): sel(ak_ref[0], 0), (0, 1): sel(aks_ref[0], 1),
            (1, 0): sel(aks_ref[0], 0), (1, 1): sel(ak_ref[0], 1)}
    v_of = {(0, 0): av_ref[0], (0, 1): avs_ref[0], (1, 0): avs_ref[0], (1, 1): av_ref[0]}
    for hp in range(A_HEADS // 2):
        cols = slice(hp * LANES, (hp + 1) * LANES)
        qa, qb = aq_ref[0, :, cols], bq_ref[0, :, cols]
        kb, vb = bk_ref[0, :, cols], bv_ref[0, :, cols]
        oa, ob = [], []
        for half in range(2):
            h = 2 * hp + half
            g = h // (A_HEADS // A_KV_HEADS)
            oa.append(_softmax_pv([_nt(qa, k_of[(g, half)])], [v_of[(g, half)]],
                                  extra_logit=sink_ref[h]))
            ob.append(_softmax_pv([_nt(qb, sel(kb, half))], [vb]))
        ya_ref[0, :, cols] = jnp.where(lo, oa[0], oa[1]).astype(ya_ref.dtype)
        yb_ref[0, :, cols] = jnp.where(lo, ob[0], ob[1]).astype(yb_ref.dtype)
    lam = _diff_lambda(lam_ref, lam_init)
    for h in range(C_HEADS):
        cols = slice(h * LANES, (h + 1) * LANES)
        q, k, v = cq_ref[0, :, cols], ck_ref[0, :, cols], cv_ref[0, :, cols]
        o = _softmax_pv([_nt(q, sel(k, 0))], [v]) - lam * _softmax_pv([_nt(q, sel(k, 1))], [v])
        yc_ref[0, :, cols] = _head_norm(o, g_ref[h], lam_init).astype(yc_ref.dtype)


def _ctx_attn(sink, c_lambda, subln3, con, lam_init):
    aq, ak, aks, av, avs, bq, bk, bv, cq, ck, cv = con[:11]
    bsz, ctx, _ = aq.shape
    wide = pl.BlockSpec((1, ctx, 512), lambda b: (b, 0, 0))
    nar = pl.BlockSpec((1, ctx, LANES), lambda b: (b, 0, 0))
    return pl.pallas_call(
        functools.partial(_ctx_attn_kernel, lam_init=lam_init),
        out_shape=[jax.ShapeDtypeStruct(aq.shape, MXU_DTYPE)] * 3,
        grid=(bsz,),
        in_specs=[pl.BlockSpec(memory_space=pltpu.SMEM),
                  pl.BlockSpec((4, HEAD_DIM), lambda b: (0, 0)),
                  pl.BlockSpec((C_HEADS, 1, LANES), lambda b: (0, 0, 0)),
                  wide, nar, nar, nar, nar, wide, wide, wide, wide, wide, wide],
        out_specs=[wide] * 3,
        compiler_params=_params(("parallel",)),
        name="attn_context",
    )(sink, c_lambda, subln3, aq, ak, aks, av, avs, bq, bk, bv, cq, ck, cv)


def _merge_kernel(x_ref, gt_ref, ya_ref, yb_ref, yc_ref, ga_ref, gb_ref, gc_ref,
                  wa_ref, wb_ref, wc_ref, wo_ref, o_ref):
    m = (ga_ref[0].astype(F32) * _nn(ya_ref[0], wa_ref[...])
         + gb_ref[0].astype(F32) * _nn(yb_ref[0], wb_ref[...])
         + gc_ref[0].astype(F32) * _nn(yc_ref[0], wc_ref[...]))
    o_ref[0] = x_ref[0] + gt_ref[0] * _nn(m.astype(MXU_DTYPE), wo_ref[...])


def _merge(x, mod3, mod_row, ya, yb, yc, ga, gb, gc, wa, wb, wc, wo):
    bsz, n, d = x.shape
    tm = min(TM_PROJ, n)
    tok = lambda w: pl.BlockSpec((1, tm, w), lambda b, i: (b, i, 0))
    return pl.pallas_call(
        _merge_kernel,
        out_shape=jax.ShapeDtypeStruct(x.shape, F32),
        grid=(bsz, n // tm),
        in_specs=[tok(d), pl.BlockSpec((1, 1, d), lambda b, i: (mod_row(b), 0, 2)),
                  tok(512), tok(512), tok(512), tok(d), tok(d), tok(d),
                  _resident(wa.shape), _resident(wb.shape), _resident(wc.shape), _resident(wo.shape)],
        out_specs=tok(d),
        compiler_params=_params(("parallel", "parallel")),
        name="merge",
    )(x, mod3, ya, yb, yc, ga, gb, gc, wa, wb, wc, wo)


def _top_values(ref, row0, n_rows, count, out_ref, out_row0, floor):
    for k in range(count):
        cur = ref[row0:row0 + n_rows]
        m = jnp.max(cur, axis=0, keepdims=True)
        out_ref[out_row0 + k:out_row0 + k + 1] = m
        if k + 1 < count:
            ref[row0:row0 + n_rows] = jnp.where(cur == m, floor, cur)


def _route_kernel(x_ref, g_ref, sc_ref, sh_ref, wq_ref, keys_ref,
                  h_ref, e1_ref, e2_ref, thr_ref, s_ref, top_ref, cand_ref):
    n_top = PEER_TOPK + 1
    h2 = _rms_mod(x_ref[0], g_ref[...], sc_ref[0], sh_ref[0]).astype(MXU_DTYPE)
    h_ref[0] = h2
    q = _nn(h2, wq_ref[...]).astype(MXU_DTYPE)
    n_cand = PEER_TOPK * PEER_TOPK
    for h in range(PEER_HEADS):
        e_full, top_e = [], []
        for p in range(2):
            hp = 2 * h + p
            s = _nt(keys_ref[hp], q[:, hp * LANES:(hp + 1) * LANES])
            m = jnp.max(s, axis=0, keepdims=True)
            e_full.append(jnp.exp(s - m))
            s_ref[...] = s
            _top_values(s_ref, 0, PEER_KEYS, n_top, top_ref, 0, -jnp.inf)
            top_e.append(jnp.exp(top_ref[0:n_top] - m))
        e1_top, e2_top = top_e
        for a in range(PEER_TOPK):
            cand_ref[a * PEER_TOPK:(a + 1) * PEER_TOPK] = e1_top[a:a + 1] * e2_top[0:PEER_TOPK]
        cand_ref[n_cand:n_cand + 1] = e1_top[PEER_TOPK:n_top] * e2_top[0:1]
        cand_ref[n_cand + 1:n_cand + 2] = e1_top[0:1] * e2_top[PEER_TOPK:n_top]
        cand_ref[n_cand + 2:] = jnp.full((6, cand_ref.shape[1]), -1.0, F32)
        _top_values(cand_ref, 0, n_cand + 8, n_top, top_ref, 0, -1.0)
        best = top_ref[0:n_top]
        inv_z = 1.0 / jnp.sum(best[0:PEER_TOPK], axis=0, keepdims=True)
        e1_ref[0, h] = e_full[0] * inv_z
        e2_ref[0, h] = e_full[1]
        thr_ref[0, h:h + 1] = 0.5 * (best[PEER_TOPK - 1:PEER_TOPK] + best[PEER_TOPK:n_top]) * inv_z


def _route(x, g2, mod3, mod_row, wq, keys):
    bsz, n, d = x.shape
    tt = min(TT_PEER, n)
    gate_shape = jax.ShapeDtypeStruct((bsz, PEER_HEADS, PEER_KEYS, n), F32)
    gate_spec = pl.BlockSpec((1, PEER_HEADS, PEER_KEYS, tt), lambda b, i: (b, 0, 0, i))
    return pl.pallas_call(
        _route_kernel,
        out_shape=[jax.ShapeDtypeStruct((bsz, n, d), MXU_DTYPE), gate_shape, gate_shape,
                   jax.ShapeDtypeStruct((bsz, PEER_HEADS, n), F32)],
        grid=(bsz, n // tt),
        in_specs=[pl.BlockSpec((1, tt, d), lambda b, i: (b, i, 0)),
                  pl.BlockSpec((1, d), lambda b, i: (0, 0)),
                  pl.BlockSpec((1, 1, d), lambda b, i: (mod_row(b), 0, 4)),
                  pl.BlockSpec((1, 1, d), lambda b, i: (mod_row(b), 0, 3)),
                  _resident(wq.shape), _resident(keys.shape)],
        out_specs=[pl.BlockSpec((1, tt, d), lambda b, i: (b, i, 0)), gate_spec, gate_spec,
                   pl.BlockSpec((1, PEER_HEADS, tt), lambda b, i: (b, 0, i))],
        scratch_shapes=[pltpu.VMEM((PEER_KEYS, tt), F32),
                        pltpu.VMEM((24, tt), F32),
                        pltpu.VMEM((PEER_TOPK * PEER_TOPK + 8, tt), F32)],
        compiler_params=_params(("parallel", "parallel")),
        name="peer_route",
    )(x, g2, mod3, mod3, wq, keys)


def _expert_kernel(x_ref, gt_ref, gf_ref, h_ref, u_ref, v_ref, e1_ref, e2_ref, thr_ref, o_ref,
                   acc_ref, a_ref, w_ref, *, final_norm):
    eb = pl.program_id(2)
    tt = h_ref.shape[1]

    @pl.when(eb == 0)
    def _zero():
        acc_ref[...] = jnp.zeros_like(acc_ref)

    a_ref[...] = _nt(u_ref[...], h_ref[0])
    sub = 64
    n_i = TE_PEER // PEER_KEYS
    i0 = pl.multiple_of(eb * n_i, n_i)
    for il in range(n_i):
        for tc in range(tt // LANES):
            lanes = slice(tc * LANES, (tc + 1) * LANES)
            e1_rows = [e1_ref[0, h, pl.ds(i0, n_i), lanes][il:il + 1] for h in range(PEER_HEADS)]
            thr_rows = [thr_ref[0, h:h + 1, lanes] for h in range(PEER_HEADS)]
            for js in range(PEER_KEYS // sub):
                rows = slice(js * sub, (js + 1) * sub)
                gate = jnp.zeros((sub, LANES), F32)
                for h in range(PEER_HEADS):
                    prod = e2_ref[0, h, rows, lanes] * e1_rows[h]
                    gate = gate + jnp.where(prod >= thr_rows[h], prod, 0.0)
                erows = slice(il * PEER_KEYS + js * sub, il * PEER_KEYS + (js + 1) * sub)
                a = a_ref[erows, lanes]
                act = (0.5 * a) * (1.0 + lax.erf(a * math.sqrt(0.5)))
                w_ref[erows, lanes] = gate * act
    w = w_ref[...].T.astype(MXU_DTYPE)
    acc_ref[...] += _nn(w, v_ref[...])

    @pl.when(eb == pl.num_programs(2) - 1)
    def _finish():
        y = x_ref[0] + gt_ref[0] * acc_ref[...]
        if final_norm:
            y = (y * lax.rsqrt(jnp.mean(y * y, axis=-1, keepdims=True) + NORM_EPS)) * gf_ref[...]
        o_ref[0] = y


def _experts(x, mod3, mod_row, g_final, h2, u, v, e1, e2, thr, final_norm):
    bsz, n, d = x.shape
    tt = min(TT_PEER, n)
    n_exp = u.shape[0]
    gate_spec = pl.BlockSpec((1, PEER_HEADS, PEER_KEYS, tt), lambda b, i, e: (b, 0, 0, i))
    return pl.pallas_call(
        functools.partial(_expert_kernel, final_norm=final_norm),
        out_shape=jax.ShapeDtypeStruct(x.shape, F32),
        grid=(bsz, n // tt, n_exp // TE_PEER),
        in_specs=[pl.BlockSpec((1, tt, d), lambda b, i, e: (b, i, 0)),
                  pl.BlockSpec((1, 1, d), lambda b, i, e: (mod_row(b), 0, 5)),
                  pl.BlockSpec((1, d), lambda b, i, e: (0, 0)),
                  pl.BlockSpec((1, tt, d), lambda b, i, e: (b, i, 0)),
                  pl.BlockSpec((TE_PEER, d), lambda b, i, e: (e, 0)),
                  pl.BlockSpec((TE_PEER, d), lambda b, i, e: (e, 0)),
                  gate_spec, gate_spec,
                  pl.BlockSpec((1, PEER_HEADS, tt), lambda b, i, e: (b, 0, i))],
        out_specs=pl.BlockSpec((1, tt, d), lambda b, i, e: (b, i, 0)),
        scratch_shapes=[pltpu.VMEM((tt, d), F32),
                        pltpu.VMEM((TE_PEER, tt), F32),
                        pltpu.VMEM((TE_PEER, tt), F32)],
        compiler_params=_params(("parallel", "parallel", "arbitrary")),
        name="peer_experts",
    )(x, mod3, g_final, h2, u, v, e1, e2, thr)


def _rope_tables(n):
    t = jnp.arange(n, dtype=jnp.int32)
    row = (t // GRID_W).astype(F32)
    col = (t % GRID_W).astype(F32)
    n_freq = HEAD_DIM // 4
    inv = ROPE_THETA ** (-jnp.arange(n_freq, dtype=F32) / n_freq)
    ang = jnp.concatenate([row[:, None] * inv, col[:, None] * inv], axis=-1)
    cos, sin = jnp.cos(ang), jnp.sin(ang)
    return jnp.tile(cos, (1, 4)), jnp.tile(jnp.concatenate([-sin, sin], axis=-1), (1, 2))


def kernel(x, c, ctx, c_ctx, w_mod, b_mod, g_norm1, g_norm2, w_in, a_sink, b_rpb, c_lambda,
           c_subln, w_branch_a, w_branch_b, w_branch_c, w_out, peer_wq, peer_keys, peer_u,
           peer_v, g_final):
    bsz, n, d = x.shape
    depth = w_mod.shape[0]
    n_ctx = ctx.shape[1]
    assert d == D_MODEL and bsz + 1 <= 8
    assert n % (B_ROWS * GRID_W) == 0 and n // GRID_W >= 4 * B_KROWS and n % TK_C == 0

    cvec = jnp.zeros((8, d), F32).at[:bsz].set(c).at[bsz].set(c_ctx)
    mod = _modulation(cvec, w_mod, b_mod)
    lat_row = lambda b: b
    ctx_row = lambda b: bsz
    cos_t, sin_t = _rope_tables(n)
    ones_t, zeros_t = jnp.ones((n_ctx, LANES), F32), jnp.zeros((n_ctx, LANES), F32)
    cast = lambda w: w.astype(MXU_DTYPE)
    gfin = g_final.reshape(1, d)

    xc = ctx
    for l in range(depth):
        last = l == depth - 1
        lam_init = 0.8 - 0.6 * math.exp(-0.3 * l)
        mod3 = mod[l].reshape(8, 1, 6 * d)
        g1, g2 = g_norm1[l].reshape(1, d), g_norm2[l].reshape(1, d)
        w_in_l = cast(w_in[l])
        subln3 = c_subln[l].reshape(C_HEADS, 1, LANES)
        wa, wb, wc, wo = cast(w_branch_a[l]), cast(w_branch_b[l]), cast(w_branch_c[l]), cast(w_out[l])
        wq = cast(peer_wq[l])
        keys = cast(peer_keys[l].reshape(2 * PEER_HEADS, PEER_KEYS, LANES))
        u, v = cast(peer_u[l]), cast(peer_v[l])

        lat = _in_proj(x, g1, mod3, lat_row, w_in_l, cos_t, sin_t)
        con = _in_proj(xc, g1, mod3, ctx_row, w_in_l, ones_t, zeros_t)
        (aq, ak, aks, av, avs, bq, bk, bv, cq, ck, cv, ga, gb, gc) = lat
        ya = _attn_a(a_sink[l], aq, ak, aks, av, avs, con[1], con[2], con[3], con[4])
        yb = _attn_b(b_rpb[l], bq, bk, bv, con[6], con[7])
        yc = _attn_c(c_lambda[l], subln3, cq, ck, cv, con[9], con[10], lam_init)
        x = _merge(x, mod3, lat_row, ya, yb, yc, ga, gb, gc, wa, wb, wc, wo)
        h2, e1, e2, thr = _route(x, g2, mod3, lat_row, wq, keys)
        x = _experts(x, mod3, lat_row, gfin, h2, u, v, e1, e2, thr, final_norm=last)

        if not last:
            ya_c, yb_c, yc_c = _ctx_attn(a_sink[l], c_lambda[l], subln3, con, lam_init)
            xc = _merge(xc, mod3, ctx_row, ya_c, yb_c, yc_c, con[11], con[12], con[13],
                        wa, wb, wc, wo)
            h2c, e1c, e2c, thrc = _route(xc, g2, mod3, ctx_row, wq, keys)
            xc = _experts(xc, mod3, ctx_row, gfin, h2c, u, v, e1c, e2c, thrc, final_norm=False)
    return x
```

```python
import functools
import math

import jax
import jax.numpy as jnp
import numpy as np
from jax import lax
from jax.experimental import pallas as pl
from jax.experimental.pallas import tpu as pltpu

F32 = jnp.float32
BF16 = jnp.bfloat16
MXU_DTYPE = BF16

LANES = 128
VMEM_LIMIT_BYTES = 56 * 1024 * 1024

HEAD_DIM = 64
GRID_W = 64
ROPE_THETA = 10000.0
NORM_EPS = 1e-6
NEG_INF = -1e30
A_HEADS, A_KV_HEADS, A_WINDOW, A_BLOCK = 8, 2, 128, 128
B_HEADS, NA_ROWS, NA_COLS = 8, 8, 16
C_HEADS = 4
PEER_HEADS, PEER_KEYS, PEER_TOPK = 8, 128, 16
D_MODEL = 1024
IN_SPLITS = (512, 128, 128, 512, 512, 512, 512, 512, 512, 1024, 1024, 1024)
IN_OFFS = tuple(int(v) for v in np.cumsum((0,) + IN_SPLITS))

TM_PROJ = 512
TQ_C = 512
TK_C = 1024
B_ROWS = 8
B_KROWS = 4
TT_PEER = 512
TE_PEER = 1024
TE_GROUP = 256
N_PT = 30


def _nt(a, b):
    return lax.dot_general(a, b, (((1,), (1,)), ((), ())), preferred_element_type=F32)


def _nn(a, b):
    return jnp.dot(a, b, preferred_element_type=F32)


def _params(sem):
    return pltpu.CompilerParams(dimension_semantics=sem, vmem_limit_bytes=VMEM_LIMIT_BYTES)


def _resident(shape):
    nd = len(shape)
    return pl.BlockSpec(shape, lambda *_: (0,) * nd, pipeline_mode=pl.Buffered(1))


def _mod_kernel(c_ref, w_ref, b_ref, o_ref):
    c = c_ref[...]
    s = c * jax.nn.sigmoid(c)
    w = w_ref[0]
    s_hi = s.astype(MXU_DTYPE)
    s_lo = (s - s_hi.astype(F32)).astype(MXU_DTYPE)
    w_hi = w.astype(MXU_DTYPE)
    w_lo = (w - w_hi.astype(F32)).astype(MXU_DTYPE)
    acc = _nn(s_hi, w_hi) + _nn(s_lo, w_hi) + _nn(s_hi, w_lo)
    o_ref[0] = acc + b_ref[0]


def _modulation(cvec, w_mod, b_mod):
    depth, d, n6 = w_mod.shape
    tn = 1536
    return pl.pallas_call(
        _mod_kernel,
        out_shape=jax.ShapeDtypeStruct((depth, 8, n6), F32),
        grid=(depth, n6 // tn),
        in_specs=[pl.BlockSpec((8, d), lambda l, j: (0, 0)),
                  pl.BlockSpec((1, d, tn), lambda l, j: (l, 0, j)),
                  pl.BlockSpec((1, 1, tn), lambda l, j: (l, 0, j))],
        out_specs=pl.BlockSpec((1, 8, tn), lambda l, j: (l, 0, j)),
        compiler_params=_params(("parallel", "parallel")),
        name="modulation",
    )(cvec, w_mod, b_mod.reshape(depth, 1, n6))


def _rms_mod(x, g, sc, sh):
    y = x * lax.rsqrt(jnp.mean(x * x, axis=-1, keepdims=True) + NORM_EPS)
    return (y * g) * (1.0 + sc) + sh


def _in_proj_kernel(x_ref, g_ref, sc_ref, sh_ref, w_ref, cos_ref, sin_ref,
                    aq_ref, ak_ref, aks_ref, av_ref, avs_ref, bq_ref, bk_ref, bv_ref,
                    cq_ref, ck_ref, cv_ref, ga_ref, gb_ref, gc_ref, cvt_ref):
    h = _rms_mod(x_ref[0], g_ref[...], sc_ref[0], sh_ref[0]).astype(MXU_DTYPE)
    cos = cos_ref[...]
    sin = sin_ref[...]
    lane = lax.broadcasted_iota(jnp.int32, (1, LANES), 1)
    first_half = (lane % HEAD_DIM) < (HEAD_DIM // 2)
    scale = HEAD_DIM ** -0.5

    def proj(seg, j):
        c0 = IN_OFFS[seg] + j * LANES
        return _nn(h, w_ref[:, c0:c0 + LANES])

    def rope(v):
        rot = jnp.where(first_half, pltpu.roll(v, LANES - 32, 1), pltpu.roll(v, 32, 1))
        return v * cos + rot * sin

    for j in range(4):
        aq_ref[0, :, j * LANES:(j + 1) * LANES] = (rope(proj(0, j)) * scale).astype(aq_ref.dtype)
        bq_ref[0, :, j * LANES:(j + 1) * LANES] = (proj(3, j) * scale).astype(bq_ref.dtype)
        bk_ref[0, :, j * LANES:(j + 1) * LANES] = proj(4, j).astype(bk_ref.dtype)
        bv_ref[0, :, j * LANES:(j + 1) * LANES] = proj(5, j).astype(bv_ref.dtype)
        cq_ref[0, :, j * LANES:(j + 1) * LANES] = (rope(proj(6, j)) * scale).astype(cq_ref.dtype)
        ck_ref[0, :, j * LANES:(j + 1) * LANES] = rope(proj(7, j)).astype(ck_ref.dtype)
        cv = proj(8, j)
        cv_ref[0, :, j * LANES:(j + 1) * LANES] = cv.astype(cv_ref.dtype)
        cvt_ref[0, j * LANES:(j + 1) * LANES, :] = cv.T.astype(cvt_ref.dtype)
    ak = rope(proj(1, 0))
    av = proj(2, 0)
    ak_ref[0] = ak.astype(ak_ref.dtype)
    aks_ref[0] = pltpu.roll(ak, HEAD_DIM, 1).astype(aks_ref.dtype)
    av_ref[0] = av.astype(av_ref.dtype)
    avs_ref[0] = pltpu.roll(av, HEAD_DIM, 1).astype(avs_ref.dtype)
    for seg, ref in ((9, ga_ref), (10, gb_ref), (11, gc_ref)):
        for j in range(D_MODEL // LANES):
            ref[0, :, j * LANES:(j + 1) * LANES] = jax.nn.sigmoid(proj(seg, j)).astype(ref.dtype)


def _in_proj(x, g1, mod3, mod_row, w_in, cos_t, sin_t):
    bsz, n, d = x.shape
    tm = min(TM_PROJ, n)
    tok = lambda w: pl.BlockSpec((1, tm, w), lambda b, i: (b, i, 0))
    widths = (512, 128, 128, 128, 128, 512, 512, 512, 512, 512, 512, 1024, 1024, 1024)
    return pl.pallas_call(
        _in_proj_kernel,
        out_shape=[jax.ShapeDtypeStruct((bsz, n, w), MXU_DTYPE) for w in widths]
        + [jax.ShapeDtypeStruct((bsz, 512, n), MXU_DTYPE)],
        grid=(bsz, n // tm),
        in_specs=[tok(d),
                  pl.BlockSpec((1, d), lambda b, i: (0, 0)),
                  pl.BlockSpec((1, 1, d), lambda b, i: (mod_row(b), 0, 1)),
                  pl.BlockSpec((1, 1, d), lambda b, i: (mod_row(b), 0, 0)),
                  _resident(w_in.shape),
                  pl.BlockSpec((tm, LANES), lambda b, i: (i, 0)),
                  pl.BlockSpec((tm, LANES), lambda b, i: (i, 0))],
        out_specs=[tok(w) for w in widths] + [pl.BlockSpec((1, 512, tm), lambda b, i: (b, 0, i))],
        compiler_params=_params(("parallel", "parallel")),
        name="in_proj",
    )(x, g1, mod3, mod3, w_in, cos_t, sin_t)


def _lane_lo():
    return lax.broadcasted_iota(jnp.int32, (1, LANES), 1) < HEAD_DIM


def _softmax_pv(s_list, v_list, extra_logit=None):
    m = functools.reduce(jnp.maximum, [jnp.max(s, axis=-1, keepdims=True) for s in s_list])
    if extra_logit is not None:
        m = jnp.maximum(m, extra_logit)
    l = 0.0 if extra_logit is None else jnp.exp(extra_logit - m)
    o = 0.0
    for s, v in zip(s_list, v_list):
        p = jnp.exp(s - m)
        l = l + jnp.sum(p, axis=-1, keepdims=True)
        o = o + _nn(p.astype(v.dtype), v)
    return o / l


def _attn_a_kernel(sink_ref, q_ref, kp_ref, kc_ref, kn_ref, ksp_ref, ksc_ref, ksn_ref,
                   vp_ref, vc_ref, vn_ref, vsp_ref, vsc_ref, vsn_ref,
                   xk_ref, xks_ref, xv_ref, xvs_ref, o_ref, *, n_tokens):
    i = pl.program_id(1)
    lo = _lane_lo()
    cat = lambda refs: jnp.concatenate([r[0] for r in refs], axis=0)
    k_loc, ks_loc = cat((kp_ref, kc_ref, kn_ref)), cat((ksp_ref, ksc_ref, ksn_ref))
    v_loc, vs_loc = cat((vp_ref, vc_ref, vn_ref)), cat((vsp_ref, vsc_ref, vsn_ref))
    zero = jnp.zeros((), k_loc.dtype)
    k_of = {(0, 0): (jnp.where(lo, k_loc, zero), jnp.where(lo, xk_ref[0], zero)),
            (0, 1): (jnp.where(lo, zero, ks_loc), jnp.where(lo, zero, xks_ref[0])),
            (1, 0): (jnp.where(lo, ks_loc, zero), jnp.where(lo, xks_ref[0], zero)),
            (1, 1): (jnp.where(lo, zero, k_loc), jnp.where(lo, zero, xk_ref[0]))}
    v_of = {(0, 0): (v_loc, xv_ref[0]), (0, 1): (vs_loc, xvs_ref[0]),
            (1, 0): (vs_loc, xvs_ref[0]), (1, 1): (v_loc, xv_ref[0])}
    qpos = i * A_BLOCK + lax.broadcasted_iota(jnp.int32, (A_BLOCK, 1), 0)
    kpos = (i - 1) * A_BLOCK + lax.broadcasted_iota(jnp.int32, (1, 3 * A_BLOCK), 1)
    dist = qpos - kpos
    valid = (jnp.maximum(dist, -dist) <= A_WINDOW) & (kpos >= 0) & (kpos < n_tokens)
    for hp in range(A_HEADS // 2):
        qp = q_ref[0, :, hp * LANES:(hp + 1) * LANES]
        outs = []
        for half in range(2):
            h = 2 * hp + half
            g = h // (A_HEADS // A_KV_HEADS)
            (kl, kx), (vl, vx) = k_of[(g, half)], v_of[(g, half)]
            s_loc = jnp.where(valid, _nt(qp, kl), NEG_INF)
            outs.append(_softmax_pv([s_loc, _nt(qp, kx)], [vl, vx], extra_logit=sink_ref[h]))
        o_ref[0, :, hp * LANES:(hp + 1) * LANES] = jnp.where(lo, outs[0], outs[1]).astype(o_ref.dtype)


def _attn_a(sink, aq, ak, aks, av, avs, xak, xaks, xav, xavs):
    bsz, n, _ = aq.shape
    nb = n // A_BLOCK
    ctx = xak.shape[1]
    prev = pl.BlockSpec((1, A_BLOCK, LANES), lambda b, i: (b, jnp.maximum(i - 1, 0), 0))
    cur = pl.BlockSpec((1, A_BLOCK, LANES), lambda b, i: (b, i, 0))
    nxt = pl.BlockSpec((1, A_BLOCK, LANES), lambda b, i: (b, jnp.minimum(i + 1, nb - 1), 0))
    cx = pl.BlockSpec((1, ctx, LANES), lambda b, i: (b, 0, 0))
    return pl.pallas_call(
        functools.partial(_attn_a_kernel, n_tokens=n),
        out_shape=jax.ShapeDtypeStruct(aq.shape, MXU_DTYPE),
        grid=(bsz, nb),
        in_specs=[pl.BlockSpec(memory_space=pltpu.SMEM),
                  pl.BlockSpec((1, A_BLOCK, 512), lambda b, i: (b, i, 0)),
                  prev, cur, nxt, prev, cur, nxt, prev, cur, nxt, prev, cur, nxt, cx, cx, cx, cx],
        out_specs=pl.BlockSpec((1, A_BLOCK, 512), lambda b, i: (b, i, 0)),
        compiler_params=_params(("parallel", "parallel")),
        name="attn_window",
    )(sink, aq, ak, ak, ak, aks, aks, aks, av, av, av, avs, avs, avs, xak, xaks, xav, xavs)


def _attn_b_kernel(rpb_ref, q_ref, k0_ref, k1_ref, k2_ref, k3_ref, v0_ref, v1_ref, v2_ref, v3_ref,
                   xk_ref, xv_ref, o_ref, pt_ref, *, n_rows):
    pair, i = pl.program_id(0), pl.program_id(2)
    n_a = 2 * NA_ROWS - 1
    n_b = 2 * NA_COLS - 1

    @pl.when((pl.program_id(1) == 0) & (i == 0))
    def _build_bias_tables():
        qc = lax.broadcasted_iota(jnp.int32, (GRID_W, LANES), 0)
        ln = lax.broadcasted_iota(jnp.int32, (GRID_W, LANES), 1)
        kc = ln % GRID_W
        hi = ln >= GRID_W
        cstart = jnp.clip(qc - NA_COLS // 2, 0, GRID_W - NA_COLS)
        col_ok = (kc >= cstart) & (kc < cstart + NA_COLS)
        d = kc - qc + (NA_COLS - 1)
        for hh in range(2):
            h = pair * 2 + hh

            def body(ai, carry):
                a = ai - 8
                a_ok, a1_ok = (a >= 0) & (a < n_a), (a + 1 >= 0) & (a + 1 < n_a)
                ra, ra1 = h * n_a + jnp.clip(a, 0, n_a - 1), h * n_a + jnp.clip(a + 1, 0, n_a - 1)
                t = jnp.full((GRID_W, LANES), NEG_INF, F32)
                for b in range(n_b):
                    va = jnp.where(a_ok, rpb_ref[ra, b], NEG_INF)
                    va1 = jnp.where(a1_ok, rpb_ref[ra1, b], NEG_INF)
                    t = jnp.where(d == b, jnp.where(hi, va1, va), t)
                pt_ref[hh, ai] = jnp.where(col_ok, t, NEG_INF)
                return carry

            lax.fori_loop(0, N_PT, body, 0)

    lo = _lane_lo()
    r0 = i * B_ROWS
    kb0 = jnp.clip(2 * i - 1, 0, n_rows // B_KROWS - 4)
    krow_lane = lax.broadcasted_iota(jnp.int32, (1, B_KROWS * GRID_W), 1) // GRID_W
    k_refs, v_refs = (k0_ref, k1_ref, k2_ref, k3_ref), (v0_ref, v1_ref, v2_ref, v3_ref)
    qp = q_ref[0]
    zero = jnp.zeros((), qp.dtype)
    outs = []
    for hh in range(2):
        keep = lambda t: jnp.where(lo, t, zero) if hh == 0 else jnp.where(lo, zero, t)
        s_tiles = []
        for j in range(4):
            s = _nt(qp, keep(k_refs[j][0, 0]))
            kr0 = (kb0 + j) * B_KROWS
            rows = []
            for qr in range(B_ROWS):
                r = r0 + qr
                rs = jnp.clip(r - NA_ROWS // 2, 0, n_rows - NA_ROWS)
                a0 = kr0 - r + (NA_ROWS - 1)
                bias = jnp.concatenate([pt_ref[hh, a0 + 8], pt_ref[hh, a0 + 10]], axis=1)
                row_ok = (kr0 + krow_lane >= rs) & (kr0 + krow_lane < rs + NA_ROWS)
                rows.append(jnp.where(row_ok, s[qr * GRID_W:(qr + 1) * GRID_W] + bias, NEG_INF))
            s_tiles.append(jnp.concatenate(rows, axis=0))
        s_tiles.append(_nt(qp, keep(xk_ref[0])))
        outs.append(_softmax_pv(s_tiles, [v_refs[j][0, 0] for j in range(4)] + [xv_ref[0]]))
    o_ref[0] = jnp.where(lo, outs[0], outs[1]).astype(o_ref.dtype)


def _attn_b(rpb, bq, bk, bv, xbk, xbv):
    bsz, n, _ = bq.shape
    n_rows = n // GRID_W
    tq = B_ROWS * GRID_W
    tkb = B_KROWS * GRID_W
    ctx = xbk.shape[1]
    bk4 = bk.reshape(bsz, n // tkb, tkb, 512)
    bv4 = bv.reshape(bsz, n // tkb, tkb, 512)
    nkb = n // tkb

    def kspec(j):
        return pl.BlockSpec((1, 1, tkb, LANES),
                            lambda p, b, i: (b, jnp.clip(2 * i - 1, 0, nkb - 4) + j, 0, p))

    cx = pl.BlockSpec((1, ctx, LANES), lambda p, b, i: (b, 0, p))
    return pl.pallas_call(
        functools.partial(_attn_b_kernel, n_rows=n_rows),
        out_shape=jax.ShapeDtypeStruct(bq.shape, MXU_DTYPE),
        grid=(B_HEADS // 2, bsz, n // tq),
        in_specs=[pl.BlockSpec(memory_space=pltpu.SMEM),
                  pl.BlockSpec((1, tq, LANES), lambda p, b, i: (b, i, p)),
                  kspec(0), kspec(1), kspec(2), kspec(3), kspec(0), kspec(1), kspec(2), kspec(3),
                  cx, cx],
        out_specs=pl.BlockSpec((1, tq, LANES), lambda p, b, i: (b, i, p)),
        scratch_shapes=[pltpu.VMEM((2, N_PT, GRID_W, LANES), F32)],
        compiler_params=_params(("arbitrary", "arbitrary", "arbitrary")),
        name="attn_neighbourhood",
    )(rpb.reshape(B_HEADS * (2 * NA_ROWS - 1), 2 * NA_COLS - 1), bq,
      bk4, bk4, bk4, bk4, bv4, bv4, bv4, bv4, xbk, xbv)


def _diff_lambda(lam_ref, lam_init):
    lam = lam_ref[...]
    a = jnp.sum(lam[0:1] * lam[1:2], axis=-1, keepdims=True)
    b = jnp.sum(lam[2:3] * lam[3:4], axis=-1, keepdims=True)
    return jnp.exp(a) - jnp.exp(b) + lam_init


def _head_norm(o, g, lam_init):
    y = o * lax.rsqrt(jnp.mean(o * o, axis=-1, keepdims=True) + NORM_EPS)
    return (y * g) * (1.0 - lam_init)


def _attn_c_kernel(lam_ref, g_ref, q_ref, k_ref, vt_ref, xk_ref, xvt_ref, o_ref, k1_ref, k2_ref,
                   *, lam_init, n_tokens):
    lo = _lane_lo()

    @pl.when(pl.program_id(2) == 0)
    def _split_keys():
        zero = jnp.zeros((), k1_ref.dtype)
        k1_ref[0:n_tokens] = jnp.where(lo, k_ref[0], zero)
        k2_ref[0:n_tokens] = jnp.where(lo, zero, k_ref[0])
        k1_ref[n_tokens:] = jnp.where(lo, xk_ref[0], zero)
        k2_ref[n_tokens:] = jnp.where(lo, zero, xk_ref[0])

    q = q_ref[0]
    tq = q.shape[0]

    def update(carry, ka, kb, vt):
        new = []
        for (m, l, acc), kk in zip(carry, (ka, kb)):
            s = _nt(kk, q)
            m_new = jnp.maximum(m, jnp.max(s, axis=0, keepdims=True))
            alpha = jnp.exp(m - m_new)
            p = jnp.exp(s - m_new)
            new.append((m_new, alpha * l + jnp.sum(p, axis=0, keepdims=True),
                        alpha * acc + _nn(vt, p.astype(vt.dtype))))
        return tuple(new)

    def body(c, carry):
        off = pl.multiple_of(c * TK_C, TK_C)
        return update(carry, k1_ref[pl.ds(off, TK_C), :], k2_ref[pl.ds(off, TK_C), :],
                      vt_ref[0, :, pl.ds(off, TK_C)])

    init = tuple((jnp.full((1, tq), -jnp.inf, F32), jnp.zeros((1, tq), F32),
                  jnp.zeros((LANES, tq), F32)) for _ in range(2))
    carry = lax.fori_loop(0, n_tokens // TK_C, body, init)
    (_, l1, acc1), (_, l2, acc2) = update(carry, k1_ref[n_tokens:], k2_ref[n_tokens:], xvt_ref[0])
    ot = acc1 / l1 - _diff_lambda(lam_ref, lam_init) * (acc2 / l2)
    o_ref[0] = _head_norm(ot.T, g_ref[0], lam_init).astype(o_ref.dtype)


def _attn_c(c_lambda, subln3, cq, ck, cvt, xck, xcvt, lam_init):
    bsz, n, _ = cq.shape
    ctx = xck.shape[1]
    tq = min(TQ_C, n)
    return pl.pallas_call(
        functools.partial(_attn_c_kernel, lam_init=lam_init, n_tokens=n),
        out_shape=jax.ShapeDtypeStruct(cq.shape, MXU_DTYPE),
        grid=(bsz, C_HEADS, n // tq),
        in_specs=[pl.BlockSpec((4, HEAD_DIM), lambda b, h, i: (0, 0)),
                  pl.BlockSpec((1, 1, LANES), lambda b, h, i: (h, 0, 0)),
                  pl.BlockSpec((1, tq, LANES), lambda b, h, i: (b, i, h)),
                  pl.BlockSpec((1, n, LANES), lambda b, h, i: (b, 0, h)),
                  pl.BlockSpec((1, LANES, n), lambda b, h, i: (b, h, 0)),
                  pl.BlockSpec((1, ctx, LANES), lambda b, h, i: (b, 0, h)),
                  pl.BlockSpec((1, LANES, ctx), lambda b, h, i: (b, h, 0))],
        out_specs=pl.BlockSpec((1, tq, LANES), lambda b, h, i: (b, i, h)),
        scratch_shapes=[pltpu.VMEM((n + ctx, LANES), MXU_DTYPE),
                        pltpu.VMEM((n + ctx, LANES), MXU_DTYPE)],
        compiler_params=_params(("arbitrary", "arbitrary", "arbitrary")),
        name="attn_differential",
    )(c_lambda, subln3, cq, ck, cvt, xck, xcvt)


def _ctx_attn_kernel(sink_ref, lam_ref, g_ref, aq_ref, ak_ref, aks_ref, av_ref, avs_ref,
                     bq_ref, bk_ref, bv_ref, cq_ref, ck_ref, cv_ref, ya_ref, yb_ref, yc_ref,
                     *, lam_init):
    lo = _lane_lo()
    zero = jnp.zeros((), ak_ref.dtype)
    sel = lambda t, half: jnp.where(lo, t, zero) if half == 0 else jnp.where(lo, zero, t)
    k_of = {(0, 0): sel(ak_ref[0], 0), (0, 1): sel(aks_ref[0], 1),
            (1, 0): sel(aks_ref[0], 0), (1, 1): sel(ak_ref[0], 1)}
    v_of = {(0, 0): av_ref[0], (0, 1): avs_ref[0], (1, 0): avs_ref[0], (1, 1): av_ref[0]}
    for hp in range(A_HEADS // 2):
        cols = slice(hp * LANES, (hp + 1) * LANES)
        qa, qb = aq_ref[0, :, cols], bq_ref[0, :, cols]
        kb, vb = bk_ref[0, :, cols], bv_ref[0, :, cols]
        oa, ob = [], []
        for half in range(2):
            h = 2 * hp + half
            g = h // (A_HEADS // A_KV_HEADS)
            oa.append(_softmax_pv([_nt(qa, k_of[(g, half)])], [v_of[(g, half)]],
                                  extra_logit=sink_ref[h]))
            ob.append(_softmax_pv([_nt(qb, sel(kb, half))], [vb]))
        ya_ref[0, :, cols] = jnp.where(lo, oa[0], oa[1]).astype(ya_ref.dtype)
        yb_ref[0, :, cols] = jnp.where(lo, ob[0], ob[1]).astype(yb_ref.dtype)
    lam = _diff_lambda(lam_ref, lam_init)
    for h in range(C_HEADS):
        cols = slice(h * LANES, (h + 1) * LANES)
        q, k, v = cq_ref[0, :, cols], ck_ref[0, :, cols], cv_ref[0, :, cols]
        o = _softmax_pv([_nt(q, sel(k, 0))], [v]) - lam * _softmax_pv([_nt(q, sel(k, 1))], [v])
        yc_ref[0, :, cols] = _head_norm(o, g_ref[h], lam_init).astype(yc_ref.dtype)


def _ctx_attn(sink, c_lambda, subln3, con, lam_init):
    aq, ak, aks, av, avs, bq, bk, bv, cq, ck, cv = con[:11]
    bsz, ctx, _ = aq.shape
    wide = pl.BlockSpec((1, ctx, 512), lambda b: (b, 0, 0))
    nar = pl.BlockSpec((1, ctx, LANES), lambda b: (b, 0, 0))
    return pl.pallas_call(
        functools.partial(_ctx_attn_kernel, lam_init=lam_init),
        out_shape=[jax.ShapeDtypeStruct(aq.shape, MXU_DTYPE)] * 3,
        grid=(bsz,),
        in_specs=[pl.BlockSpec(memory_space=pltpu.SMEM),
                  pl.BlockSpec((4, HEAD_DIM), lambda b: (0, 0)),
                  pl.BlockSpec((C_HEADS, 1, LANES), lambda b: (0, 0, 0)),
                  wide, nar, nar, nar, nar, wide, wide, wide, wide, wide, wide],
        out_specs=[wide] * 3,
        compiler_params=_params(("parallel",)),
        name="attn_context",
    )(sink, c_lambda, subln3, aq, ak, aks, av, avs, bq, bk, bv, cq, ck, cv)


def _merge_kernel(x_ref, gt_ref, ya_ref, yb_ref, yc_ref, ga_ref, gb_ref, gc_ref,
                  wa_ref, wb_ref, wc_ref, wo_ref, o_ref):
    m = (ga_ref[0].astype(F32) * _nn(ya_ref[0], wa_ref[...])
         + gb_ref[0].astype(F32) * _nn(yb_ref[0], wb_ref[...])
         + gc_ref[0].astype(F32) * _nn(yc_ref[0], wc_ref[...]))
    o_ref[0] = x_ref[0] + gt_ref[0] * _nn(m.astype(MXU_DTYPE), wo_ref[...])


def _merge(x, mod3, mod_row, ya, yb, yc, ga, gb, gc, wa, wb, wc, wo):
    bsz, n, d = x.shape
    tm = min(TM_PROJ, n)
    tok = lambda w: pl.BlockSpec((1, tm, w), lambda b, i: (b, i, 0))
    return pl.pallas_call(
        _merge_kernel,
        out_shape=jax.ShapeDtypeStruct(x.shape, F32),
        grid=(bsz, n // tm),
        in_specs=[tok(d), pl.BlockSpec((1, 1, d), lambda b, i: (mod_row(b), 0, 2)),
                  tok(512), tok(512), tok(512), tok(d), tok(d), tok(d),
                  _resident(wa.shape), _resident(wb.shape), _resident(wc.shape), _resident(wo.shape)],
        out_specs=tok(d),
        compiler_params=_params(("parallel", "parallel")),
        name="merge",
    )(x, mod3, ya, yb, yc, ga, gb, gc, wa, wb, wc, wo)


def _top_values(ref, row0, n_rows, count, out_ref, out_row0, floor):
    for k in range(count):
        cur = ref[row0:row0 + n_rows]
        m = jnp.max(cur, axis=0, keepdims=True)
        out_ref[out_row0 + k:out_row0 + k + 1] = m
        if k + 1 < count:
            ref[row0:row0 + n_rows] = jnp.where(cur == m, floor, cur)


def _batcher_network(n):
    pairs = []
    p = 1
    while p < n:
        k = p
        while k >= 1:
            for j in range(k % p, n - k, 2 * k):
                for i in range(min(k, n - j - k)):
                    if (i + j) // (2 * p) == (i + j + k) // (2 * p):
                        pairs.append((i + j, i + j + k))
            k //= 2
        p *= 2
    return tuple(pairs)


N_TOP = PEER_TOPK + 1
SUBLANES = 8
_SORT_NET = _batcher_network(PEER_KEYS // SUBLANES)
_CAND_COUNTS = tuple(N_TOP // (a + 1) for a in range(N_TOP))
_CAND_OFFS = tuple(int(v) for v in np.cumsum((0,) + _CAND_COUNTS))
N_CAND = -(-_CAND_OFFS[-1] // SUBLANES) * SUBLANES


def _top_sorted(s_ref, lanes, out_ref, slot):
    n_grp = PEER_KEYS // SUBLANES
    lists = [s_ref[r * SUBLANES:(r + 1) * SUBLANES, lanes] for r in range(n_grp)]
    for a, b in _SORT_NET:
        lists[a], lists[b] = jnp.maximum(lists[a], lists[b]), jnp.minimum(lists[a], lists[b])
    for t in range(N_TOP):
        head = lists[0]
        m = jnp.max(head, axis=0, keepdims=True)
        out_ref[slot, t:t + 1, lanes] = m
        remaining = N_TOP - 1 - t
        hit = head == m
        for k in range(remaining):
            below = lists[k + 1] if k + 1 < n_grp else -jnp.inf
            lists[k] = jnp.where(hit, below, lists[k])


def _route_kernel(x_ref, g_ref, sc_ref, sh_ref, wq_ref, keys_ref,
                  h_ref, e1_ref, e2_ref, thr_ref, s_ref, top_ref, cand_ref):
    tt = x_ref.shape[1]
    h2 = _rms_mod(x_ref[0], g_ref[...], sc_ref[0], sh_ref[0]).astype(MXU_DTYPE)
    h_ref[0] = h2
    q = _nn(h2, wq_ref[...]).astype(MXU_DTYPE)
    top_ref[:, N_TOP:, :] = jnp.full((3, top_ref.shape[1] - N_TOP, tt), -jnp.inf, F32)
    cand_ref[_CAND_OFFS[-1]:, :] = jnp.full((N_CAND - _CAND_OFFS[-1], tt), -1.0, F32)
    for h in range(PEER_HEADS):
        tops = []
        for p, e_ref in ((0, e1_ref), (1, e2_ref)):
            hp = 2 * h + p
            s_ref[...] = _nt(keys_ref[hp], q[:, hp * LANES:(hp + 1) * LANES])
            for tc in range(tt // LANES):
                _top_sorted(s_ref, slice(tc * LANES, (tc + 1) * LANES), top_ref, p)
            m = top_ref[p, 0:1]
            e_ref[0, h] = jnp.exp(s_ref[...] - m)
            tops.append(jnp.exp(top_ref[p] - m))
        e1_top, e2_top = tops
        for a in range(N_TOP):
            cand_ref[_CAND_OFFS[a]:_CAND_OFFS[a + 1]] = e1_top[a:a + 1] * e2_top[0:_CAND_COUNTS[a]]
        _top_values(cand_ref, 0, N_CAND, N_TOP, top_ref.at[2], 0, -1.0)
        best = top_ref[2]
        inv_z = 1.0 / jnp.sum(best[0:PEER_TOPK], axis=0, keepdims=True)
        e1_ref[0, h] = e1_ref[0, h] * inv_z
        thr_ref[0, h:h + 1] = 0.5 * (best[PEER_TOPK - 1:PEER_TOPK] + best[PEER_TOPK:N_TOP]) * inv_z


def _route(x, g2, mod3, mod_row, wq, keys):
    bsz, n, d = x.shape
    tt = min(TT_PEER, n)
    gate_shape = jax.ShapeDtypeStruct((bsz, PEER_HEADS, PEER_KEYS, n), F32)
    gate_spec = pl.BlockSpec((1, PEER_HEADS, PEER_KEYS, tt), lambda b, i: (b, 0, 0, i))
    return pl.pallas_call(
        _route_kernel,
        out_shape=[jax.ShapeDtypeStruct((bsz, n, d), MXU_DTYPE), gate_shape, gate_shape,
                   jax.ShapeDtypeStruct((bsz, PEER_HEADS, n), F32)],
        grid=(bsz, n // tt),
        in_specs=[pl.BlockSpec((1, tt, d), lambda b, i: (b, i, 0)),
                  pl.BlockSpec((1, d), lambda b, i: (0, 0)),
                  pl.BlockSpec((1, 1, d), lambda b, i: (mod_row(b), 0, 4)),
                  pl.BlockSpec((1, 1, d), lambda b, i: (mod_row(b), 0, 3)),
                  _resident(wq.shape), _resident(keys.shape)],
        out_specs=[pl.BlockSpec((1, tt, d), lambda b, i: (b, i, 0)), gate_spec, gate_spec,
                   pl.BlockSpec((1, PEER_HEADS, tt), lambda b, i: (b, 0, i))],
        scratch_shapes=[pltpu.VMEM((PEER_KEYS, tt), F32),
                        pltpu.VMEM((3, 3 * SUBLANES, tt), F32),
                        pltpu.VMEM((N_CAND, tt), F32)],
        compiler_params=_params(("parallel", "parallel")),
        name="peer_route",
    )(x, g2, mod3, mod3, wq, keys)


def _expert_kernel(x_ref, gt_ref, gf_ref, h_ref, u_ref, v_ref, e1_ref, e2_ref, thr_ref, o_ref,
                   acc_ref, a_ref, w_ref, *, final_norm):
    eb = pl.program_id(2)
    tt = h_ref.shape[1]

    @pl.when(eb == 0)
    def _zero():
        acc_ref[...] = jnp.zeros_like(acc_ref)

    n_i = TE_PEER // PEER_KEYS
    i0 = pl.multiple_of(eb * n_i, n_i)
    i_per_group = TE_GROUP // PEER_KEYS
    for g in range(TE_PEER // TE_GROUP):
        grows = slice(g * TE_GROUP, (g + 1) * TE_GROUP)
        a_ref[g] = _nt(u_ref[grows], h_ref[0])
        for ig in range(i_per_group):
            il = g * i_per_group + ig
            erows = slice(ig * PEER_KEYS, (ig + 1) * PEER_KEYS)
            for tc in range(tt // LANES):
                lanes = slice(tc * LANES, (tc + 1) * LANES)
                gate = None
                for h in range(PEER_HEADS):
                    e1_row = e1_ref[0, h, pl.ds(i0, n_i), lanes][il:il + 1]
                    prod = e2_ref[0, h, :, lanes] * e1_row
                    sel = jnp.where(prod >= thr_ref[0, h:h + 1, lanes], prod, 0.0)
                    gate = sel if gate is None else gate + sel
                a = a_ref[g, erows, lanes]
                act = (0.5 * a) * (1.0 + lax.erf(a * math.sqrt(0.5)))
                w_ref[g, erows, lanes] = gate * act
        w = w_ref[g].T.astype(MXU_DTYPE)
        acc_ref[...] += _nn(w, v_ref[grows])

    @pl.when(eb == pl.num_programs(2) - 1)
    def _finish():
        y = x_ref[0] + gt_ref[0] * acc_ref[...]
        if final_norm:
            y = (y * lax.rsqrt(jnp.mean(y * y, axis=-1, keepdims=True) + NORM_EPS)) * gf_ref[...]
        o_ref[0] = y


def _experts(x, mod3, mod_row, g_final, h2, u, v, e1, e2, thr, final_norm):
    bsz, n, d = x.shape
    tt = min(TT_PEER, n)
    n_exp = u.shape[0]
    gate_spec = pl.BlockSpec((1, PEER_HEADS, PEER_KEYS, tt), lambda b, i, e: (b, 0, 0, i))
    return pl.pallas_call(
        functools.partial(_expert_kernel, final_norm=final_norm),
        out_shape=jax.ShapeDtypeStruct(x.shape, F32),
        grid=(bsz, n // tt, n_exp // TE_PEER),
        in_specs=[pl.BlockSpec((1, tt, d), lambda b, i, e: (b, i, 0)),
                  pl.BlockSpec((1, 1, d), lambda b, i, e: (mod_row(b), 0, 5)),
                  pl.BlockSpec((1, d), lambda b, i, e: (0, 0)),
                  pl.BlockSpec((1, tt, d), lambda b, i, e: (b, i, 0)),
                  pl.BlockSpec((TE_PEER, d), lambda b, i, e: (e, 0)),
                  pl.BlockSpec((TE_PEER, d), lambda b, i, e: (e, 0)),
                  gate_spec, gate_spec,
                  pl.BlockSpec((1, PEER_HEADS, tt), lambda b, i, e: (b, 0, i))],
        out_specs=pl.BlockSpec((1, tt, d), lambda b, i, e: (b, i, 0)),
        scratch_shapes=[pltpu.VMEM((tt, d), F32),
                        pltpu.VMEM((TE_PEER // TE_GROUP, TE_GROUP, tt), F32),
                        pltpu.VMEM((TE_PEER // TE_GROUP, TE_GROUP, tt), F32)],
        compiler_params=_params(("parallel", "parallel", "arbitrary")),
        name="peer_experts",
    )(x, mod3, g_final, h2, u, v, e1, e2, thr)


def _rope_tables(n):
    t = jnp.arange(n, dtype=jnp.int32)
    row = (t // GRID_W).astype(F32)
    col = (t % GRID_W).astype(F32)
    n_freq = HEAD_DIM // 4
    inv = ROPE_THETA ** (-jnp.arange(n_freq, dtype=F32) / n_freq)
    ang = jnp.concatenate([row[:, None] * inv, col[:, None] * inv], axis=-1)
    cos, sin = jnp.cos(ang), jnp.sin(ang)
    return jnp.tile(cos, (1, 4)), jnp.tile(jnp.concatenate([-sin, sin], axis=-1), (1, 2))


def kernel(x, c, ctx, c_ctx, w_mod, b_mod, g_norm1, g_norm2, w_in, a_sink, b_rpb, c_lambda,
           c_subln, w_branch_a, w_branch_b, w_branch_c, w_out, peer_wq, peer_keys, peer_u,
           peer_v, g_final):
    bsz, n, d = x.shape
    depth = w_mod.shape[0]
    n_ctx = ctx.shape[1]
    assert d == D_MODEL and bsz + 1 <= 8
    assert n % (B_ROWS * GRID_W) == 0 and n // GRID_W >= 4 * B_KROWS and n % TK_C == 0

    cvec = jnp.zeros((8, d), F32).at[:bsz].set(c).at[bsz].set(c_ctx)
    mod = _modulation(cvec, w_mod, b_mod)
    lat_row = lambda b: b
    ctx_row = lambda b: bsz
    cos_t, sin_t = _rope_tables(n)
    ones_t, zeros_t = jnp.ones((n_ctx, LANES), F32), jnp.zeros((n_ctx, LANES), F32)
    cast = lambda w: w.astype(MXU_DTYPE)
    gfin = g_final.reshape(1, d)

    xc = ctx
    for l in range(depth):
        last = l == depth - 1
        lam_init = 0.8 - 0.6 * math.exp(-0.3 * l)
        mod3 = mod[l].reshape(8, 1, 6 * d)
        g1, g2 = g_norm1[l].reshape(1, d), g_norm2[l].reshape(1, d)
        w_in_l = cast(w_in[l])
        subln3 = c_subln[l].reshape(C_HEADS, 1, LANES)
        wa, wb, wc, wo = cast(w_branch_a[l]), cast(w_branch_b[l]), cast(w_branch_c[l]), cast(w_out[l])
        wq = cast(peer_wq[l])
        keys = cast(peer_keys[l].reshape(2 * PEER_HEADS, PEER_KEYS, LANES))
        u, v = cast(peer_u[l]), cast(peer_v[l])

        lat = _in_proj(x, g1, mod3, lat_row, w_in_l, cos_t, sin_t)
        con = _in_proj(xc, g1, mod3, ctx_row, w_in_l, ones_t, zeros_t)
        (aq, ak, aks, av, avs, bq, bk, bv, cq, ck, _, ga, gb, gc, cvt) = lat
        ya = _attn_a(a_sink[l], aq, ak, aks, av, avs, con[1], con[2], con[3], con[4])
        yb = _attn_b(b_rpb[l], bq, bk, bv, con[6], con[7])
        yc = _attn_c(c_lambda[l], subln3, cq, ck, cvt, con[9], con[14], lam_init)
        x = _merge(x, mod3, lat_row, ya, yb, yc, ga, gb, gc, wa, wb, wc, wo)
        h2, e1, e2, thr = _route(x, g2, mod3, lat_row, wq, keys)
        x = _experts(x, mod3, lat_row, gfin, h2, u, v, e1, e2, thr, final_norm=last)

        if not last:
            ya_c, yb_c, yc_c = _ctx_attn(a_sink[l], c_lambda[l], subln3, con, lam_init)
            xc = _merge(xc, mod3, ctx_row, ya_c, yb_c, yc_c, con[11], con[12], con[13],
                        wa, wb, wc, wo)
            h2c, e1c, e2c, thrc = _route(xc, g2, mod3, ctx_row, wq, keys)
            xc = _experts(xc, mod3, ctx_row, gfin, h2c, u, v, e1c, e2c, thrc, final_norm=False)
    return x
```

```python
import functools
import math

import jax
import jax.numpy as jnp
import numpy as np
from jax import lax
from jax.experimental import pallas as pl
from jax.experimental.pallas import tpu as pltpu

F32 = jnp.float32
BF16 = jnp.bfloat16
MXU_DTYPE = BF16

LANES = 128
VMEM_LIMIT_BYTES = 56 * 1024 * 1024

HEAD_DIM = 64
GRID_W = 64
ROPE_THETA = 10000.0
NORM_EPS = 1e-6
NEG_INF = -1e30
A_HEADS, A_KV_HEADS, A_WINDOW, A_BLOCK = 8, 2, 128, 128
B_HEADS, NA_ROWS, NA_COLS = 8, 8, 16
C_HEADS = 4
PEER_HEADS, PEER_KEYS, PEER_TOPK = 8, 128, 16
D_MODEL = 1024
IN_SPLITS = (512, 128, 128, 512, 512, 512, 512, 512, 512, 1024, 1024, 1024)
IN_OFFS = tuple(int(v) for v in np.cumsum((0,) + IN_SPLITS))

TM_PROJ = 512
TQ_C = 512
TK_C = 1024
B_ROWS = 8
B_KROWS = 4
TT_PEER = 512
TE_PEER = 1024
TE_GROUP = 256
N_PT = 30


def _nt(a, b):
    return lax.dot_general(a, b, (((1,), (1,)), ((), ())), preferred_element_type=F32)


def _nn(a, b):
    return jnp.dot(a, b, preferred_element_type=F32)


def _params(sem):
    return pltpu.CompilerParams(dimension_semantics=sem, vmem_limit_bytes=VMEM_LIMIT_BYTES)


def _resident(shape):
    nd = len(shape)
    return pl.BlockSpec(shape, lambda *_: (0,) * nd, pipeline_mode=pl.Buffered(1))


def _mod_kernel(c_ref, w_ref, b_ref, o_ref):
    c = c_ref[...]
    s = c * jax.nn.sigmoid(c)
    w = w_ref[0]
    s_hi = s.astype(MXU_DTYPE)
    s_lo = (s - s_hi.astype(F32)).astype(MXU_DTYPE)
    w_hi = w.astype(MXU_DTYPE)
    w_lo = (w - w_hi.astype(F32)).astype(MXU_DTYPE)
    acc = _nn(s_hi, w_hi) + _nn(s_lo, w_hi) + _nn(s_hi, w_lo)
    o_ref[0] = acc + b_ref[0]


def _modulation(cvec, w_mod, b_mod):
    depth, d, n6 = w_mod.shape
    tn = 1536
    return pl.pallas_call(
        _mod_kernel,
        out_shape=jax.ShapeDtypeStruct((depth, 8, n6), F32),
        grid=(depth, n6 // tn),
        in_specs=[pl.BlockSpec((8, d), lambda l, j: (0, 0)),
                  pl.BlockSpec((1, d, tn), lambda l, j: (l, 0, j)),
                  pl.BlockSpec((1, 1, tn), lambda l, j: (l, 0, j))],
        out_specs=pl.BlockSpec((1, 8, tn), lambda l, j: (l, 0, j)),
        compiler_params=_params(("parallel", "parallel")),
        name="modulation",
    )(cvec, w_mod, b_mod.reshape(depth, 1, n6))


def _rms_mod(x, g, sc, sh):
    y = x * lax.rsqrt(jnp.mean(x * x, axis=-1, keepdims=True) + NORM_EPS)
    return (y * g) * (1.0 + sc) + sh


def _in_proj_kernel(x_ref, g_ref, sc_ref, sh_ref, w_ref, cos_ref, sin_ref,
                    aq_ref, ak_ref, aks_ref, av_ref, avs_ref, bq_ref, bk_ref, bv_ref,
                    cq_ref, ck_ref, cv_ref, ga_ref, gb_ref, gc_ref, cvt_ref):
    h = _rms_mod(x_ref[0], g_ref[...], sc_ref[0], sh_ref[0]).astype(MXU_DTYPE)
    cos = cos_ref[...]
    sin = sin_ref[...]
    lane = lax.broadcasted_iota(jnp.int32, (1, LANES), 1)
    first_half = (lane % HEAD_DIM) < (HEAD_DIM // 2)
    scale = HEAD_DIM ** -0.5

    def proj(seg, j):
        c0 = IN_OFFS[seg] + j * LANES
        return _nn(h, w_ref[:, c0:c0 + LANES])

    def rope(v):
        rot = jnp.where(first_half, pltpu.roll(v, LANES - 32, 1), pltpu.roll(v, 32, 1))
        return v * cos + rot * sin

    for j in range(4):
        aq_ref[0, :, j * LANES:(j + 1) * LANES] = (rope(proj(0, j)) * scale).astype(aq_ref.dtype)
        bq_ref[0, :, j * LANES:(j + 1) * LANES] = (proj(3, j) * scale).astype(bq_ref.dtype)
        bk_ref[0, :, j * LANES:(j + 1) * LANES] = proj(4, j).astype(bk_ref.dtype)
        bv_ref[0, :, j * LANES:(j + 1) * LANES] = proj(5, j).astype(bv_ref.dtype)
        cq_ref[0, :, j * LANES:(j + 1) * LANES] = (rope(proj(6, j)) * scale).astype(cq_ref.dtype)
        ck_ref[0, :, j * LANES:(j + 1) * LANES] = rope(proj(7, j)).astype(ck_ref.dtype)
        cv = proj(8, j)
        cv_ref[0, :, j * LANES:(j + 1) * LANES] = cv.astype(cv_ref.dtype)
        cvt_ref[0, j * LANES:(j + 1) * LANES, :] = cv.T.astype(cvt_ref.dtype)
    ak = rope(proj(1, 0))
    av = proj(2, 0)
    ak_ref[0] = ak.astype(ak_ref.dtype)
    aks_ref[0] = pltpu.roll(ak, HEAD_DIM, 1).astype(aks_ref.dtype)
    av_ref[0] = av.astype(av_ref.dtype)
    avs_ref[0] = pltpu.roll(av, HEAD_DIM, 1).astype(avs_ref.dtype)
    for seg, ref in ((9, ga_ref), (10, gb_ref), (11, gc_ref)):
        for j in range(D_MODEL // LANES):
            ref[0, :, j * LANES:(j + 1) * LANES] = jax.nn.sigmoid(proj(seg, j)).astype(ref.dtype)


def _in_proj(x, g1, mod3, mod_row, w_in, cos_t, sin_t):
    bsz, n, d = x.shape
    tm = min(TM_PROJ, n)
    tok = lambda w: pl.BlockSpec((1, tm, w), lambda b, i: (b, i, 0))
    widths = (512, 128, 128, 128, 128, 512, 512, 512, 512, 512, 512, 1024, 1024, 1024)
    return pl.pallas_call(
        _in_proj_kernel,
        out_shape=[jax.ShapeDtypeStruct((bsz, n, w), MXU_DTYPE) for w in widths]
        + [jax.ShapeDtypeStruct((bsz, 512, n), MXU_DTYPE)],
        grid=(bsz, n // tm),
        in_specs=[tok(d),
                  pl.BlockSpec((1, d), lambda b, i: (0, 0)),
                  pl.BlockSpec((1, 1, d), lambda b, i: (mod_row(b), 0, 1)),
                  pl.BlockSpec((1, 1, d), lambda b, i: (mod_row(b), 0, 0)),
                  _resident(w_in.shape),
                  pl.BlockSpec((tm, LANES), lambda b, i: (i, 0)),
                  pl.BlockSpec((tm, LANES), lambda b, i: (i, 0))],
        out_specs=[tok(w) for w in widths] + [pl.BlockSpec((1, 512, tm), lambda b, i: (b, 0, i))],
        compiler_params=_params(("parallel", "parallel")),
        name="in_proj",
    )(x, g1, mod3, mod3, w_in, cos_t, sin_t)


def _lane_lo():
    return lax.broadcasted_iota(jnp.int32, (1, LANES), 1) < HEAD_DIM


def _softmax_pv(s_list, v_list, extra_logit=None):
    m = functools.reduce(jnp.maximum, [jnp.max(s, axis=-1, keepdims=True) for s in s_list])
    if extra_logit is not None:
        m = jnp.maximum(m, extra_logit)
    l = 0.0 if extra_logit is None else jnp.exp(extra_logit - m)
    o = 0.0
    for s, v in zip(s_list, v_list):
        p = jnp.exp(s - m)
        l = l + jnp.sum(p, axis=-1, keepdims=True)
        o = o + _nn(p.astype(v.dtype), v)
    return o / l


def _attn_a_kernel(sink_ref, q_ref, kp_ref, kc_ref, kn_ref, ksp_ref, ksc_ref, ksn_ref,
                   vp_ref, vc_ref, vn_ref, vsp_ref, vsc_ref, vsn_ref,
                   xk_ref, xks_ref, xv_ref, xvs_ref, o_ref, *, n_tokens):
    i = pl.program_id(1)
    lo = _lane_lo()
    cat = lambda refs: jnp.concatenate([r[0] for r in refs], axis=0)
    k_loc, ks_loc = cat((kp_ref, kc_ref, kn_ref)), cat((ksp_ref, ksc_ref, ksn_ref))
    v_loc, vs_loc = cat((vp_ref, vc_ref, vn_ref)), cat((vsp_ref, vsc_ref, vsn_ref))
    zero = jnp.zeros((), k_loc.dtype)
    k_of = {(0, 0): (jnp.where(lo, k_loc, zero), jnp.where(lo, xk_ref[0], zero)),
            (0, 1): (jnp.where(lo, zero, ks_loc), jnp.where(lo, zero, xks_ref[0])),
            (1, 0): (jnp.where(lo, ks_loc, zero), jnp.where(lo, xks_ref[0], zero)),
            (1, 1): (jnp.where(lo, zero, k_loc), jnp.where(lo, zero, xk_ref[0]))}
    v_of = {(0, 0): (v_loc, xv_ref[0]), (0, 1): (vs_loc, xvs_ref[0]),
            (1, 0): (vs_loc, xvs_ref[0]), (1, 1): (v_loc, xv_ref[0])}
    qpos = i * A_BLOCK + lax.broadcasted_iota(jnp.int32, (A_BLOCK, 1), 0)
    kpos = (i - 1) * A_BLOCK + lax.broadcasted_iota(jnp.int32, (1, 3 * A_BLOCK), 1)
    dist = qpos - kpos
    valid = (jnp.maximum(dist, -dist) <= A_WINDOW) & (kpos >= 0) & (kpos < n_tokens)
    for hp in range(A_HEADS // 2):
        qp = q_ref[0, :, hp * LANES:(hp + 1) * LANES]
        outs = []
        for half in range(2):
            h = 2 * hp + half
            g = h // (A_HEADS // A_KV_HEADS)
            (kl, kx), (vl, vx) = k_of[(g, half)], v_of[(g, half)]
            s_loc = jnp.where(valid, _nt(qp, kl), NEG_INF)
            outs.append(_softmax_pv([s_loc, _nt(qp, kx)], [vl, vx], extra_logit=sink_ref[h]))
        o_ref[0, :, hp * LANES:(hp + 1) * LANES] = jnp.where(lo, outs[0], outs[1]).astype(o_ref.dtype)


def _attn_a(sink, aq, ak, aks, av, avs, xak, xaks, xav, xavs):
    bsz, n, _ = aq.shape
    nb = n // A_BLOCK
    ctx = xak.shape[1]
    prev = pl.BlockSpec((1, A_BLOCK, LANES), lambda b, i: (b, jnp.maximum(i - 1, 0), 0))
    cur = pl.BlockSpec((1, A_BLOCK, LANES), lambda b, i: (b, i, 0))
    nxt = pl.BlockSpec((1, A_BLOCK, LANES), lambda b, i: (b, jnp.minimum(i + 1, nb - 1), 0))
    cx = pl.BlockSpec((1, ctx, LANES), lambda b, i: (b, 0, 0))
    return pl.pallas_call(
        functools.partial(_attn_a_kernel, n_tokens=n),
        out_shape=jax.ShapeDtypeStruct(aq.shape, MXU_DTYPE),
        grid=(bsz, nb),
        in_specs=[pl.BlockSpec(memory_space=pltpu.SMEM),
                  pl.BlockSpec((1, A_BLOCK, 512), lambda b, i: (b, i, 0)),
                  prev, cur, nxt, prev, cur, nxt, prev, cur, nxt, prev, cur, nxt, cx, cx, cx, cx],
        out_specs=pl.BlockSpec((1, A_BLOCK, 512), lambda b, i: (b, i, 0)),
        compiler_params=_params(("parallel", "parallel")),
        name="attn_window",
    )(sink, aq, ak, ak, ak, aks, aks, aks, av, av, av, avs, avs, avs, xak, xaks, xav, xavs)


def _attn_b_kernel(rpb_ref, q_ref, k0_ref, k1_ref, k2_ref, k3_ref, v0_ref, v1_ref, v2_ref, v3_ref,
                   xk_ref, xv_ref, o_ref, pt_ref, *, n_rows):
    pair, i = pl.program_id(0), pl.program_id(2)
    n_a = 2 * NA_ROWS - 1
    n_b = 2 * NA_COLS - 1

    @pl.when((pl.program_id(1) == 0) & (i == 0))
    def _build_bias_tables():
        qc = lax.broadcasted_iota(jnp.int32, (GRID_W, LANES), 0)
        ln = lax.broadcasted_iota(jnp.int32, (GRID_W, LANES), 1)
        kc = ln % GRID_W
        hi = ln >= GRID_W
        cstart = jnp.clip(qc - NA_COLS // 2, 0, GRID_W - NA_COLS)
        col_ok = (kc >= cstart) & (kc < cstart + NA_COLS)
        d = kc - qc + (NA_COLS - 1)
        for hh in range(2):
            h = pair * 2 + hh

            def body(ai, carry):
                a = ai - 8
                a_ok, a1_ok = (a >= 0) & (a < n_a), (a + 1 >= 0) & (a + 1 < n_a)
                ra, ra1 = h * n_a + jnp.clip(a, 0, n_a - 1), h * n_a + jnp.clip(a + 1, 0, n_a - 1)
                t = jnp.full((GRID_W, LANES), NEG_INF, F32)
                for b in range(n_b):
                    va = jnp.where(a_ok, rpb_ref[ra, b], NEG_INF)
                    va1 = jnp.where(a1_ok, rpb_ref[ra1, b], NEG_INF)
                    t = jnp.where(d == b, jnp.where(hi, va1, va), t)
                pt_ref[hh, ai] = jnp.where(col_ok, t, NEG_INF)
                return carry

            lax.fori_loop(0, N_PT, body, 0)

    lo = _lane_lo()
    r0 = i * B_ROWS
    kb0 = jnp.clip(2 * i - 1, 0, n_rows // B_KROWS - 4)
    krow_lane = lax.broadcasted_iota(jnp.int32, (1, B_KROWS * GRID_W), 1) // GRID_W
    k_refs, v_refs = (k0_ref, k1_ref, k2_ref, k3_ref), (v0_ref, v1_ref, v2_ref, v3_ref)
    qp = q_ref[0]
    zero = jnp.zeros((), qp.dtype)
    outs = []
    for hh in range(2):
        keep = lambda t: jnp.where(lo, t, zero) if hh == 0 else jnp.where(lo, zero, t)
        s_tiles = []
        for j in range(4):
            s = _nt(qp, keep(k_refs[j][0, 0]))
            kr0 = (kb0 + j) * B_KROWS
            rows = []
            for qr in range(B_ROWS):
                r = r0 + qr
                rs = jnp.clip(r - NA_ROWS // 2, 0, n_rows - NA_ROWS)
                a0 = kr0 - r + (NA_ROWS - 1)
                bias = jnp.concatenate([pt_ref[hh, a0 + 8], pt_ref[hh, a0 + 10]], axis=1)
                row_ok = (kr0 + krow_lane >= rs) & (kr0 + krow_lane < rs + NA_ROWS)
                rows.append(jnp.where(row_ok, s[qr * GRID_W:(qr + 1) * GRID_W] + bias, NEG_INF))
            s_tiles.append(jnp.concatenate(rows, axis=0))
        s_tiles.append(_nt(qp, keep(xk_ref[0])))
        outs.append(_softmax_pv(s_tiles, [v_refs[j][0, 0] for j in range(4)] + [xv_ref[0]]))
    o_ref[0] = jnp.where(lo, outs[0], outs[1]).astype(o_ref.dtype)


def _attn_b(rpb, bq, bk, bv, xbk, xbv):
    bsz, n, _ = bq.shape
    n_rows = n // GRID_W
    tq = B_ROWS * GRID_W
    tkb = B_KROWS * GRID_W
    ctx = xbk.shape[1]
    bk4 = bk.reshape(bsz, n // tkb, tkb, 512)
    bv4 = bv.reshape(bsz, n // tkb, tkb, 512)
    nkb = n // tkb

    def kspec(j):
        return pl.BlockSpec((1, 1, tkb, LANES),
                            lambda p, b, i: (b, jnp.clip(2 * i - 1, 0, nkb - 4) + j, 0, p))

    cx = pl.BlockSpec((1, ctx, LANES), lambda p, b, i: (b, 0, p))
    return pl.pallas_call(
        functools.partial(_attn_b_kernel, n_rows=n_rows),
        out_shape=jax.ShapeDtypeStruct(bq.shape, MXU_DTYPE),
        grid=(B_HEADS // 2, bsz, n // tq),
        in_specs=[pl.BlockSpec(memory_space=pltpu.SMEM),
                  pl.BlockSpec((1, tq, LANES), lambda p, b, i: (b, i, p)),
                  kspec(0), kspec(1), kspec(2), kspec(3), kspec(0), kspec(1), kspec(2), kspec(3),
                  cx, cx],
        out_specs=pl.BlockSpec((1, tq, LANES), lambda p, b, i: (b, i, p)),
        scratch_shapes=[pltpu.VMEM((2, N_PT, GRID_W, LANES), F32)],
        compiler_params=_params(("arbitrary", "arbitrary", "arbitrary")),
        name="attn_neighbourhood",
    )(rpb.reshape(B_HEADS * (2 * NA_ROWS - 1), 2 * NA_COLS - 1), bq,
      bk4, bk4, bk4, bk4, bv4, bv4, bv4, bv4, xbk, xbv)


def _diff_lambda(lam_ref, lam_init):
    lam = lam_ref[...]
    a = jnp.sum(lam[0:1] * lam[1:2], axis=-1, keepdims=True)
    b = jnp.sum(lam[2:3] * lam[3:4], axis=-1, keepdims=True)
    return jnp.exp(a) - jnp.exp(b) + lam_init


def _head_norm(o, g, lam_init):
    y = o * lax.rsqrt(jnp.mean(o * o, axis=-1, keepdims=True) + NORM_EPS)
    return (y * g) * (1.0 - lam_init)


def _attn_c_kernel(lam_ref, g_ref, q_ref, k_ref, vt_ref, xk_ref, xvt_ref, o_ref,
                   k1_ref, k2_ref, vall_ref, s_ref, p_ref, stat_ref, acc_ref,
                   *, lam_init, n_tokens, tk, n_chunks):
    lo = _lane_lo()

    @pl.when(pl.program_id(2) == 0)
    def _split_keys():
        zero = jnp.zeros((), k1_ref.dtype)
        k1_ref[0:n_tokens] = jnp.where(lo, k_ref[0], zero)
        k2_ref[0:n_tokens] = jnp.where(lo, zero, k_ref[0])
        k1_ref[n_tokens:] = jnp.where(lo, xk_ref[0], zero)
        k2_ref[n_tokens:] = jnp.where(lo, zero, xk_ref[0])

        vall_ref[:, 0:n_tokens] = vt_ref[0]
        vall_ref[:, n_tokens:] = xvt_ref[0]

    q = q_ref[0]
    tq = q.shape[0]
    key_refs = (k1_ref, k2_ref)
    stat_ref[:, 0:1, :] = jnp.full((2, 1, tq), -jnp.inf, F32)
    stat_ref[:, 1:2, :] = jnp.zeros((2, 1, tq), F32)
    acc_ref[...] = jnp.zeros_like(acc_ref)

    def scores(t, slot):
        off = pl.multiple_of(t * tk, LANES)
        for c in range(2):
            s_ref[slot, c] = _nt(key_refs[c][pl.ds(off, tk), :], q)

    def softmax(slot):
        for c in range(2):
            s = s_ref[slot, c]
            m_old = stat_ref[c, 0:1, :]
            m_new = jnp.maximum(m_old, jnp.max(s, axis=0, keepdims=True))
            alpha = jnp.exp(m_old - m_new)
            p = jnp.exp(s - m_new)
            p_ref[slot, c] = p.astype(p_ref.dtype)
            stat_ref[c, 0:1, :] = m_new
            stat_ref[c, 1:2, :] = alpha * stat_ref[c, 1:2, :] + jnp.sum(p, axis=0, keepdims=True)
            stat_ref[c, 2 + slot:3 + slot, :] = alpha

    def values(t, slot):
        off = pl.multiple_of(t * tk, LANES)
        vt = vall_ref[:, pl.ds(off, tk)]
        for c in range(2):
            acc_ref[c] = stat_ref[c, 2 + slot:3 + slot, :] * acc_ref[c] + _nn(vt, p_ref[slot, c])

    def iteration(t, parity):
        if not isinstance(t, int) or t < n_chunks:
            scores(t, parity)
        if not isinstance(t, int) or t >= 2:
            values(t - 2, parity)
        if not isinstance(t, int) or 1 <= t <= n_chunks:
            softmax(1 - parity)

    iteration(0, 0)
    iteration(1, 1)
    n_mid = n_chunks - 2

    def pair(i, carry):
        iteration(2 + 2 * i, 0)
        iteration(3 + 2 * i, 1)
        return carry

    lax.fori_loop(0, n_mid // 2, pair, 0)
    for t in range(2 + 2 * (n_mid // 2), n_chunks + 2):
        iteration(t, t % 2)
    inv_l = 1.0 / stat_ref[:, 1:2, :]
    ot = acc_ref[0] * inv_l[0] - _diff_lambda(lam_ref, lam_init) * (acc_ref[1] * inv_l[1])
    o_ref[0] = _head_norm(ot.T, g_ref[0], lam_init).astype(o_ref.dtype)


def _key_chunk(n_keys):
    return max(c for c in range(LANES, TK_C + 1, LANES) if n_keys % c == 0)


def _attn_c(c_lambda, subln3, cq, ck, cvt, xck, xcvt, lam_init):
    bsz, n, _ = cq.shape
    ctx = xck.shape[1]
    tq = min(TQ_C, n)
    tk = _key_chunk(n + ctx)
    return pl.pallas_call(
        functools.partial(_attn_c_kernel, lam_init=lam_init, n_tokens=n, tk=tk,
                          n_chunks=(n + ctx) // tk),
        out_shape=jax.ShapeDtypeStruct(cq.shape, MXU_DTYPE),
        grid=(bsz, C_HEADS, n // tq),
        in_specs=[pl.BlockSpec((4, HEAD_DIM), lambda b, h, i: (0, 0)),
                  pl.BlockSpec((1, 1, LANES), lambda b, h, i: (h, 0, 0)),
                  pl.BlockSpec((1, tq, LANES), lambda b, h, i: (b, i, h)),
                  pl.BlockSpec((1, n, LANES), lambda b, h, i: (b, 0, h)),
                  pl.BlockSpec((1, LANES, n), lambda b, h, i: (b, h, 0)),
                  pl.BlockSpec((1, ctx, LANES), lambda b, h, i: (b, 0, h)),
                  pl.BlockSpec((1, LANES, ctx), lambda b, h, i: (b, h, 0))],
        out_specs=pl.BlockSpec((1, tq, LANES), lambda b, h, i: (b, i, h)),
        scratch_shapes=[pltpu.VMEM((n + ctx, LANES), MXU_DTYPE),
                        pltpu.VMEM((n + ctx, LANES), MXU_DTYPE),
                        pltpu.VMEM((LANES, n + ctx), MXU_DTYPE),
                        pltpu.VMEM((2, 2, tk, tq), F32),
                        pltpu.VMEM((2, 2, tk, tq), MXU_DTYPE),
                        pltpu.VMEM((2, 8, tq), F32),
                        pltpu.VMEM((2, LANES, tq), F32)],
        compiler_params=_params(("arbitrary", "arbitrary", "arbitrary")),
        name="attn_differential",
    )(c_lambda, subln3, cq, ck, cvt, xck, xcvt)


def _ctx_attn_kernel(sink_ref, lam_ref, g_ref, aq_ref, ak_ref, aks_ref, av_ref, avs_ref,
                     bq_ref, bk_ref, bv_ref, cq_ref, ck_ref, cv_ref, ya_ref, yb_ref, yc_ref,
                     *, lam_init):
    lo = _lane_lo()
    zero = jnp.zeros((), ak_ref.dtype)
    sel = lambda t, half: jnp.where(lo, t, zero) if half == 0 else jnp.where(lo, zero, t)
    k_of = {(0, 0): sel(ak_ref[0], 0), (0, 1): sel(aks_ref[0], 1),
            (1, 0): sel(aks_ref[0], 0), (1, 1): sel(ak_ref[0], 1)}
    v_of = {(0, 0): av_ref[0], (0, 1): avs_ref[0], (1, 0): avs_ref[0], (1, 1): av_ref[0]}
    for hp in range(A_HEADS // 2):
        cols = slice(hp * LANES, (hp + 1) * LANES)
        qa, qb = aq_ref[0, :, cols], bq_ref[0, :, cols]
        kb, vb = bk_ref[0, :, cols], bv_ref[0, :, cols]
        oa, ob = [], []
        for half in range(2):
            h = 2 * hp + half
            g = h // (A_HEADS // A_KV_HEADS)
            oa.append(_softmax_pv([_nt(qa, k_of[(g, half)])], [v_of[(g, half)]],
                                  extra_logit=sink_ref[h]))
            ob.append(_softmax_pv([_nt(qb, sel(kb, half))], [vb]))
        ya_ref[0, :, cols] = jnp.where(lo, oa[0], oa[1]).astype(ya_ref.dtype)
        yb_ref[0, :, cols] = jnp.where(lo, ob[0], ob[1]).astype(yb_ref.dtype)
    lam = _diff_lambda(lam_ref, lam_init)
    for h in range(C_HEADS):
        cols = slice(h * LANES, (h + 1) * LANES)
        q, k, v = cq_ref[0, :, cols], ck_ref[0, :, cols], cv_ref[0, :, cols]
        o = _softmax_pv([_nt(q, sel(k, 0))], [v]) - lam * _softmax_pv([_nt(q, sel(k, 1))], [v])
        yc_ref[0, :, cols] = _head_norm(o, g_ref[h], lam_init).astype(yc_ref.dtype)


def _ctx_attn(sink, c_lambda, subln3, con, lam_init):
    aq, ak, aks, av, avs, bq, bk, bv, cq, ck, cv = con[:11]
    bsz, ctx, _ = aq.shape
    wide = pl.BlockSpec((1, ctx, 512), lambda b: (b, 0, 0))
    nar = pl.BlockSpec((1, ctx, LANES), lambda b: (b, 0, 0))
    return pl.pallas_call(
        functools.partial(_ctx_attn_kernel, lam_init=lam_init),
        out_shape=[jax.ShapeDtypeStruct(aq.shape, MXU_DTYPE)] * 3,
        grid=(bsz,),
        in_specs=[pl.BlockSpec(memory_space=pltpu.SMEM),
                  pl.BlockSpec((4, HEAD_DIM), lambda b: (0, 0)),
                  pl.BlockSpec((C_HEADS, 1, LANES), lambda b: (0, 0, 0)),
                  wide, nar, nar, nar, nar, wide, wide, wide, wide, wide, wide],
        out_specs=[wide] * 3,
        compiler_params=_params(("parallel",)),
        name="attn_context",
    )(sink, c_lambda, subln3, aq, ak, aks, av, avs, bq, bk, bv, cq, ck, cv)


def _merge_kernel(x_ref, gt_ref, ya_ref, yb_ref, yc_ref, ga_ref, gb_ref, gc_ref,
                  wa_ref, wb_ref, wc_ref, wo_ref, o_ref):
    m = (ga_ref[0].astype(F32) * _nn(ya_ref[0], wa_ref[...])
         + gb_ref[0].astype(F32) * _nn(yb_ref[0], wb_ref[...])
         + gc_ref[0].astype(F32) * _nn(yc_ref[0], wc_ref[...]))
    o_ref[0] = x_ref[0] + gt_ref[0] * _nn(m.astype(MXU_DTYPE), wo_ref[...])


def _merge(x, mod3, mod_row, ya, yb, yc, ga, gb, gc, wa, wb, wc, wo):
    bsz, n, d = x.shape
    tm = min(TM_PROJ, n)
    tok = lambda w: pl.BlockSpec((1, tm, w), lambda b, i: (b, i, 0))
    return pl.pallas_call(
        _merge_kernel,
        out_shape=jax.ShapeDtypeStruct(x.shape, F32),
        grid=(bsz, n // tm),
        in_specs=[tok(d), pl.BlockSpec((1, 1, d), lambda b, i: (mod_row(b), 0, 2)),
                  tok(512), tok(512), tok(512), tok(d), tok(d), tok(d),
                  _resident(wa.shape), _resident(wb.shape), _resident(wc.shape), _resident(wo.shape)],
        out_specs=tok(d),
        compiler_params=_params(("parallel", "parallel")),
        name="merge",
    )(x, mod3, ya, yb, yc, ga, gb, gc, wa, wb, wc, wo)


def _top_values(ref, row0, n_rows, count, out_ref, out_row0, floor):
    for k in range(count):
        cur = ref[row0:row0 + n_rows]
        m = jnp.max(cur, axis=0, keepdims=True)
        out_ref[out_row0 + k:out_row0 + k + 1] = m
        if k + 1 < count:
            ref[row0:row0 + n_rows] = jnp.where(cur == m, floor, cur)


def _batcher_network(n):
    pairs = []
    p = 1
    while p < n:
        k = p
        while k >= 1:
            for j in range(k % p, n - k, 2 * k):
                for i in range(min(k, n - j - k)):
                    if (i + j) // (2 * p) == (i + j + k) // (2 * p):
                        pairs.append((i + j, i + j + k))
            k //= 2
        p *= 2
    return tuple(pairs)


N_TOP = PEER_TOPK + 1
SUBLANES = 8
_SORT_NET = _batcher_network(PEER_KEYS // SUBLANES)
_CAND_COUNTS = tuple(N_TOP // (a + 1) for a in range(N_TOP))
_CAND_OFFS = tuple(int(v) for v in np.cumsum((0,) + _CAND_COUNTS))
N_CAND = -(-_CAND_OFFS[-1] // SUBLANES) * SUBLANES


def _top_sorted(s_ref, lanes, out_ref, slot):
    n_grp = PEER_KEYS // SUBLANES
    lists = [s_ref[r * SUBLANES:(r + 1) * SUBLANES, lanes] for r in range(n_grp)]
    for a, b in _SORT_NET:
        lists[a], lists[b] = jnp.maximum(lists[a], lists[b]), jnp.minimum(lists[a], lists[b])
    for t in range(N_TOP):
        head = lists[0]
        m = jnp.max(head, axis=0, keepdims=True)
        out_ref[slot, t:t + 1, lanes] = m
        remaining = N_TOP - 1 - t
        hit = head == m
        for k in range(remaining):
            below = lists[k + 1] if k + 1 < n_grp else -jnp.inf
            lists[k] = jnp.where(hit, below, lists[k])


def _route_kernel(x_ref, g_ref, sc_ref, sh_ref, wq_ref, keys_ref,
                  h_ref, e1_ref, e2_ref, thr_ref, s_ref, top_ref, cand_ref):
    tt = x_ref.shape[1]
    h2 = _rms_mod(x_ref[0], g_ref[...], sc_ref[0], sh_ref[0]).astype(MXU_DTYPE)
    h_ref[0] = h2
    q = _nn(h2, wq_ref[...]).astype(MXU_DTYPE)
    top_ref[:, N_TOP:, :] = jnp.full((3, top_ref.shape[1] - N_TOP, tt), -jnp.inf, F32)
    cand_ref[_CAND_OFFS[-1]:, :] = jnp.full((N_CAND - _CAND_OFFS[-1], tt), -1.0, F32)
    for h in range(PEER_HEADS):
        tops = []
        for p, e_ref in ((0, e1_ref), (1, e2_ref)):
            hp = 2 * h + p
            s_ref[...] = _nt(keys_ref[hp], q[:, hp * LANES:(hp + 1) * LANES])
            for tc in range(tt // LANES):
                _top_sorted(s_ref, slice(tc * LANES, (tc + 1) * LANES), top_ref, p)
            m = top_ref[p, 0:1]
            e_ref[0, h] = jnp.exp(s_ref[...] - m)
            tops.append(jnp.exp(top_ref[p] - m))
        e1_top, e2_top = tops
        for a in range(N_TOP):
            cand_ref[_CAND_OFFS[a]:_CAND_OFFS[a + 1]] = e1_top[a:a + 1] * e2_top[0:_CAND_COUNTS[a]]
        _top_values(cand_ref, 0, N_CAND, N_TOP, top_ref.at[2], 0, -1.0)
        best = top_ref[2]
        inv_z = 1.0 / jnp.sum(best[0:PEER_TOPK], axis=0, keepdims=True)
        e1_ref[0, h] = e1_ref[0, h] * inv_z
        thr_ref[0, h:h + 1] = 0.5 * (best[PEER_TOPK - 1:PEER_TOPK] + best[PEER_TOPK:N_TOP]) * inv_z


def _route(x, g2, mod3, mod_row, wq, keys):
    bsz, n, d = x.shape
    tt = min(TT_PEER, n)
    gate_shape = jax.ShapeDtypeStruct((bsz, PEER_HEADS, PEER_KEYS, n), F32)
    gate_spec = pl.BlockSpec((1, PEER_HEADS, PEER_KEYS, tt), lambda b, i: (b, 0, 0, i))
    return pl.pallas_call(
        _route_kernel,
        out_shape=[jax.ShapeDtypeStruct((bsz, n, d), MXU_DTYPE), gate_shape, gate_shape,
                   jax.ShapeDtypeStruct((bsz, PEER_HEADS, n), F32)],
        grid=(bsz, n // tt),
        in_specs=[pl.BlockSpec((1, tt, d), lambda b, i: (b, i, 0)),
                  pl.BlockSpec((1, d), lambda b, i: (0, 0)),
                  pl.BlockSpec((1, 1, d), lambda b, i: (mod_row(b), 0, 4)),
                  pl.BlockSpec((1, 1, d), lambda b, i: (mod_row(b), 0, 3)),
                  _resident(wq.shape), _resident(keys.shape)],
        out_specs=[pl.BlockSpec((1, tt, d), lambda b, i: (b, i, 0)), gate_spec, gate_spec,
                   pl.BlockSpec((1, PEER_HEADS, tt), lambda b, i: (b, 0, i))],
        scratch_shapes=[pltpu.VMEM((PEER_KEYS, tt), F32),
                        pltpu.VMEM((3, 3 * SUBLANES, tt), F32),
                        pltpu.VMEM((N_CAND, tt), F32)],
        compiler_params=_params(("parallel", "parallel")),
        name="peer_route",
    )(x, g2, mod3, mod3, wq, keys)


def _expert_kernel(x_ref, gt_ref, gf_ref, h_ref, u_ref, v_ref, e1_ref, e2_ref, thr_ref, o_ref,
                   acc_ref, a_ref, w_ref, *, final_norm):
    eb = pl.program_id(2)
    tt = h_ref.shape[1]

    @pl.when(eb == 0)
    def _zero():
        acc_ref[...] = jnp.zeros_like(acc_ref)

    n_i = TE_PEER // PEER_KEYS
    i0 = pl.multiple_of(eb * n_i, n_i)
    i_per_group = TE_GROUP // PEER_KEYS
    n_groups = TE_PEER // TE_GROUP
    sub = 64
    grows = lambda g: slice(g * TE_GROUP, (g + 1) * TE_GROUP)

    def score(g):
        a_ref[g] = _nt(u_ref[grows(g)], h_ref[0])

    def value(g):
        acc_ref[...] += _nn(w_ref[g].T.astype(MXU_DTYPE), v_ref[grows(g)])

    def gates(g):
        for ig in range(i_per_group):
            il = g * i_per_group + ig
            for tc in range(tt // LANES):
                lanes = slice(tc * LANES, (tc + 1) * LANES)
                for js in range(PEER_KEYS // sub):
                    rows = slice(js * sub, (js + 1) * sub)
                    gate = None
                    for h in range(PEER_HEADS):
                        e1_row = e1_ref[0, h, pl.ds(i0, n_i), lanes][il:il + 1]
                        prod = e2_ref[0, h, rows, lanes] * e1_row
                        sel = jnp.where(prod >= thr_ref[0, h:h + 1, lanes], prod, 0.0)
                        gate = sel if gate is None else gate + sel
                    erows = slice(ig * PEER_KEYS + js * sub, ig * PEER_KEYS + (js + 1) * sub)
                    a = a_ref[g, erows, lanes]
                    act = (0.5 * a) * (1.0 + lax.erf(a * math.sqrt(0.5)))
                    w_ref[g, erows, lanes] = gate * act

    score(0)
    for g in range(n_groups):
        if g + 1 < n_groups:
            score(g + 1)
        if g >= 1:
            value(g - 1)
        gates(g)
    value(n_groups - 1)

    @pl.when(eb == pl.num_programs(2) - 1)
    def _finish():
        y = x_ref[0] + gt_ref[0] * acc_ref[...]
        if final_norm:
            y = (y * lax.rsqrt(jnp.mean(y * y, axis=-1, keepdims=True) + NORM_EPS)) * gf_ref[...]
        o_ref[0] = y


def _experts(x, mod3, mod_row, g_final, h2, u, v, e1, e2, thr, final_norm):
    bsz, n, d = x.shape
    tt = min(TT_PEER, n)
    n_exp = u.shape[0]
    gate_spec = pl.BlockSpec((1, PEER_HEADS, PEER_KEYS, tt), lambda b, i, e: (b, 0, 0, i))
    return pl.pallas_call(
        functools.partial(_expert_kernel, final_norm=final_norm),
        out_shape=jax.ShapeDtypeStruct(x.shape, F32),
        grid=(bsz, n // tt, n_exp // TE_PEER),
        in_specs=[pl.BlockSpec((1, tt, d), lambda b, i, e: (b, i, 0)),
                  pl.BlockSpec((1, 1, d), lambda b, i, e: (mod_row(b), 0, 5)),
                  pl.BlockSpec((1, d), lambda b, i, e: (0, 0)),
                  pl.BlockSpec((1, tt, d), lambda b, i, e: (b, i, 0)),
                  pl.BlockSpec((TE_PEER, d), lambda b, i, e: (e, 0)),
                  pl.BlockSpec((TE_PEER, d), lambda b, i, e: (e, 0)),
                  gate_spec, gate_spec,
                  pl.BlockSpec((1, PEER_HEADS, tt), lambda b, i, e: (b, 0, i))],
        out_specs=pl.BlockSpec((1, tt, d), lambda b, i, e: (b, i, 0)),
        scratch_shapes=[pltpu.VMEM((tt, d), F32),
                        pltpu.VMEM((TE_PEER // TE_GROUP, TE_GROUP, tt), F32),
                        pltpu.VMEM((TE_PEER // TE_GROUP, TE_GROUP, tt), F32)],
        compiler_params=_params(("parallel", "parallel", "arbitrary")),
        name="peer_experts",
    )(x, mod3, g_final, h2, u, v, e1, e2, thr)


def _rope_tables(n):
    t = jnp.arange(n, dtype=jnp.int32)
    row = (t // GRID_W).astype(F32)
    col = (t % GRID_W).astype(F32)
    n_freq = HEAD_DIM // 4
    inv = ROPE_THETA ** (-jnp.arange(n_freq, dtype=F32) / n_freq)
    ang = jnp.concatenate([row[:, None] * inv, col[:, None] * inv], axis=-1)
    cos, sin = jnp.cos(ang), jnp.sin(ang)
    return jnp.tile(cos, (1, 4)), jnp.tile(jnp.concatenate([-sin, sin], axis=-1), (1, 2))


def kernel(x, c, ctx, c_ctx, w_mod, b_mod, g_norm1, g_norm2, w_in, a_sink, b_rpb, c_lambda,
           c_subln, w_branch_a, w_branch_b, w_branch_c, w_out, peer_wq, peer_keys, peer_u,
           peer_v, g_final):
    bsz, n, d = x.shape
    depth = w_mod.shape[0]
    n_ctx = ctx.shape[1]
    assert d == D_MODEL and bsz + 1 <= 8
    assert n % (B_ROWS * GRID_W) == 0 and n // GRID_W >= 4 * B_KROWS and n % TK_C == 0

    cvec = jnp.zeros((8, d), F32).at[:bsz].set(c).at[bsz].set(c_ctx)
    mod = _modulation(cvec, w_mod, b_mod)
    lat_row = lambda b: b
    ctx_row = lambda b: bsz
    cos_t, sin_t = _rope_tables(n)
    ones_t, zeros_t = jnp.ones((n_ctx, LANES), F32), jnp.zeros((n_ctx, LANES), F32)
    cast = lambda w: w.astype(MXU_DTYPE)
    gfin = g_final.reshape(1, d)

    xc = ctx
    for l in range(depth):
        last = l == depth - 1
        lam_init = 0.8 - 0.6 * math.exp(-0.3 * l)
        mod3 = mod[l].reshape(8, 1, 6 * d)
        g1, g2 = g_norm1[l].reshape(1, d), g_norm2[l].reshape(1, d)
        w_in_l = cast(w_in[l])
        subln3 = c_subln[l].reshape(C_HEADS, 1, LANES)
        wa, wb, wc, wo = cast(w_branch_a[l]), cast(w_branch_b[l]), cast(w_branch_c[l]), cast(w_out[l])
        wq = cast(peer_wq[l])
        keys = cast(peer_keys[l].reshape(2 * PEER_HEADS, PEER_KEYS, LANES))
        u, v = cast(peer_u[l]), cast(peer_v[l])

        lat = _in_proj(x, g1, mod3, lat_row, w_in_l, cos_t, sin_t)
        con = _in_proj(xc, g1, mod3, ctx_row, w_in_l, ones_t, zeros_t)
        (aq, ak, aks, av, avs, bq, bk, bv, cq, ck, _, ga, gb, gc, cvt) = lat
        ya = _attn_a(a_sink[l], aq, ak, aks, av, avs, con[1], con[2], con[3], con[4])
        yb = _attn_b(b_rpb[l], bq, bk, bv, con[6], con[7])
        yc = _attn_c(c_lambda[l], subln3, cq, ck, cvt, con[9], con[14], lam_init)
        x = _merge(x, mod3, lat_row, ya, yb, yc, ga, gb, gc, wa, wb, wc, wo)
        h2, e1, e2, thr = _route(x, g2, mod3, lat_row, wq, keys)
        x = _experts(x, mod3, lat_row, gfin, h2, u, v, e1, e2, thr, final_norm=last)

        if not last:
            ya_c, yb_c, yc_c = _ctx_attn(a_sink[l], c_lambda[l], subln3, con, lam_init)
            xc = _merge(xc, mod3, ctx_row, ya_c, yb_c, yc_c, con[11], con[12], con[13],
                        wa, wb, wc, wo)
            h2c, e1c, e2c, thrc = _route(xc, g2, mod3, ctx_row, wq, keys)
            xc = _experts(xc, mod3, ctx_row, gfin, h2c, u, v, e1c, e2c, thrc, final_norm=False)
    return x
```

```python
import functools
import math

import jax
import jax.numpy as jnp
import numpy as np
from jax import lax
from jax.experimental import pallas as pl
from jax.experimental.pallas import tpu as pltpu

F32 = jnp.float32
BF16 = jnp.bfloat16
MXU_DTYPE = BF16

LANES = 128
MXU_WIDTH = 256
VMEM_LIMIT_BYTES = 56 * 1024 * 1024

HEAD_DIM = 64
GRID_W = 64
ROPE_THETA = 10000.0
NORM_EPS = 1e-6
NEG_INF = -1e30
LOG2_E = math.log2(math.e)
A_HEADS, A_KV_HEADS, A_WINDOW, A_BLOCK = 8, 2, 128, 128
B_HEADS, NA_ROWS, NA_COLS = 8, 8, 16
C_HEADS = 4
PEER_HEADS, PEER_KEYS, PEER_TOPK = 8, 128, 16
D_MODEL = 1024
IN_SPLITS = (512, 128, 128, 512, 512, 512, 512, 512, 512, 1024, 1024, 1024)
IN_OFFS = tuple(int(v) for v in np.cumsum((0,) + IN_SPLITS))

TM_PROJ = 512
A_QBLOCKS = 4
TQ_C = 512
TK_C = 1024
B_ROWS = 8
B_KROWS = 4
TT_PEER = 512
TE_PEER = 1024
TE_GROUP = 256
N_PT = 30


def _nt(a, b):
    return lax.dot_general(a, b, (((1,), (1,)), ((), ())), preferred_element_type=F32)


def _nn(a, b):
    return jnp.dot(a, b, preferred_element_type=F32)


def _params(sem):
    return pltpu.CompilerParams(dimension_semantics=sem, vmem_limit_bytes=VMEM_LIMIT_BYTES)


def _resident(shape):
    nd = len(shape)
    return pl.BlockSpec(shape, lambda *_: (0,) * nd, pipeline_mode=pl.Buffered(1))


def _mod_kernel(c_ref, w_ref, b_ref, o_ref):
    c = c_ref[...]
    s = c * jax.nn.sigmoid(c)
    w = w_ref[0]
    s_hi = s.astype(MXU_DTYPE)
    s_lo = (s - s_hi.astype(F32)).astype(MXU_DTYPE)
    w_hi = w.astype(MXU_DTYPE)
    w_lo = (w - w_hi.astype(F32)).astype(MXU_DTYPE)
    acc = _nn(s_hi, w_hi) + _nn(s_lo, w_hi) + _nn(s_hi, w_lo)
    o_ref[0] = acc + b_ref[0]


def _modulation(cvec, w_mod, b_mod):
    depth, d, n6 = w_mod.shape
    tn = 1536
    return pl.pallas_call(
        _mod_kernel,
        out_shape=jax.ShapeDtypeStruct((depth, 8, n6), F32),
        grid=(depth, n6 // tn),
        in_specs=[pl.BlockSpec((8, d), lambda l, j: (0, 0)),
                  pl.BlockSpec((1, d, tn), lambda l, j: (l, 0, j)),
                  pl.BlockSpec((1, 1, tn), lambda l, j: (l, 0, j))],
        out_specs=pl.BlockSpec((1, 8, tn), lambda l, j: (l, 0, j)),
        compiler_params=_params(("parallel", "parallel")),
        name="modulation",
    )(cvec, w_mod, b_mod.reshape(depth, 1, n6))


def _rms_mod(x, g, sc, sh):
    y = x * lax.rsqrt(jnp.mean(x * x, axis=-1, keepdims=True) + NORM_EPS)
    return (y * g) * (1.0 + sc) + sh


def _in_proj_kernel(x_ref, g_ref, sc_ref, sh_ref, w_ref, cos_ref, sin_ref,
                    aq_ref, ak_ref, aks_ref, av_ref, avs_ref, bq_ref, bk_ref, bv_ref,
                    cq_ref, ck_ref, cv_ref, ga_ref, gb_ref, gc_ref, cvt_ref):
    h = _rms_mod(x_ref[0], g_ref[...], sc_ref[0], sh_ref[0]).astype(MXU_DTYPE)
    cos = cos_ref[...]
    sin = sin_ref[...]
    lane = lax.broadcasted_iota(jnp.int32, (1, LANES), 1)
    first_half = (lane % HEAD_DIM) < (HEAD_DIM // 2)
    scale = HEAD_DIM ** -0.5

    def proj(seg, jp):
        c0 = IN_OFFS[seg] + jp * MXU_WIDTH
        t = _nn(h, w_ref[:, c0:c0 + MXU_WIDTH])
        return t[:, :LANES], t[:, LANES:]

    def rope(v):
        rot = jnp.where(first_half, pltpu.roll(v, LANES - 32, 1), pltpu.roll(v, 32, 1))
        return v * cos + rot * sin

    def emit(ref, seg, fn=lambda v: v):
        for jp in range(IN_SPLITS[seg] // MXU_WIDTH):
            for k, part in enumerate(proj(seg, jp)):
                j = 2 * jp + k
                ref[0, :, j * LANES:(j + 1) * LANES] = fn(part).astype(ref.dtype)

    emit(aq_ref, 0, lambda v: rope(v) * scale)
    emit(bq_ref, 3, lambda v: v * scale)
    emit(bk_ref, 4)
    emit(bv_ref, 5)
    emit(cq_ref, 6, lambda v: rope(v) * (scale * LOG2_E))
    emit(ck_ref, 7, rope)
    for jp in range(IN_SPLITS[8] // MXU_WIDTH):
        for k, cv in enumerate(proj(8, jp)):
            j = 2 * jp + k
            cv_ref[0, :, j * LANES:(j + 1) * LANES] = cv.astype(cv_ref.dtype)
            cvt_ref[0, j * LANES:(j + 1) * LANES, :] = cv.T.astype(cvt_ref.dtype)
    ak, av = proj(1, 0)
    ak = rope(ak)
    ak_ref[0] = ak.astype(ak_ref.dtype)
    aks_ref[0] = pltpu.roll(ak, HEAD_DIM, 1).astype(aks_ref.dtype)
    av_ref[0] = av.astype(av_ref.dtype)
    avs_ref[0] = pltpu.roll(av, HEAD_DIM, 1).astype(avs_ref.dtype)
    emit(ga_ref, 9, jax.nn.sigmoid)
    emit(gb_ref, 10, jax.nn.sigmoid)
    emit(gc_ref, 11, jax.nn.sigmoid)


def _in_proj(x, g1, mod3, mod_row, w_in, cos_t, sin_t):
    bsz, n, d = x.shape
    tm = min(TM_PROJ, n)
    tok = lambda w: pl.BlockSpec((1, tm, w), lambda b, i: (b, i, 0))
    widths = (512, 128, 128, 128, 128, 512, 512, 512, 512, 512, 512, 1024, 1024, 1024)
    return pl.pallas_call(
        _in_proj_kernel,
        out_shape=[jax.ShapeDtypeStruct((bsz, n, w), MXU_DTYPE) for w in widths]
        + [jax.ShapeDtypeStruct((bsz, 512, n), MXU_DTYPE)],
        grid=(bsz, n // tm),
        in_specs=[tok(d),
                  pl.BlockSpec((1, d), lambda b, i: (0, 0)),
                  pl.BlockSpec((1, 1, d), lambda b, i: (mod_row(b), 0, 1)),
                  pl.BlockSpec((1, 1, d), lambda b, i: (mod_row(b), 0, 0)),
                  _resident(w_in.shape),
                  pl.BlockSpec((tm, LANES), lambda b, i: (i, 0)),
                  pl.BlockSpec((tm, LANES), lambda b, i: (i, 0))],
        out_specs=[tok(w) for w in widths] + [pl.BlockSpec((1, 512, tm), lambda b, i: (b, 0, i))],
        compiler_params=_params(("parallel", "parallel")),
        name="in_proj",
    )(x, g1, mod3, mod3, w_in, cos_t, sin_t)


def _lane_lo():
    return lax.broadcasted_iota(jnp.int32, (1, LANES), 1) < HEAD_DIM


def _softmax_pv(s_list, v_list, extra_logit=None, exp=jnp.exp):
    m = functools.reduce(jnp.maximum, [jnp.max(s, axis=-1, keepdims=True) for s in s_list])
    if extra_logit is not None:
        m = jnp.maximum(m, extra_logit)
    l = 0.0 if extra_logit is None else exp(extra_logit - m)
    o = 0.0
    for s, v in zip(s_list, v_list):
        p = exp(s - m)
        l = l + jnp.sum(p, axis=-1, keepdims=True)
        o = o + _nn(p.astype(v.dtype), v)
    return o / l


def _attn_a_kernel(sink_ref, q_ref, kp_ref, kc_ref, kn_ref, ksp_ref, ksc_ref, ksn_ref,
                   vp_ref, vc_ref, vn_ref, vsp_ref, vsc_ref, vsn_ref,
                   xk_ref, xks_ref, xv_ref, xvs_ref, o_ref, *, n_tokens):
    i = pl.program_id(1)
    lo = _lane_lo()
    cat = lambda refs: jnp.concatenate([r[0] for r in refs], axis=0)
    k_loc, ks_loc = cat((kp_ref, kc_ref, kn_ref)), cat((ksp_ref, ksc_ref, ksn_ref))
    v_loc, vs_loc = cat((vp_ref, vc_ref, vn_ref)), cat((vsp_ref, vsc_ref, vsn_ref))
    zero = jnp.zeros((), k_loc.dtype)
    k_of = {(0, 0): (jnp.where(lo, k_loc, zero), jnp.where(lo, xk_ref[0], zero)),
            (0, 1): (jnp.where(lo, zero, ks_loc), jnp.where(lo, zero, xks_ref[0])),
            (1, 0): (jnp.where(lo, ks_loc, zero), jnp.where(lo, xks_ref[0], zero)),
            (1, 1): (jnp.where(lo, zero, k_loc), jnp.where(lo, zero, xk_ref[0]))}
    v_of = {(0, 0): (v_loc, xv_ref[0]), (0, 1): (vs_loc, xvs_ref[0]),
            (1, 0): (vs_loc, xvs_ref[0]), (1, 1): (v_loc, xv_ref[0])}
    chains = []
    for qb in range(A_QBLOCKS):
        blk = i * A_QBLOCKS + qb
        qpos = blk * A_BLOCK + lax.broadcasted_iota(jnp.int32, (A_BLOCK, 1), 0)
        kpos = (blk - 1) * A_BLOCK + lax.broadcasted_iota(jnp.int32, (1, 3 * A_BLOCK), 1)
        dist = qpos - kpos
        valid = (jnp.maximum(dist, -dist) <= A_WINDOW) & (kpos >= 0) & (kpos < n_tokens)
        qrows = slice(qb * A_BLOCK, (qb + 1) * A_BLOCK)
        krows = slice(qb * A_BLOCK, (qb + 3) * A_BLOCK)
        for hp in range(A_HEADS // 2):
            qp = q_ref[0, qrows, hp * LANES:(hp + 1) * LANES]
            for half in range(2):
                h = 2 * hp + half
                g = h // (A_HEADS // A_KV_HEADS)
                (kl, kx), (vl, vx) = k_of[(g, half)], v_of[(g, half)]
                s_loc = jnp.where(valid, _nt(qp, kl[krows]), NEG_INF)
                chains.append(([s_loc, _nt(qp, kx)], [vl[krows], vx], sink_ref[h]))
    outs = [_softmax_pv(s, v, extra_logit=sink) for s, v, sink in chains]
    for qb in range(A_QBLOCKS):
        qrows = slice(qb * A_BLOCK, (qb + 1) * A_BLOCK)
        for hp in range(A_HEADS // 2):
            even, odd = outs[(qb * (A_HEADS // 2) + hp) * 2:(qb * (A_HEADS // 2) + hp) * 2 + 2]
            o_ref[0, qrows, hp * LANES:(hp + 1) * LANES] = jnp.where(lo, even, odd).astype(o_ref.dtype)


def _attn_a(sink, aq, ak, aks, av, avs, xak, xaks, xav, xavs):
    bsz, n, _ = aq.shape
    nb = n // A_BLOCK
    tq = A_QBLOCKS * A_BLOCK
    ctx = xak.shape[1]
    prev = pl.BlockSpec((1, A_BLOCK, LANES),
                        lambda b, i: (b, jnp.maximum(i * A_QBLOCKS - 1, 0), 0))
    cur = pl.BlockSpec((1, tq, LANES), lambda b, i: (b, i, 0))
    nxt = pl.BlockSpec((1, A_BLOCK, LANES),
                       lambda b, i: (b, jnp.minimum((i + 1) * A_QBLOCKS, nb - 1), 0))
    cx = pl.BlockSpec((1, ctx, LANES), lambda b, i: (b, 0, 0))
    return pl.pallas_call(
        functools.partial(_attn_a_kernel, n_tokens=n),
        out_shape=jax.ShapeDtypeStruct(aq.shape, MXU_DTYPE),
        grid=(bsz, n // tq),
        in_specs=[pl.BlockSpec(memory_space=pltpu.SMEM),
                  pl.BlockSpec((1, tq, 512), lambda b, i: (b, i, 0)),
                  prev, cur, nxt, prev, cur, nxt, prev, cur, nxt, prev, cur, nxt, cx, cx, cx, cx],
        out_specs=pl.BlockSpec((1, tq, 512), lambda b, i: (b, i, 0)),
        compiler_params=_params(("parallel", "parallel")),
        name="attn_window",
    )(sink, aq, ak, ak, ak, aks, aks, aks, av, av, av, avs, avs, avs, xak, xaks, xav, xavs)


def _attn_b_kernel(rpb_ref, q_ref, k0_ref, k1_ref, k2_ref, k3_ref, v0_ref, v1_ref, v2_ref, v3_ref,
                   xk_ref, xv_ref, o_ref, pt_ref, *, n_rows):
    pair, i = pl.program_id(0), pl.program_id(2)
    n_a = 2 * NA_ROWS - 1
    n_b = 2 * NA_COLS - 1

    @pl.when((pl.program_id(1) == 0) & (i == 0))
    def _build_bias_tables():
        qc = lax.broadcasted_iota(jnp.int32, (GRID_W, LANES), 0)
        ln = lax.broadcasted_iota(jnp.int32, (GRID_W, LANES), 1)
        kc = ln % GRID_W
        hi = ln >= GRID_W
        cstart = jnp.clip(qc - NA_COLS // 2, 0, GRID_W - NA_COLS)
        col_ok = (kc >= cstart) & (kc < cstart + NA_COLS)
        d = kc - qc + (NA_COLS - 1)
        for hh in range(2):
            h = pair * 2 + hh

            def body(ai, carry):
                a = ai - 8
                a_ok, a1_ok = (a >= 0) & (a < n_a), (a + 1 >= 0) & (a + 1 < n_a)
                ra, ra1 = h * n_a + jnp.clip(a, 0, n_a - 1), h * n_a + jnp.clip(a + 1, 0, n_a - 1)
                t = jnp.full((GRID_W, LANES), NEG_INF, F32)
                for b in range(n_b):
                    va = jnp.where(a_ok, rpb_ref[ra, b], NEG_INF)
                    va1 = jnp.where(a1_ok, rpb_ref[ra1, b], NEG_INF)
                    t = jnp.where(d == b, jnp.where(hi, va1, va), t)
                pt_ref[hh, ai] = jnp.where(col_ok, t, NEG_INF)
                return carry

            lax.fori_loop(0, N_PT, body, 0)

    lo = _lane_lo()
    r0 = i * B_ROWS
    kb0 = jnp.clip(2 * i - 1, 0, n_rows // B_KROWS - 4)
    krow_lane = lax.broadcasted_iota(jnp.int32, (1, B_KROWS * GRID_W), 1) // GRID_W
    k_refs, v_refs = (k0_ref, k1_ref, k2_ref, k3_ref), (v0_ref, v1_ref, v2_ref, v3_ref)
    qp = q_ref[0]
    zero = jnp.zeros((), qp.dtype)
    raw = []
    for hh in range(2):
        keep = lambda t: jnp.where(lo, t, zero) if hh == 0 else jnp.where(lo, zero, t)
        raw.append([_nt(qp, keep(k_refs[j][0, 0])) for j in range(4)]
                   + [_nt(qp, keep(xk_ref[0]))])
    outs = []
    for hh in range(2):
        s_tiles = []
        for j in range(4):
            s = raw[hh][j]
            kr0 = (kb0 + j) * B_KROWS
            rows = []
            for qr in range(B_ROWS):
                r = r0 + qr
                rs = jnp.clip(r - NA_ROWS // 2, 0, n_rows - NA_ROWS)
                a0 = kr0 - r + (NA_ROWS - 1)
                bias = jnp.concatenate([pt_ref[hh, a0 + 8], pt_ref[hh, a0 + 10]], axis=1)
                row_ok = (kr0 + krow_lane >= rs) & (kr0 + krow_lane < rs + NA_ROWS)
                rows.append(jnp.where(row_ok, s[qr * GRID_W:(qr + 1) * GRID_W] + bias, NEG_INF))
            s_tiles.append(jnp.concatenate(rows, axis=0))
        s_tiles.append(raw[hh][4])
        outs.append(_softmax_pv(s_tiles, [v_refs[j][0, 0] for j in range(4)] + [xv_ref[0]]))
    o_ref[0] = jnp.where(lo, outs[0], outs[1]).astype(o_ref.dtype)


def _attn_b(rpb, bq, bk, bv, xbk, xbv):
    bsz, n, _ = bq.shape
    n_rows = n // GRID_W
    tq = B_ROWS * GRID_W
    tkb = B_KROWS * GRID_W
    ctx = xbk.shape[1]
    bk4 = bk.reshape(bsz, n // tkb, tkb, 512)
    bv4 = bv.reshape(bsz, n // tkb, tkb, 512)
    nkb = n // tkb

    def kspec(j):
        return pl.BlockSpec((1, 1, tkb, LANES),
                            lambda p, b, i: (b, jnp.clip(2 * i - 1, 0, nkb - 4) + j, 0, p))

    cx = pl.BlockSpec((1, ctx, LANES), lambda p, b, i: (b, 0, p))
    return pl.pallas_call(
        functools.partial(_attn_b_kernel, n_rows=n_rows),
        out_shape=jax.ShapeDtypeStruct(bq.shape, MXU_DTYPE),
        grid=(B_HEADS // 2, bsz, n // tq),
        in_specs=[pl.BlockSpec(memory_space=pltpu.SMEM),
                  pl.BlockSpec((1, tq, LANES), lambda p, b, i: (b, i, p)),
                  kspec(0), kspec(1), kspec(2), kspec(3), kspec(0), kspec(1), kspec(2), kspec(3),
                  cx, cx],
        out_specs=pl.BlockSpec((1, tq, LANES), lambda p, b, i: (b, i, p)),
        scratch_shapes=[pltpu.VMEM((2, N_PT, GRID_W, LANES), F32)],
        compiler_params=_params(("arbitrary", "arbitrary", "arbitrary")),
        name="attn_neighbourhood",
    )(rpb.reshape(B_HEADS * (2 * NA_ROWS - 1), 2 * NA_COLS - 1), bq,
      bk4, bk4, bk4, bk4, bv4, bv4, bv4, bv4, xbk, xbv)


def _diff_lambda(lam_ref, lam_init):
    lam = lam_ref[...]
    a = jnp.sum(lam[0:1] * lam[1:2], axis=-1, keepdims=True)
    b = jnp.sum(lam[2:3] * lam[3:4], axis=-1, keepdims=True)
    return jnp.exp(a) - jnp.exp(b) + lam_init


def _head_norm(o, g, lam_init):
    y = o * lax.rsqrt(jnp.mean(o * o, axis=-1, keepdims=True) + NORM_EPS)
    return (y * g) * (1.0 - lam_init)


def _attn_c_kernel(lam_ref, g_ref, q_ref, k_ref, vt_ref, xk_ref, xvt_ref, o_ref,
                   k1_ref, k2_ref, vall_ref, s_ref, p_ref, stat_ref, acc_ref,
                   *, lam_init, n_tokens, tk, n_chunks):
    lo = _lane_lo()

    @pl.when(pl.program_id(2) == 0)
    def _split_keys():
        zero = jnp.zeros((), k1_ref.dtype)
        k1_ref[0:n_tokens] = jnp.where(lo, k_ref[0], zero)
        k2_ref[0:n_tokens] = jnp.where(lo, zero, k_ref[0])
        k1_ref[n_tokens:] = jnp.where(lo, xk_ref[0], zero)
        k2_ref[n_tokens:] = jnp.where(lo, zero, xk_ref[0])

        vall_ref[:, 0:n_tokens] = vt_ref[0]
        vall_ref[:, n_tokens:] = xvt_ref[0]

    q = q_ref[0]
    tq = q.shape[0]
    key_refs = (k1_ref, k2_ref)
    stat_ref[:, 0:1, :] = jnp.full((2, 1, tq), -jnp.inf, F32)
    stat_ref[:, 1:2, :] = jnp.zeros((2, 1, tq), F32)
    acc_ref[...] = jnp.zeros_like(acc_ref)

    def scores(t, slot):
        off = pl.multiple_of(t * tk, LANES)
        for c in range(2):
            s = _nt(key_refs[c][pl.ds(off, tk), :], q)
            s_ref[slot, c] = s
            stat_ref[c, 4 + slot:5 + slot, :] = jnp.max(s, axis=0, keepdims=True)

    def softmax(slot):
        for c in range(2):
            s = s_ref[slot, c]
            m_old = stat_ref[c, 0:1, :]
            m_new = jnp.maximum(m_old, stat_ref[c, 4 + slot:5 + slot, :])
            alpha = jnp.exp2(m_old - m_new)
            p = jnp.exp2(s - m_new)
            p_ref[slot, c] = p.astype(p_ref.dtype)
            stat_ref[c, 0:1, :] = m_new
            stat_ref[c, 1:2, :] = alpha * stat_ref[c, 1:2, :] + jnp.sum(p, axis=0, keepdims=True)
            stat_ref[c, 2 + slot:3 + slot, :] = alpha

    def values(t, slot):
        off = pl.multiple_of(t * tk, LANES)
        vt = vall_ref[:, pl.ds(off, tk)]
        for c in range(2):
            acc_ref[c] = stat_ref[c, 2 + slot:3 + slot, :] * acc_ref[c] + _nn(vt, p_ref[slot, c])

    def iteration(t, parity):
        if not isinstance(t, int) or t < n_chunks:
            scores(t, parity)
        if not isinstance(t, int) or t >= 2:
            values(t - 2, parity)
        if not isinstance(t, int) or 1 <= t <= n_chunks:
            softmax(1 - parity)

    iteration(0, 0)
    iteration(1, 1)
    n_mid = n_chunks - 2

    def pair(i, carry):
        iteration(2 + 2 * i, 0)
        iteration(3 + 2 * i, 1)
        return carry

    lax.fori_loop(0, n_mid // 2, pair, 0)
    for t in range(2 + 2 * (n_mid // 2), n_chunks + 2):
        iteration(t, t % 2)
    inv_l = 1.0 / stat_ref[:, 1:2, :]
    ot = acc_ref[0] * inv_l[0] - _diff_lambda(lam_ref, lam_init) * (acc_ref[1] * inv_l[1])
    o_ref[0] = _head_norm(ot.T, g_ref[0], lam_init).astype(o_ref.dtype)


def _key_chunk(n_keys):
    return max(c for c in range(LANES, TK_C + 1, LANES) if n_keys % c == 0)


def _attn_c(c_lambda, subln3, cq, ck, cvt, xck, xcvt, lam_init):
    bsz, n, _ = cq.shape
    ctx = xck.shape[1]
    tq = min(TQ_C, n)
    tk = _key_chunk(n + ctx)
    return pl.pallas_call(
        functools.partial(_attn_c_kernel, lam_init=lam_init, n_tokens=n, tk=tk,
                          n_chunks=(n + ctx) // tk),
        out_shape=jax.ShapeDtypeStruct(cq.shape, MXU_DTYPE),
        grid=(bsz, C_HEADS, n // tq),
        in_specs=[pl.BlockSpec((4, HEAD_DIM), lambda b, h, i: (0, 0)),
                  pl.BlockSpec((1, 1, LANES), lambda b, h, i: (h, 0, 0)),
                  pl.BlockSpec((1, tq, LANES), lambda b, h, i: (b, i, h)),
                  pl.BlockSpec((1, n, LANES), lambda b, h, i: (b, 0, h)),
                  pl.BlockSpec((1, LANES, n), lambda b, h, i: (b, h, 0)),
                  pl.BlockSpec((1, ctx, LANES), lambda b, h, i: (b, 0, h)),
                  pl.BlockSpec((1, LANES, ctx), lambda b, h, i: (b, h, 0))],
        out_specs=pl.BlockSpec((1, tq, LANES), lambda b, h, i: (b, i, h)),
        scratch_shapes=[pltpu.VMEM((n + ctx, LANES), MXU_DTYPE),
                        pltpu.VMEM((n + ctx, LANES), MXU_DTYPE),
                        pltpu.VMEM((LANES, n + ctx), MXU_DTYPE),
                        pltpu.VMEM((2, 2, tk, tq), F32),
                        pltpu.VMEM((2, 2, tk, tq), MXU_DTYPE),
                        pltpu.VMEM((2, 8, tq), F32),
                        pltpu.VMEM((2, LANES, tq), F32)],
        compiler_params=_params(("arbitrary", "arbitrary", "arbitrary")),
        name="attn_differential",
    )(c_lambda, subln3, cq, ck, cvt, xck, xcvt)


def _ctx_attn_kernel(sink_ref, lam_ref, g_ref, aq_ref, ak_ref, aks_ref, av_ref, avs_ref,
                     bq_ref, bk_ref, bv_ref, cq_ref, ck_ref, cv_ref, ya_ref, yb_ref, yc_ref,
                     *, lam_init):
    lo = _lane_lo()
    zero = jnp.zeros((), ak_ref.dtype)
    sel = lambda t, half: jnp.where(lo, t, zero) if half == 0 else jnp.where(lo, zero, t)
    k_of = {(0, 0): sel(ak_ref[0], 0), (0, 1): sel(aks_ref[0], 1),
            (1, 0): sel(aks_ref[0], 0), (1, 1): sel(ak_ref[0], 1)}
    v_of = {(0, 0): av_ref[0], (0, 1): avs_ref[0], (1, 0): avs_ref[0], (1, 1): av_ref[0]}
    for hp in range(A_HEADS // 2):
        cols = slice(hp * LANES, (hp + 1) * LANES)
        qa, qb = aq_ref[0, :, cols], bq_ref[0, :, cols]
        kb, vb = bk_ref[0, :, cols], bv_ref[0, :, cols]
        oa, ob = [], []
        for half in range(2):
            h = 2 * hp + half
            g = h // (A_HEADS // A_KV_HEADS)
            oa.append(_softmax_pv([_nt(qa, k_of[(g, half)])], [v_of[(g, half)]],
                                  extra_logit=sink_ref[h]))
            ob.append(_softmax_pv([_nt(qb, sel(kb, half))], [vb]))
        ya_ref[0, :, cols] = jnp.where(lo, oa[0], oa[1]).astype(ya_ref.dtype)
        yb_ref[0, :, cols] = jnp.where(lo, ob[0], ob[1]).astype(yb_ref.dtype)
    lam = _diff_lambda(lam_ref, lam_init)
    for h in range(C_HEADS):
        cols = slice(h * LANES, (h + 1) * LANES)
        q, k, v = cq_ref[0, :, cols], ck_ref[0, :, cols], cv_ref[0, :, cols]
        o = (_softmax_pv([_nt(q, sel(k, 0))], [v], exp=jnp.exp2)
             - lam * _softmax_pv([_nt(q, sel(k, 1))], [v], exp=jnp.exp2))
        yc_ref[0, :, cols] = _head_norm(o, g_ref[h], lam_init).astype(yc_ref.dtype)


def _ctx_attn(sink, c_lambda, subln3, con, lam_init):
    aq, ak, aks, av, avs, bq, bk, bv, cq, ck, cv = con[:11]
    bsz, ctx, _ = aq.shape
    wide = pl.BlockSpec((1, ctx, 512), lambda b: (b, 0, 0))
    nar = pl.BlockSpec((1, ctx, LANES), lambda b: (b, 0, 0))
    return pl.pallas_call(
        functools.partial(_ctx_attn_kernel, lam_init=lam_init),
        out_shape=[jax.ShapeDtypeStruct(aq.shape, MXU_DTYPE)] * 3,
        grid=(bsz,),
        in_specs=[pl.BlockSpec(memory_space=pltpu.SMEM),
                  pl.BlockSpec((4, HEAD_DIM), lambda b: (0, 0)),
                  pl.BlockSpec((C_HEADS, 1, LANES), lambda b: (0, 0, 0)),
                  wide, nar, nar, nar, nar, wide, wide, wide, wide, wide, wide],
        out_specs=[wide] * 3,
        compiler_params=_params(("parallel",)),
        name="attn_context",
    )(sink, c_lambda, subln3, aq, ak, aks, av, avs, bq, bk, bv, cq, ck, cv)


def _merge_kernel(x_ref, gt_ref, ya_ref, yb_ref, yc_ref, ga_ref, gb_ref, gc_ref,
                  wa_ref, wb_ref, wc_ref, wo_ref, o_ref):
    m = (ga_ref[0].astype(F32) * _nn(ya_ref[0], wa_ref[...])
         + gb_ref[0].astype(F32) * _nn(yb_ref[0], wb_ref[...])
         + gc_ref[0].astype(F32) * _nn(yc_ref[0], wc_ref[...]))
    o_ref[0] = x_ref[0] + gt_ref[0] * _nn(m.astype(MXU_DTYPE), wo_ref[...])


def _merge(x, mod3, mod_row, ya, yb, yc, ga, gb, gc, wa, wb, wc, wo):
    bsz, n, d = x.shape
    tm = min(TM_PROJ, n)
    tok = lambda w: pl.BlockSpec((1, tm, w), lambda b, i: (b, i, 0))
    return pl.pallas_call(
        _merge_kernel,
        out_shape=jax.ShapeDtypeStruct(x.shape, F32),
        grid=(bsz, n // tm),
        in_specs=[tok(d), pl.BlockSpec((1, 1, d), lambda b, i: (mod_row(b), 0, 2)),
                  tok(512), tok(512), tok(512), tok(d), tok(d), tok(d),
                  _resident(wa.shape), _resident(wb.shape), _resident(wc.shape), _resident(wo.shape)],
        out_specs=tok(d),
        compiler_params=_params(("parallel", "parallel")),
        name="merge",
    )(x, mod3, ya, yb, yc, ga, gb, gc, wa, wb, wc, wo)


def _top_values(ref, row0, n_rows, count, out_ref, out_row0, floor):
    for k in range(count):
        cur = ref[row0:row0 + n_rows]
        m = jnp.max(cur, axis=0, keepdims=True)
        out_ref[out_row0 + k:out_row0 + k + 1] = m
        if k + 1 < count:
            ref[row0:row0 + n_rows] = jnp.where(cur == m, floor, cur)


def _batcher_network(n):
    pairs = []
    p = 1
    while p < n:
        k = p
        while k >= 1:
            for j in range(k % p, n - k, 2 * k):
                for i in range(min(k, n - j - k)):
                    if (i + j) // (2 * p) == (i + j + k) // (2 * p):
                        pairs.append((i + j, i + j + k))
            k //= 2
        p *= 2
    return tuple(pairs)


N_TOP = PEER_TOPK + 1
SUBLANES = 8
_SORT_NET = _batcher_network(PEER_KEYS // SUBLANES)
_CAND_COUNTS = tuple(N_TOP // (a + 1) for a in range(N_TOP))
_CAND_OFFS = tuple(int(v) for v in np.cumsum((0,) + _CAND_COUNTS))
N_CAND = -(-_CAND_OFFS[-1] // SUBLANES) * SUBLANES


def _top_sorted(s_ref, lanes, out_ref, slot):
    n_grp = PEER_KEYS // SUBLANES
    lists = [s_ref[r * SUBLANES:(r + 1) * SUBLANES, lanes] for r in range(n_grp)]
    for a, b in _SORT_NET:
        lists[a], lists[b] = jnp.maximum(lists[a], lists[b]), jnp.minimum(lists[a], lists[b])
    for t in range(N_TOP):
        head = lists[0]
        m = jnp.max(head, axis=0, keepdims=True)
        out_ref[slot, t:t + 1, lanes] = m
        remaining = N_TOP - 1 - t
        hit = head == m
        for k in range(remaining):
            below = lists[k + 1] if k + 1 < n_grp else -jnp.inf
            lists[k] = jnp.where(hit, below, lists[k])


def _route_kernel(x_ref, g_ref, sc_ref, sh_ref, wq_ref, keys_ref,
                  h_ref, e1_ref, e2_ref, thr_ref, s_ref, top_ref, cand_ref):
    tt = x_ref.shape[1]
    h2 = _rms_mod(x_ref[0], g_ref[...], sc_ref[0], sh_ref[0]).astype(MXU_DTYPE)
    h_ref[0] = h2
    q = _nn(h2, wq_ref[...]).astype(MXU_DTYPE)
    top_ref[:, N_TOP:, :] = jnp.full((3, top_ref.shape[1] - N_TOP, tt), -jnp.inf, F32)
    cand_ref[_CAND_OFFS[-1]:, :] = jnp.full((N_CAND - _CAND_OFFS[-1], tt), -1.0, F32)
    for h in range(PEER_HEADS):
        tops = []
        for p, e_ref in ((0, e1_ref), (1, e2_ref)):
            hp = 2 * h + p
            s_ref[...] = _nt(keys_ref[hp], q[:, hp * LANES:(hp + 1) * LANES])
            for tc in range(tt // LANES):
                _top_sorted(s_ref, slice(tc * LANES, (tc + 1) * LANES), top_ref, p)
            m = top_ref[p, 0:1]
            e_ref[0, h] = jnp.exp(s_ref[...] - m)
            tops.append(jnp.exp(top_ref[p] - m))
        e1_top, e2_top = tops
        for a in range(N_TOP):
            cand_ref[_CAND_OFFS[a]:_CAND_OFFS[a + 1]] = e1_top[a:a + 1] * e2_top[0:_CAND_COUNTS[a]]
        _top_values(cand_ref, 0, N_CAND, N_TOP, top_ref.at[2], 0, -1.0)
        best = top_ref[2]
        inv_z = 1.0 / jnp.sum(best[0:PEER_TOPK], axis=0, keepdims=True)
        e1_ref[0, h] = e1_ref[0, h] * inv_z
        thr_ref[0, h:h + 1] = 0.5 * (best[PEER_TOPK - 1:PEER_TOPK] + best[PEER_TOPK:N_TOP]) * inv_z


def _route(x, g2, mod3, mod_row, wq, keys):
    bsz, n, d = x.shape
    tt = min(TT_PEER, n)
    gate_shape = jax.ShapeDtypeStruct((bsz, PEER_HEADS, PEER_KEYS, n), F32)
    gate_spec = pl.BlockSpec((1, PEER_HEADS, PEER_KEYS, tt), lambda b, i: (b, 0, 0, i))
    return pl.pallas_call(
        _route_kernel,
        out_shape=[jax.ShapeDtypeStruct((bsz, n, d), MXU_DTYPE), gate_shape, gate_shape,
                   jax.ShapeDtypeStruct((bsz, PEER_HEADS, n), F32)],
        grid=(bsz, n // tt),
        in_specs=[pl.BlockSpec((1, tt, d), lambda b, i: (b, i, 0)),
                  pl.BlockSpec((1, d), lambda b, i: (0, 0)),
                  pl.BlockSpec((1, 1, d), lambda b, i: (mod_row(b), 0, 4)),
                  pl.BlockSpec((1, 1, d), lambda b, i: (mod_row(b), 0, 3)),
                  _resident(wq.shape), _resident(keys.shape)],
        out_specs=[pl.BlockSpec((1, tt, d), lambda b, i: (b, i, 0)), gate_spec, gate_spec,
                   pl.BlockSpec((1, PEER_HEADS, tt), lambda b, i: (b, 0, i))],
        scratch_shapes=[pltpu.VMEM((PEER_KEYS, tt), F32),
                        pltpu.VMEM((3, 3 * SUBLANES, tt), F32),
                        pltpu.VMEM((N_CAND, tt), F32)],
        compiler_params=_params(("parallel", "parallel")),
        name="peer_route",
    )(x, g2, mod3, mod3, wq, keys)


def _expert_kernel(x_ref, gt_ref, gf_ref, h_ref, u_ref, v_ref, e1_ref, e2_ref, thr_ref, o_ref,
                   acc_ref, a_ref, w_ref, *, final_norm):
    eb = pl.program_id(2)
    tt = h_ref.shape[1]

    @pl.when(eb == 0)
    def _zero():
        acc_ref[...] = jnp.zeros_like(acc_ref)

    n_i = TE_PEER // PEER_KEYS
    i0 = pl.multiple_of(eb * n_i, n_i)
    i_per_group = TE_GROUP // PEER_KEYS
    n_groups = TE_PEER // TE_GROUP
    sub = 64
    grows = lambda g: slice(g * TE_GROUP, (g + 1) * TE_GROUP)

    def score(g):
        a_ref[g] = _nt(u_ref[grows(g)], h_ref[0])

    def value(g):
        acc_ref[...] += _nn(w_ref[g].T.astype(MXU_DTYPE), v_ref[grows(g)])

    def gates(g):
        for ig in range(i_per_group):
            il = g * i_per_group + ig
            for tc in range(tt // LANES):
                lanes = slice(tc * LANES, (tc + 1) * LANES)
                for js in range(PEER_KEYS // sub):
                    rows = slice(js * sub, (js + 1) * sub)
                    gate = None
                    for h in range(PEER_HEADS):
                        e1_row = e1_ref[0, h, pl.ds(i0, n_i), lanes][il:il + 1]
                        prod = e2_ref[0, h, rows, lanes] * e1_row
                        sel = jnp.where(prod >= thr_ref[0, h:h + 1, lanes], prod, 0.0)
                        gate = sel if gate is None else gate + sel
                    erows = slice(ig * PEER_KEYS + js * sub, ig * PEER_KEYS + (js + 1) * sub)
                    a = a_ref[g, erows, lanes]
                    act = (0.5 * a) * (1.0 + lax.erf(a * math.sqrt(0.5)))
                    w_ref[g, erows, lanes] = gate * act

    score(0)
    for g in range(n_groups):
        if g + 1 < n_groups:
            score(g + 1)
        if g >= 1:
            value(g - 1)
        gates(g)
    value(n_groups - 1)

    @pl.when(eb == pl.num_programs(2) - 1)
    def _finish():
        y = x_ref[0] + gt_ref[0] * acc_ref[...]
        if final_norm:
            y = (y * lax.rsqrt(jnp.mean(y * y, axis=-1, keepdims=True) + NORM_EPS)) * gf_ref[...]
        o_ref[0] = y


def _experts(x, mod3, mod_row, g_final, h2, u, v, e1, e2, thr, final_norm):
    bsz, n, d = x.shape
    tt = min(TT_PEER, n)
    n_exp = u.shape[0]
    gate_spec = pl.BlockSpec((1, PEER_HEADS, PEER_KEYS, tt), lambda b, i, e: (b, 0, 0, i))
    return pl.pallas_call(
        functools.partial(_expert_kernel, final_norm=final_norm),
        out_shape=jax.ShapeDtypeStruct(x.shape, F32),
        grid=(bsz, n // tt, n_exp // TE_PEER),
        in_specs=[pl.BlockSpec((1, tt, d), lambda b, i, e: (b, i, 0)),
                  pl.BlockSpec((1, 1, d), lambda b, i, e: (mod_row(b), 0, 5)),
                  pl.BlockSpec((1, d), lambda b, i, e: (0, 0)),
                  pl.BlockSpec((1, tt, d), lambda b, i, e: (b, i, 0)),
                  pl.BlockSpec((TE_PEER, d), lambda b, i, e: (e, 0)),
                  pl.BlockSpec((TE_PEER, d), lambda b, i, e: (e, 0)),
                  gate_spec, gate_spec,
                  pl.BlockSpec((1, PEER_HEADS, tt), lambda b, i, e: (b, 0, i))],
        out_specs=pl.BlockSpec((1, tt, d), lambda b, i, e: (b, i, 0)),
        scratch_shapes=[pltpu.VMEM((tt, d), F32),
                        pltpu.VMEM((TE_PEER // TE_GROUP, TE_GROUP, tt), F32),
                        pltpu.VMEM((TE_PEER // TE_GROUP, TE_GROUP, tt), F32)],
        compiler_params=_params(("parallel", "parallel", "arbitrary")),
        name="peer_experts",
    )(x, mod3, g_final, h2, u, v, e1, e2, thr)


def _rope_tables(n):
    t = jnp.arange(n, dtype=jnp.int32)
    row = (t // GRID_W).astype(F32)
    col = (t % GRID_W).astype(F32)
    n_freq = HEAD_DIM // 4
    inv = ROPE_THETA ** (-jnp.arange(n_freq, dtype=F32) / n_freq)
    ang = jnp.concatenate([row[:, None] * inv, col[:, None] * inv], axis=-1)
    cos, sin = jnp.cos(ang), jnp.sin(ang)
    return jnp.tile(cos, (1, 4)), jnp.tile(jnp.concatenate([-sin, sin], axis=-1), (1, 2))


def kernel(x, c, ctx, c_ctx, w_mod, b_mod, g_norm1, g_norm2, w_in, a_sink, b_rpb, c_lambda,
           c_subln, w_branch_a, w_branch_b, w_branch_c, w_out, peer_wq, peer_keys, peer_u,
           peer_v, g_final):
    bsz, n, d = x.shape
    depth = w_mod.shape[0]
    n_ctx = ctx.shape[1]
    assert d == D_MODEL and bsz + 1 <= 8
    assert n % (B_ROWS * GRID_W) == 0 and n // GRID_W >= 4 * B_KROWS and n % TK_C == 0

    cvec = jnp.zeros((8, d), F32).at[:bsz].set(c).at[bsz].set(c_ctx)
    mod = _modulation(cvec, w_mod, b_mod)
    lat_row = lambda b: b
    ctx_row = lambda b: bsz
    cos_t, sin_t = _rope_tables(n)
    ones_t, zeros_t = jnp.ones((n_ctx, LANES), F32), jnp.zeros((n_ctx, LANES), F32)
    cast = lambda w: w.astype(MXU_DTYPE)
    gfin = g_final.reshape(1, d)

    xc = ctx
    for l in range(depth):
        last = l == depth - 1
        lam_init = 0.8 - 0.6 * math.exp(-0.3 * l)
        mod3 = mod[l].reshape(8, 1, 6 * d)
        g1, g2 = g_norm1[l].reshape(1, d), g_norm2[l].reshape(1, d)
        w_in_l = cast(w_in[l])
        subln3 = c_subln[l].reshape(C_HEADS, 1, LANES)
        wa, wb, wc, wo = cast(w_branch_a[l]), cast(w_branch_b[l]), cast(w_branch_c[l]), cast(w_out[l])
        wq = cast(peer_wq[l])
        keys = cast(peer_keys[l].reshape(2 * PEER_HEADS, PEER_KEYS, LANES))
        u, v = cast(peer_u[l]), cast(peer_v[l])

        lat = _in_proj(x, g1, mod3, lat_row, w_in_l, cos_t, sin_t)
        con = _in_proj(xc, g1, mod3, ctx_row, w_in_l, ones_t, zeros_t)
        (aq, ak, aks, av, avs, bq, bk, bv, cq, ck, _, ga, gb, gc, cvt) = lat
        ya = _attn_a(a_sink[l], aq, ak, aks, av, avs, con[1], con[2], con[3], con[4])
        yb = _attn_b(b_rpb[l], bq, bk, bv, con[6], con[7])
        yc = _attn_c(c_lambda[l], subln3, cq, ck, cvt, con[9], con[14], lam_init)
        x = _merge(x, mod3, lat_row, ya, yb, yc, ga, gb, gc, wa, wb, wc, wo)
        h2, e1, e2, thr = _route(x, g2, mod3, lat_row, wq, keys)
        x = _experts(x, mod3, lat_row, gfin, h2, u, v, e1, e2, thr, final_norm=last)

        if not last:
            ya_c, yb_c, yc_c = _ctx_attn(a_sink[l], c_lambda[l], subln3, con, lam_init)
            xc = _merge(xc, mod3, ctx_row, ya_c, yb_c, yc_c, con[11], con[12], con[13],
                        wa, wb, wc, wo)
            h2c, e1c, e2c, thrc = _route(xc, g2, mod3, ctx_row, wq, keys)
            xc = _experts(xc, mod3, ctx_row, gfin, h2c, u, v, e1c, e2c, thrc, final_norm=False)
    return x
```

```python
import functools
import math

import jax
import jax.numpy as jnp
import numpy as np
from jax import lax
from jax.experimental import pallas as pl
from jax.experimental.pallas import tpu as pltpu

F32 = jnp.float32
BF16 = jnp.bfloat16
MXU_DTYPE = BF16

LANES = 128
MXU_WIDTH = 256
VMEM_LIMIT_BYTES = 56 * 1024 * 1024

HEAD_DIM = 64
GRID_W = 64
ROPE_THETA = 10000.0
NORM_EPS = 1e-6
NEG_INF = -1e30
LOG2_E = math.log2(math.e)
A_HEADS, A_KV_HEADS, A_WINDOW, A_BLOCK = 8, 2, 128, 128
B_HEADS, NA_ROWS, NA_COLS = 8, 8, 16
C_HEADS = 4
PEER_HEADS, PEER_KEYS, PEER_TOPK = 8, 128, 16
D_MODEL = 1024
IN_SPLITS = (512, 128, 128, 512, 512, 512, 512, 512, 512, 1024, 1024, 1024)
IN_OFFS = tuple(int(v) for v in np.cumsum((0,) + IN_SPLITS))

TM_PROJ = 512
A_QBLOCKS = 4
TQ_C = 512
TK_C = 768
B_ROWS = 8
B_KROWS = 4
TT_PEER = 512
TE_PEER = 1024
N_PT = 30


def _nt(a, b):
    return lax.dot_general(a, b, (((1,), (1,)), ((), ())), preferred_element_type=F32)


def _nn(a, b):
    return jnp.dot(a, b, preferred_element_type=F32)


def _params(sem):
    return pltpu.CompilerParams(dimension_semantics=sem, vmem_limit_bytes=VMEM_LIMIT_BYTES)


def _resident(shape):
    nd = len(shape)
    return pl.BlockSpec(shape, lambda *_: (0,) * nd, pipeline_mode=pl.Buffered(1))


def _mod_kernel(c_ref, w_ref, b_ref, o_ref):
    c = c_ref[...]
    s = c * jax.nn.sigmoid(c)
    w = w_ref[0]
    s_hi = s.astype(MXU_DTYPE)
    s_lo = (s - s_hi.astype(F32)).astype(MXU_DTYPE)
    w_hi = w.astype(MXU_DTYPE)
    w_lo = (w - w_hi.astype(F32)).astype(MXU_DTYPE)
    acc = _nn(s_hi, w_hi) + _nn(s_lo, w_hi) + _nn(s_hi, w_lo)
    o_ref[0] = acc + b_ref[0]


def _modulation(cvec, w_mod, b_mod):
    depth, d, n6 = w_mod.shape
    tn = 1536
    return pl.pallas_call(
        _mod_kernel,
        out_shape=jax.ShapeDtypeStruct((depth, 8, n6), F32),
        grid=(depth, n6 // tn),
        in_specs=[pl.BlockSpec((8, d), lambda l, j: (0, 0)),
                  pl.BlockSpec((1, d, tn), lambda l, j: (l, 0, j)),
                  pl.BlockSpec((1, 1, tn), lambda l, j: (l, 0, j))],
        out_specs=pl.BlockSpec((1, 8, tn), lambda l, j: (l, 0, j)),
        compiler_params=_params(("parallel", "parallel")),
        name="modulation",
    )(cvec, w_mod, b_mod.reshape(depth, 1, n6))


def _rms_mod(x, g, sc, sh):
    y = x * lax.rsqrt(jnp.mean(x * x, axis=-1, keepdims=True) + NORM_EPS)
    return (y * g) * (1.0 + sc) + sh


def _in_proj_kernel(x_ref, g_ref, sc_ref, sh_ref, w_ref, cos_ref, sin_ref,
                    aq_ref, ak_ref, aks_ref, av_ref, avs_ref, bq_ref, bk_ref, bv_ref,
                    cq_ref, ck_ref, cv_ref, ga_ref, gb_ref, gc_ref, cvt_ref):
    h = _rms_mod(x_ref[0], g_ref[...], sc_ref[0], sh_ref[0]).astype(MXU_DTYPE)
    cos = cos_ref[...]
    sin = sin_ref[...]
    lane = lax.broadcasted_iota(jnp.int32, (1, LANES), 1)
    first_half = (lane % HEAD_DIM) < (HEAD_DIM // 2)
    scale = HEAD_DIM ** -0.5

    def proj(seg, jp):
        c0 = IN_OFFS[seg] + jp * MXU_WIDTH
        t = _nn(h, w_ref[:, c0:c0 + MXU_WIDTH])
        return t[:, :LANES], t[:, LANES:]

    def rope(v):
        rot = jnp.where(first_half, pltpu.roll(v, LANES - 32, 1), pltpu.roll(v, 32, 1))
        return v * cos + rot * sin

    def emit(ref, seg, fn=lambda v: v):
        for jp in range(IN_SPLITS[seg] // MXU_WIDTH):
            for k, part in enumerate(proj(seg, jp)):
                j = 2 * jp + k
                ref[0, :, j * LANES:(j + 1) * LANES] = fn(part).astype(ref.dtype)

    emit(aq_ref, 0, lambda v: rope(v) * scale)
    emit(bq_ref, 3, lambda v: v * scale)
    emit(bk_ref, 4)
    emit(bv_ref, 5)
    emit(cq_ref, 6, lambda v: rope(v) * (scale * LOG2_E))
    emit(ck_ref, 7, rope)
    for jp in range(IN_SPLITS[8] // MXU_WIDTH):
        for k, cv in enumerate(proj(8, jp)):
            j = 2 * jp + k
            cv_ref[0, :, j * LANES:(j + 1) * LANES] = cv.astype(cv_ref.dtype)
            cvt_ref[0, j * LANES:(j + 1) * LANES, :] = cv.T.astype(cvt_ref.dtype)
    ak, av = proj(1, 0)
    ak = rope(ak)
    ak_ref[0] = ak.astype(ak_ref.dtype)
    aks_ref[0] = pltpu.roll(ak, HEAD_DIM, 1).astype(aks_ref.dtype)
    av_ref[0] = av.astype(av_ref.dtype)
    avs_ref[0] = pltpu.roll(av, HEAD_DIM, 1).astype(avs_ref.dtype)
    emit(ga_ref, 9, jax.nn.sigmoid)
    emit(gb_ref, 10, jax.nn.sigmoid)
    emit(gc_ref, 11, jax.nn.sigmoid)


def _in_proj(x, g1, mod3, mod_row, w_in, cos_t, sin_t):
    bsz, n, d = x.shape
    tm = min(TM_PROJ, n)
    tok = lambda w: pl.BlockSpec((1, tm, w), lambda b, i: (b, i, 0))
    widths = (512, 128, 128, 128, 128, 512, 512, 512, 512, 512, 512, 1024, 1024, 1024)
    return pl.pallas_call(
        _in_proj_kernel,
        out_shape=[jax.ShapeDtypeStruct((bsz, n, w), MXU_DTYPE) for w in widths]
        + [jax.ShapeDtypeStruct((bsz, 512, n), MXU_DTYPE)],
        grid=(bsz, n // tm),
        in_specs=[tok(d),
                  pl.BlockSpec((1, d), lambda b, i: (0, 0)),
                  pl.BlockSpec((1, 1, d), lambda b, i: (mod_row(b), 0, 1)),
                  pl.BlockSpec((1, 1, d), lambda b, i: (mod_row(b), 0, 0)),
                  _resident(w_in.shape),
                  pl.BlockSpec((tm, LANES), lambda b, i: (i, 0)),
                  pl.BlockSpec((tm, LANES), lambda b, i: (i, 0))],
        out_specs=[tok(w) for w in widths] + [pl.BlockSpec((1, 512, tm), lambda b, i: (b, 0, i))],
        compiler_params=_params(("parallel", "parallel")),
        name="in_proj",
    )(x, g1, mod3, mod3, w_in, cos_t, sin_t)


def _lane_lo():
    return lax.broadcasted_iota(jnp.int32, (1, LANES), 1) < HEAD_DIM


def _softmax_pv(s_list, v_list, extra_logit=None, exp=jnp.exp):
    m = functools.reduce(jnp.maximum, [jnp.max(s, axis=-1, keepdims=True) for s in s_list])
    if extra_logit is not None:
        m = jnp.maximum(m, extra_logit)
    l = 0.0 if extra_logit is None else exp(extra_logit - m)
    o = 0.0
    for s, v in zip(s_list, v_list):
        p = exp(s - m)
        l = l + jnp.sum(p, axis=-1, keepdims=True)
        o = o + _nn(p.astype(v.dtype), v)
    return o / l


def _attn_a_kernel(sink_ref, q_ref, kp_ref, kc_ref, kn_ref, ksp_ref, ksc_ref, ksn_ref,
                   vp_ref, vc_ref, vn_ref, vsp_ref, vsc_ref, vsn_ref,
                   xk_ref, xks_ref, xv_ref, xvs_ref, o_ref, *, n_tokens):
    i = pl.program_id(1)
    lo = _lane_lo()
    cat = lambda refs: jnp.concatenate([r[0] for r in refs], axis=0)
    k_loc, ks_loc = cat((kp_ref, kc_ref, kn_ref)), cat((ksp_ref, ksc_ref, ksn_ref))
    v_loc, vs_loc = cat((vp_ref, vc_ref, vn_ref)), cat((vsp_ref, vsc_ref, vsn_ref))
    zero = jnp.zeros((), k_loc.dtype)
    k_of = {(0, 0): (jnp.where(lo, k_loc, zero), jnp.where(lo, xk_ref[0], zero)),
            (0, 1): (jnp.where(lo, zero, ks_loc), jnp.where(lo, zero, xks_ref[0])),
            (1, 0): (jnp.where(lo, ks_loc, zero), jnp.where(lo, xks_ref[0], zero)),
            (1, 1): (jnp.where(lo, zero, k_loc), jnp.where(lo, zero, xk_ref[0]))}
    v_of = {(0, 0): (v_loc, xv_ref[0]), (0, 1): (vs_loc, xvs_ref[0]),
            (1, 0): (vs_loc, xvs_ref[0]), (1, 1): (v_loc, xv_ref[0])}
    chains = []
    for qb in range(A_QBLOCKS):
        blk = i * A_QBLOCKS + qb
        qpos = blk * A_BLOCK + lax.broadcasted_iota(jnp.int32, (A_BLOCK, 1), 0)
        kpos = (blk - 1) * A_BLOCK + lax.broadcasted_iota(jnp.int32, (1, 3 * A_BLOCK), 1)
        dist = qpos - kpos
        valid = (jnp.maximum(dist, -dist) <= A_WINDOW) & (kpos >= 0) & (kpos < n_tokens)
        qrows = slice(qb * A_BLOCK, (qb + 1) * A_BLOCK)
        krows = slice(qb * A_BLOCK, (qb + 3) * A_BLOCK)
        for hp in range(A_HEADS // 2):
            qp = q_ref[0, qrows, hp * LANES:(hp + 1) * LANES]
            for half in range(2):
                h = 2 * hp + half
                g = h // (A_HEADS // A_KV_HEADS)
                (kl, kx), (vl, vx) = k_of[(g, half)], v_of[(g, half)]
                s_loc = jnp.where(valid, _nt(qp, kl[krows]), NEG_INF)
                chains.append(([s_loc, _nt(qp, kx)], [vl[krows], vx], sink_ref[h]))
    outs = [_softmax_pv(s, v, extra_logit=sink) for s, v, sink in chains]
    for qb in range(A_QBLOCKS):
        qrows = slice(qb * A_BLOCK, (qb + 1) * A_BLOCK)
        for hp in range(A_HEADS // 2):
            even, odd = outs[(qb * (A_HEADS // 2) + hp) * 2:(qb * (A_HEADS // 2) + hp) * 2 + 2]
            o_ref[0, qrows, hp * LANES:(hp + 1) * LANES] = jnp.where(lo, even, odd).astype(o_ref.dtype)


def _attn_a(sink, aq, ak, aks, av, avs, xak, xaks, xav, xavs):
    bsz, n, _ = aq.shape
    nb = n // A_BLOCK
    tq = A_QBLOCKS * A_BLOCK
    ctx = xak.shape[1]
    prev = pl.BlockSpec((1, A_BLOCK, LANES),
                        lambda b, i: (b, jnp.maximum(i * A_QBLOCKS - 1, 0), 0))
    cur = pl.BlockSpec((1, tq, LANES), lambda b, i: (b, i, 0))
    nxt = pl.BlockSpec((1, A_BLOCK, LANES),
                       lambda b, i: (b, jnp.minimum((i + 1) * A_QBLOCKS, nb - 1), 0))
    cx = pl.BlockSpec((1, ctx, LANES), lambda b, i: (b, 0, 0))
    return pl.pallas_call(
        functools.partial(_attn_a_kernel, n_tokens=n),
        out_shape=jax.ShapeDtypeStruct(aq.shape, MXU_DTYPE),
        grid=(bsz, n // tq),
        in_specs=[pl.BlockSpec(memory_space=pltpu.SMEM),
                  pl.BlockSpec((1, tq, 512), lambda b, i: (b, i, 0)),
                  prev, cur, nxt, prev, cur, nxt, prev, cur, nxt, prev, cur, nxt, cx, cx, cx, cx],
        out_specs=pl.BlockSpec((1, tq, 512), lambda b, i: (b, i, 0)),
        compiler_params=_params(("parallel", "parallel")),
        name="attn_window",
    )(sink, aq, ak, ak, ak, aks, aks, aks, av, av, av, avs, avs, avs, xak, xaks, xav, xavs)


def _attn_b_kernel(rpb_ref, q_ref, k0_ref, k1_ref, k2_ref, k3_ref, v0_ref, v1_ref, v2_ref, v3_ref,
                   xk_ref, xv_ref, o_ref, pt_ref, *, n_rows):
    pair, i = pl.program_id(0), pl.program_id(2)
    n_a = 2 * NA_ROWS - 1
    n_b = 2 * NA_COLS - 1

    @pl.when((pl.program_id(1) == 0) & (i == 0))
    def _build_bias_tables():
        qc = lax.broadcasted_iota(jnp.int32, (GRID_W, LANES), 0)
        ln = lax.broadcasted_iota(jnp.int32, (GRID_W, LANES), 1)
        kc = ln % GRID_W
        hi = ln >= GRID_W
        cstart = jnp.clip(qc - NA_COLS // 2, 0, GRID_W - NA_COLS)
        col_ok = (kc >= cstart) & (kc < cstart + NA_COLS)
        d = kc - qc + (NA_COLS - 1)
        for hh in range(2):
            h = pair * 2 + hh

            def body(ai, carry):
                a = ai - 8
                a_ok, a1_ok = (a >= 0) & (a < n_a), (a + 1 >= 0) & (a + 1 < n_a)
                ra, ra1 = h * n_a + jnp.clip(a, 0, n_a - 1), h * n_a + jnp.clip(a + 1, 0, n_a - 1)
                t = jnp.full((GRID_W, LANES), NEG_INF, F32)
                for b in range(n_b):
                    va = jnp.where(a_ok, rpb_ref[ra, b], NEG_INF)
                    va1 = jnp.where(a1_ok, rpb_ref[ra1, b], NEG_INF)
                    t = jnp.where(d == b, jnp.where(hi, va1, va), t)
                pt_ref[hh, ai] = jnp.where(col_ok, t, NEG_INF)
                return carry

            lax.fori_loop(0, N_PT, body, 0)

    lo = _lane_lo()
    r0 = i * B_ROWS
    kb0 = jnp.clip(2 * i - 1, 0, n_rows // B_KROWS - 4)
    krow_lane = lax.broadcasted_iota(jnp.int32, (1, B_KROWS * GRID_W), 1) // GRID_W
    k_refs, v_refs = (k0_ref, k1_ref, k2_ref, k3_ref), (v0_ref, v1_ref, v2_ref, v3_ref)
    qp = q_ref[0]
    zero = jnp.zeros((), qp.dtype)
    raw = []
    for hh in range(2):
        keep = lambda t: jnp.where(lo, t, zero) if hh == 0 else jnp.where(lo, zero, t)
        raw.append([_nt(qp, keep(k_refs[j][0, 0])) for j in range(4)]
                   + [_nt(qp, keep(xk_ref[0]))])
    outs = []
    for hh in range(2):
        s_tiles = []
        for j in range(4):
            s = raw[hh][j]
            kr0 = (kb0 + j) * B_KROWS
            rows = []
            for qr in range(B_ROWS):
                r = r0 + qr
                rs = jnp.clip(r - NA_ROWS // 2, 0, n_rows - NA_ROWS)
                a0 = kr0 - r + (NA_ROWS - 1)
                bias = jnp.concatenate([pt_ref[hh, a0 + 8], pt_ref[hh, a0 + 10]], axis=1)
                row_ok = (kr0 + krow_lane >= rs) & (kr0 + krow_lane < rs + NA_ROWS)
                rows.append(jnp.where(row_ok, s[qr * GRID_W:(qr + 1) * GRID_W] + bias, NEG_INF))
            s_tiles.append(jnp.concatenate(rows, axis=0))
        s_tiles.append(raw[hh][4])
        outs.append(_softmax_pv(s_tiles, [v_refs[j][0, 0] for j in range(4)] + [xv_ref[0]]))
    o_ref[0] = jnp.where(lo, outs[0], outs[1]).astype(o_ref.dtype)


def _attn_b(rpb, bq, bk, bv, xbk, xbv):
    bsz, n, _ = bq.shape
    n_rows = n // GRID_W
    tq = B_ROWS * GRID_W
    tkb = B_KROWS * GRID_W
    ctx = xbk.shape[1]
    bk4 = bk.reshape(bsz, n // tkb, tkb, 512)
    bv4 = bv.reshape(bsz, n // tkb, tkb, 512)
    nkb = n // tkb

    def kspec(j):
        return pl.BlockSpec((1, 1, tkb, LANES),
                            lambda p, b, i: (b, jnp.clip(2 * i - 1, 0, nkb - 4) + j, 0, p))

    cx = pl.BlockSpec((1, ctx, LANES), lambda p, b, i: (b, 0, p))
    return pl.pallas_call(
        functools.partial(_attn_b_kernel, n_rows=n_rows),
        out_shape=jax.ShapeDtypeStruct(bq.shape, MXU_DTYPE),
        grid=(B_HEADS // 2, bsz, n // tq),
        in_specs=[pl.BlockSpec(memory_space=pltpu.SMEM),
                  pl.BlockSpec((1, tq, LANES), lambda p, b, i: (b, i, p)),
                  kspec(0), kspec(1), kspec(2), kspec(3), kspec(0), kspec(1), kspec(2), kspec(3),
                  cx, cx],
        out_specs=pl.BlockSpec((1, tq, LANES), lambda p, b, i: (b, i, p)),
        scratch_shapes=[pltpu.VMEM((2, N_PT, GRID_W, LANES), F32)],
        compiler_params=_params(("arbitrary", "arbitrary", "arbitrary")),
        name="attn_neighbourhood",
    )(rpb.reshape(B_HEADS * (2 * NA_ROWS - 1), 2 * NA_COLS - 1), bq,
      bk4, bk4, bk4, bk4, bv4, bv4, bv4, bv4, xbk, xbv)


def _diff_lambda(lam_ref, lam_init):
    lam = lam_ref[...]
    a = jnp.sum(lam[0:1] * lam[1:2], axis=-1, keepdims=True)
    b = jnp.sum(lam[2:3] * lam[3:4], axis=-1, keepdims=True)
    return jnp.exp(a) - jnp.exp(b) + lam_init


def _head_norm(o, g, lam_init):
    y = o * lax.rsqrt(jnp.mean(o * o, axis=-1, keepdims=True) + NORM_EPS)
    return (y * g) * (1.0 - lam_init)


def _attn_c_kernel(lam_ref, g_ref, q_ref, k_ref, vt_ref, xk_ref, xvt_ref, o_ref,
                   k1_ref, k2_ref, vall_ref, s_ref, p_ref, stat_ref, acc_ref,
                   *, lam_init, n_tokens, tk, n_chunks):
    lo = _lane_lo()

    @pl.when(pl.program_id(2) == 0)
    def _split_keys():
        zero = jnp.zeros((), k1_ref.dtype)
        k1_ref[0:n_tokens] = jnp.where(lo, k_ref[0], zero)
        k2_ref[0:n_tokens] = jnp.where(lo, zero, k_ref[0])
        k1_ref[n_tokens:] = jnp.where(lo, xk_ref[0], zero)
        k2_ref[n_tokens:] = jnp.where(lo, zero, xk_ref[0])

        vall_ref[:, 0:n_tokens] = vt_ref[0]
        vall_ref[:, n_tokens:] = xvt_ref[0]

    q = q_ref[0]
    tq = q.shape[0]
    key_refs = (k1_ref, k2_ref)
    stat_ref[:, 0:1, :] = jnp.full((2, 1, tq), -jnp.inf, F32)
    stat_ref[:, 1:2, :] = jnp.zeros((2, 1, tq), F32)
    acc_ref[...] = jnp.zeros_like(acc_ref)

    def scores(t, slot):
        off = pl.multiple_of(t * tk, LANES)
        for c in range(2):
            s = _nt(key_refs[c][pl.ds(off, tk), :], q)
            s_ref[slot, c] = s
            stat_ref[c, 4 + slot:5 + slot, :] = jnp.max(s, axis=0, keepdims=True)

    def softmax(slot):
        for c in range(2):
            s = s_ref[slot, c]
            m_old = stat_ref[c, 0:1, :]
            m_new = jnp.maximum(m_old, stat_ref[c, 4 + slot:5 + slot, :])
            alpha = jnp.exp2(m_old - m_new)
            p = jnp.exp2(s - m_new)
            p_ref[slot, c] = p.astype(p_ref.dtype)
            stat_ref[c, 0:1, :] = m_new
            stat_ref[c, 1:2, :] = alpha * stat_ref[c, 1:2, :] + jnp.sum(p, axis=0, keepdims=True)
            stat_ref[c, 2 + slot:3 + slot, :] = alpha

    def values(t, slot):
        off = pl.multiple_of(t * tk, LANES)
        vt = vall_ref[:, pl.ds(off, tk)]
        for c in range(2):
            acc_ref[c] = stat_ref[c, 2 + slot:3 + slot, :] * acc_ref[c] + _nn(vt, p_ref[slot, c])

    def iteration(t, parity):
        if not isinstance(t, int) or t < n_chunks:
            scores(t, parity)
        if not isinstance(t, int) or t >= 2:
            values(t - 2, parity)
        if not isinstance(t, int) or 1 <= t <= n_chunks:
            softmax(1 - parity)

    iteration(0, 0)
    iteration(1, 1)
    n_mid = n_chunks - 2

    def pair(i, carry):
        iteration(2 + 2 * i, 0)
        iteration(3 + 2 * i, 1)
        return carry

    lax.fori_loop(0, n_mid // 2, pair, 0)
    for t in range(2 + 2 * (n_mid // 2), n_chunks + 2):
        iteration(t, t % 2)
    inv_l = 1.0 / stat_ref[:, 1:2, :]
    ot = acc_ref[0] * inv_l[0] - _diff_lambda(lam_ref, lam_init) * (acc_ref[1] * inv_l[1])
    o_ref[0] = _head_norm(ot.T, g_ref[0], lam_init).astype(o_ref.dtype)


def _key_chunk(n_keys):
    return max(c for c in range(LANES, TK_C + 1, LANES) if n_keys % c == 0)


def _attn_c(c_lambda, subln3, cq, ck, cvt, xck, xcvt, lam_init):
    bsz, n, _ = cq.shape
    ctx = xck.shape[1]
    tq = min(TQ_C, n)
    tk = _key_chunk(n + ctx)
    return pl.pallas_call(
        functools.partial(_attn_c_kernel, lam_init=lam_init, n_tokens=n, tk=tk,
                          n_chunks=(n + ctx) // tk),
        out_shape=jax.ShapeDtypeStruct(cq.shape, MXU_DTYPE),
        grid=(bsz, C_HEADS, n // tq),
        in_specs=[pl.BlockSpec((4, HEAD_DIM), lambda b, h, i: (0, 0)),
                  pl.BlockSpec((1, 1, LANES), lambda b, h, i: (h, 0, 0)),
                  pl.BlockSpec((1, tq, LANES), lambda b, h, i: (b, i, h)),
                  pl.BlockSpec((1, n, LANES), lambda b, h, i: (b, 0, h)),
                  pl.BlockSpec((1, LANES, n), lambda b, h, i: (b, h, 0)),
                  pl.BlockSpec((1, ctx, LANES), lambda b, h, i: (b, 0, h)),
                  pl.BlockSpec((1, LANES, ctx), lambda b, h, i: (b, h, 0))],
        out_specs=pl.BlockSpec((1, tq, LANES), lambda b, h, i: (b, i, h)),
        scratch_shapes=[pltpu.VMEM((n + ctx, LANES), MXU_DTYPE),
                        pltpu.VMEM((n + ctx, LANES), MXU_DTYPE),
                        pltpu.VMEM((LANES, n + ctx), MXU_DTYPE),
                        pltpu.VMEM((2, 2, tk, tq), F32),
                        pltpu.VMEM((2, 2, tk, tq), MXU_DTYPE),
                        pltpu.VMEM((2, 8, tq), F32),
                        pltpu.VMEM((2, LANES, tq), F32)],
        compiler_params=_params(("arbitrary", "arbitrary", "arbitrary")),
        name="attn_differential",
    )(c_lambda, subln3, cq, ck, cvt, xck, xcvt)


def _ctx_attn_kernel(sink_ref, lam_ref, g_ref, aq_ref, ak_ref, aks_ref, av_ref, avs_ref,
                     bq_ref, bk_ref, bv_ref, cq_ref, ck_ref, cv_ref, ya_ref, yb_ref, yc_ref,
                     *, lam_init):
    lo = _lane_lo()
    zero = jnp.zeros((), ak_ref.dtype)
    sel = lambda t, half: jnp.where(lo, t, zero) if half == 0 else jnp.where(lo, zero, t)
    k_of = {(0, 0): sel(ak_ref[0], 0), (0, 1): sel(aks_ref[0], 1),
            (1, 0): sel(aks_ref[0], 0), (1, 1): sel(ak_ref[0], 1)}
    v_of = {(0, 0): av_ref[0], (0, 1): avs_ref[0], (1, 0): avs_ref[0], (1, 1): av_ref[0]}
    for hp in range(A_HEADS // 2):
        cols = slice(hp * LANES, (hp + 1) * LANES)
        qa, qb = aq_ref[0, :, cols], bq_ref[0, :, cols]
        kb, vb = bk_ref[0, :, cols], bv_ref[0, :, cols]
        oa, ob = [], []
        for half in range(2):
            h = 2 * hp + half
            g = h // (A_HEADS // A_KV_HEADS)
            oa.append(_softmax_pv([_nt(qa, k_of[(g, half)])], [v_of[(g, half)]],
                                  extra_logit=sink_ref[h]))
            ob.append(_softmax_pv([_nt(qb, sel(kb, half))], [vb]))
        ya_ref[0, :, cols] = jnp.where(lo, oa[0], oa[1]).astype(ya_ref.dtype)
        yb_ref[0, :, cols] = jnp.where(lo, ob[0], ob[1]).astype(yb_ref.dtype)
    lam = _diff_lambda(lam_ref, lam_init)
    for h in range(C_HEADS):
        cols = slice(h * LANES, (h + 1) * LANES)
        q, k, v = cq_ref[0, :, cols], ck_ref[0, :, cols], cv_ref[0, :, cols]
        o = (_softmax_pv([_nt(q, sel(k, 0))], [v], exp=jnp.exp2)
             - lam * _softmax_pv([_nt(q, sel(k, 1))], [v], exp=jnp.exp2))
        yc_ref[0, :, cols] = _head_norm(o, g_ref[h], lam_init).astype(yc_ref.dtype)


def _ctx_attn(sink, c_lambda, subln3, con, lam_init):
    aq, ak, aks, av, avs, bq, bk, bv, cq, ck, cv = con[:11]
    bsz, ctx, _ = aq.shape
    wide = pl.BlockSpec((1, ctx, 512), lambda b: (b, 0, 0))
    nar = pl.BlockSpec((1, ctx, LANES), lambda b: (b, 0, 0))
    return pl.pallas_call(
        functools.partial(_ctx_attn_kernel, lam_init=lam_init),
        out_shape=[jax.ShapeDtypeStruct(aq.shape, MXU_DTYPE)] * 3,
        grid=(bsz,),
        in_specs=[pl.BlockSpec(memory_space=pltpu.SMEM),
                  pl.BlockSpec((4, HEAD_DIM), lambda b: (0, 0)),
                  pl.BlockSpec((C_HEADS, 1, LANES), lambda b: (0, 0, 0)),
                  wide, nar, nar, nar, nar, wide, wide, wide, wide, wide, wide],
        out_specs=[wide] * 3,
        compiler_params=_params(("parallel",)),
        name="attn_context",
    )(sink, c_lambda, subln3, aq, ak, aks, av, avs, bq, bk, bv, cq, ck, cv)


def _merge_kernel(x_ref, gt_ref, ya_ref, yb_ref, yc_ref, ga_ref, gb_ref, gc_ref,
                  wa_ref, wb_ref, wc_ref, wo_ref, o_ref):
    m = (ga_ref[0].astype(F32) * _nn(ya_ref[0], wa_ref[...])
         + gb_ref[0].astype(F32) * _nn(yb_ref[0], wb_ref[...])
         + gc_ref[0].astype(F32) * _nn(yc_ref[0], wc_ref[...]))
    o_ref[0] = x_ref[0] + gt_ref[0] * _nn(m.astype(MXU_DTYPE), wo_ref[...])


def _merge(x, mod3, mod_row, ya, yb, yc, ga, gb, gc, wa, wb, wc, wo):
    bsz, n, d = x.shape
    tm = min(TM_PROJ, n)
    tok = lambda w: pl.BlockSpec((1, tm, w), lambda b, i: (b, i, 0))
    return pl.pallas_call(
        _merge_kernel,
        out_shape=jax.ShapeDtypeStruct(x.shape, F32),
        grid=(bsz, n // tm),
        in_specs=[tok(d), pl.BlockSpec((1, 1, d), lambda b, i: (mod_row(b), 0, 2)),
                  tok(512), tok(512), tok(512), tok(d), tok(d), tok(d),
                  _resident(wa.shape), _resident(wb.shape), _resident(wc.shape), _resident(wo.shape)],
        out_specs=tok(d),
        compiler_params=_params(("parallel", "parallel")),
        name="merge",
    )(x, mod3, ya, yb, yc, ga, gb, gc, wa, wb, wc, wo)


def _top_values(ref, row0, n_rows, count, out_ref, out_row0, floor):
    for k in range(count):
        cur = ref[row0:row0 + n_rows]
        m = jnp.max(cur, axis=0, keepdims=True)
        out_ref[out_row0 + k:out_row0 + k + 1] = m
        if k + 1 < count:
            ref[row0:row0 + n_rows] = jnp.where(cur == m, floor, cur)


def _batcher_network(n):
    pairs = []
    p = 1
    while p < n:
        k = p
        while k >= 1:
            for j in range(k % p, n - k, 2 * k):
                for i in range(min(k, n - j - k)):
                    if (i + j) // (2 * p) == (i + j + k) // (2 * p):
                        pairs.append((i + j, i + j + k))
            k //= 2
        p *= 2
    return tuple(pairs)


N_TOP = PEER_TOPK + 1
SUBLANES = 8
_SORT_NET = _batcher_network(PEER_KEYS // SUBLANES)
_CAND_COUNTS = tuple(N_TOP // (a + 1) for a in range(N_TOP))
_CAND_OFFS = tuple(int(v) for v in np.cumsum((0,) + _CAND_COUNTS))
N_CAND = -(-_CAND_OFFS[-1] // SUBLANES) * SUBLANES


def _top_sorted(s_ref, lanes, out_ref, slot):
    n_grp = PEER_KEYS // SUBLANES
    lists = [s_ref[r * SUBLANES:(r + 1) * SUBLANES, lanes] for r in range(n_grp)]
    for a, b in _SORT_NET:
        lists[a], lists[b] = jnp.maximum(lists[a], lists[b]), jnp.minimum(lists[a], lists[b])
    for t in range(N_TOP):
        head = lists[0]
        m = jnp.max(head, axis=0, keepdims=True)
        out_ref[slot, t:t + 1, lanes] = m
        remaining = N_TOP - 1 - t
        hit = head == m
        for k in range(remaining):
            below = lists[k + 1] if k + 1 < n_grp else -jnp.inf
            lists[k] = jnp.where(hit, below, lists[k])


def _route_kernel(x_ref, g_ref, sc_ref, sh_ref, wq_ref, keys_ref,
                  h_ref, e1_ref, e2_ref, thr_ref, s_ref, top_ref, cand_ref):
    tt = x_ref.shape[1]
    h2 = _rms_mod(x_ref[0], g_ref[...], sc_ref[0], sh_ref[0]).astype(MXU_DTYPE)
    h_ref[0] = h2
    q = _nn(h2, wq_ref[...]).astype(MXU_DTYPE)
    top_ref[:, N_TOP:, :] = jnp.full((3, top_ref.shape[1] - N_TOP, tt), -jnp.inf, F32)
    cand_ref[_CAND_OFFS[-1]:, :] = jnp.full((N_CAND - _CAND_OFFS[-1], tt), -1.0, F32)
    for h in range(PEER_HEADS):
        tops = []
        for p, e_ref in ((0, e1_ref), (1, e2_ref)):
            hp = 2 * h + p
            s_ref[...] = _nt(keys_ref[hp], q[:, hp * LANES:(hp + 1) * LANES])
            for tc in range(tt // LANES):
                _top_sorted(s_ref, slice(tc * LANES, (tc + 1) * LANES), top_ref, p)
            m = top_ref[p, 0:1]
            e_ref[0, h] = jnp.exp(s_ref[...] - m)
            tops.append(jnp.exp(top_ref[p] - m))
        e1_top, e2_top = tops
        for a in range(N_TOP):
            cand_ref[_CAND_OFFS[a]:_CAND_OFFS[a + 1]] = e1_top[a:a + 1] * e2_top[0:_CAND_COUNTS[a]]
        _top_values(cand_ref, 0, N_CAND, N_TOP, top_ref.at[2], 0, -1.0)
        best = top_ref[2]
        inv_z = 1.0 / jnp.sum(best[0:PEER_TOPK], axis=0, keepdims=True)
        e1_ref[0, h] = e1_ref[0, h] * inv_z
        thr_ref[0, h:h + 1] = 0.5 * (best[PEER_TOPK - 1:PEER_TOPK] + best[PEER_TOPK:N_TOP]) * inv_z


def _route(x, g2, mod3, mod_row, wq, keys):
    bsz, n, d = x.shape
    tt = min(TT_PEER, n)
    gate_shape = jax.ShapeDtypeStruct((bsz, PEER_HEADS, PEER_KEYS, n), F32)
    gate_spec = pl.BlockSpec((1, PEER_HEADS, PEER_KEYS, tt), lambda b, i: (b, 0, 0, i))
    return pl.pallas_call(
        _route_kernel,
        out_shape=[jax.ShapeDtypeStruct((bsz, n, d), MXU_DTYPE), gate_shape, gate_shape,
                   jax.ShapeDtypeStruct((bsz, PEER_HEADS, n), F32)],
        grid=(bsz, n // tt),
        in_specs=[pl.BlockSpec((1, tt, d), lambda b, i: (b, i, 0)),
                  pl.BlockSpec((1, d), lambda b, i: (0, 0)),
                  pl.BlockSpec((1, 1, d), lambda b, i: (mod_row(b), 0, 4)),
                  pl.BlockSpec((1, 1, d), lambda b, i: (mod_row(b), 0, 3)),
                  _resident(wq.shape), _resident(keys.shape)],
        out_specs=[pl.BlockSpec((1, tt, d), lambda b, i: (b, i, 0)), gate_spec, gate_spec,
                   pl.BlockSpec((1, PEER_HEADS, tt), lambda b, i: (b, 0, i))],
        scratch_shapes=[pltpu.VMEM((PEER_KEYS, tt), F32),
                        pltpu.VMEM((3, 3 * SUBLANES, tt), F32),
                        pltpu.VMEM((N_CAND, tt), F32)],
        compiler_params=_params(("parallel", "parallel")),
        name="peer_route",
    )(x, g2, mod3, mod3, wq, keys)


def _expert_kernel(x_ref, gt_ref, gf_ref, h_ref, u_ref, v_ref, e1_ref, e2_ref, thr_ref, o_ref,
                   acc_ref, a_ref, w_ref, *, final_norm):
    eb = pl.program_id(2)
    tt = h_ref.shape[1]

    @pl.when(eb == 0)
    def _zero():
        acc_ref[...] = jnp.zeros_like(acc_ref)

    a_ref[...] = _nt(u_ref[...], h_ref[0])
    sub = 64
    n_i = TE_PEER // PEER_KEYS
    i0 = pl.multiple_of(eb * n_i, n_i)
    for il in range(n_i):
        for tc in range(tt // LANES):
            lanes = slice(tc * LANES, (tc + 1) * LANES)
            e1_rows = [e1_ref[0, h, pl.ds(i0, n_i), lanes][il:il + 1] for h in range(PEER_HEADS)]
            thr_rows = [thr_ref[0, h:h + 1, lanes] for h in range(PEER_HEADS)]
            for js in range(PEER_KEYS // sub):
                rows = slice(js * sub, (js + 1) * sub)
                gate = None
                for h in range(PEER_HEADS):
                    prod = e2_ref[0, h, rows, lanes] * e1_rows[h]
                    sel = jnp.where(prod >= thr_rows[h], prod, 0.0)
                    gate = sel if gate is None else gate + sel
                erows = slice(il * PEER_KEYS + js * sub, il * PEER_KEYS + (js + 1) * sub)
                a = a_ref[erows, lanes]
                act = (0.5 * a) * (1.0 + lax.erf(a * math.sqrt(0.5)))
                w_ref[erows, lanes] = gate * act
    acc_ref[...] += _nn(w_ref[...].T.astype(MXU_DTYPE), v_ref[...])

    @pl.when(eb == pl.num_programs(2) - 1)
    def _finish():
        y = x_ref[0] + gt_ref[0] * acc_ref[...]
        if final_norm:
            y = (y * lax.rsqrt(jnp.mean(y * y, axis=-1, keepdims=True) + NORM_EPS)) * gf_ref[...]
        o_ref[0] = y


def _experts(x, mod3, mod_row, g_final, h2, u, v, e1, e2, thr, final_norm):
    bsz, n, d = x.shape
    tt = min(TT_PEER, n)
    n_exp = u.shape[0]
    gate_spec = pl.BlockSpec((1, PEER_HEADS, PEER_KEYS, tt), lambda b, i, e: (b, 0, 0, i))
    return pl.pallas_call(
        functools.partial(_expert_kernel, final_norm=final_norm),
        out_shape=jax.ShapeDtypeStruct(x.shape, F32),
        grid=(bsz, n // tt, n_exp // TE_PEER),
        in_specs=[pl.BlockSpec((1, tt, d), lambda b, i, e: (b, i, 0)),
                  pl.BlockSpec((1, 1, d), lambda b, i, e: (mod_row(b), 0, 5)),
                  pl.BlockSpec((1, d), lambda b, i, e: (0, 0)),
                  pl.BlockSpec((1, tt, d), lambda b, i, e: (b, i, 0)),
                  pl.BlockSpec((TE_PEER, d), lambda b, i, e: (e, 0)),
                  pl.BlockSpec((TE_PEER, d), lambda b, i, e: (e, 0)),
                  gate_spec, gate_spec,
                  pl.BlockSpec((1, PEER_HEADS, tt), lambda b, i, e: (b, 0, i))],
        out_specs=pl.BlockSpec((1, tt, d), lambda b, i, e: (b, i, 0)),
        scratch_shapes=[pltpu.VMEM((tt, d), F32),
                        pltpu.VMEM((TE_PEER, tt), F32),
                        pltpu.VMEM((TE_PEER, tt), F32)],
        compiler_params=_params(("parallel", "parallel", "arbitrary")),
        name="peer_experts",
    )(x, mod3, g_final, h2, u, v, e1, e2, thr)


def _rope_tables(n):
    t = jnp.arange(n, dtype=jnp.int32)
    row = (t // GRID_W).astype(F32)
    col = (t % GRID_W).astype(F32)
    n_freq = HEAD_DIM // 4
    inv = ROPE_THETA ** (-jnp.arange(n_freq, dtype=F32) / n_freq)
    ang = jnp.concatenate([row[:, None] * inv, col[:, None] * inv], axis=-1)
    cos, sin = jnp.cos(ang), jnp.sin(ang)
    return jnp.tile(cos, (1, 4)), jnp.tile(jnp.concatenate([-sin, sin], axis=-1), (1, 2))


def kernel(x, c, ctx, c_ctx, w_mod, b_mod, g_norm1, g_norm2, w_in, a_sink, b_rpb, c_lambda,
           c_subln, w_branch_a, w_branch_b, w_branch_c, w_out, peer_wq, peer_keys, peer_u,
           peer_v, g_final):
    bsz, n, d = x.shape
    depth = w_mod.shape[0]
    n_ctx = ctx.shape[1]
    assert d == D_MODEL and bsz + 1 <= 8
    assert n % (B_ROWS * GRID_W) == 0 and n // GRID_W >= 4 * B_KROWS
    assert n % (A_QBLOCKS * A_BLOCK) == 0 and (n + n_ctx) % LANES == 0

    cvec = jnp.zeros((8, d), F32).at[:bsz].set(c).at[bsz].set(c_ctx)
    mod = _modulation(cvec, w_mod, b_mod)
    lat_row = lambda b: b
    ctx_row = lambda b: bsz
    cos_t, sin_t = _rope_tables(n)
    ones_t, zeros_t = jnp.ones((n_ctx, LANES), F32), jnp.zeros((n_ctx, LANES), F32)
    cast = lambda w: w.astype(MXU_DTYPE)
    gfin = g_final.reshape(1, d)

    xc = ctx
    for l in range(depth):
        last = l == depth - 1
        lam_init = 0.8 - 0.6 * math.exp(-0.3 * l)
        mod3 = mod[l].reshape(8, 1, 6 * d)
        g1, g2 = g_norm1[l].reshape(1, d), g_norm2[l].reshape(1, d)
        w_in_l = cast(w_in[l])
        subln3 = c_subln[l].reshape(C_HEADS, 1, LANES)
        wa, wb, wc, wo = cast(w_branch_a[l]), cast(w_branch_b[l]), cast(w_branch_c[l]), cast(w_out[l])
        wq = cast(peer_wq[l])
        keys = cast(peer_keys[l].reshape(2 * PEER_HEADS, PEER_KEYS, LANES))
        u, v = cast(peer_u[l]), cast(peer_v[l])

        lat = _in_proj(x, g1, mod3, lat_row, w_in_l, cos_t, sin_t)
        con = _in_proj(xc, g1, mod3, ctx_row, w_in_l, ones_t, zeros_t)
        (aq, ak, aks, av, avs, bq, bk, bv, cq, ck, _, ga, gb, gc, cvt) = lat
        ya = _attn_a(a_sink[l], aq, ak, aks, av, avs, con[1], con[2], con[3], con[4])
        yb = _attn_b(b_rpb[l], bq, bk, bv, con[6], con[7])
        yc = _attn_c(c_lambda[l], subln3, cq, ck, cvt, con[9], con[14], lam_init)
        x = _merge(x, mod3, lat_row, ya, yb, yc, ga, gb, gc, wa, wb, wc, wo)
        h2, e1, e2, thr = _route(x, g2, mod3, lat_row, wq, keys)
        x = _experts(x, mod3, lat_row, gfin, h2, u, v, e1, e2, thr, final_norm=last)

        if not last:
            ya_c, yb_c, yc_c = _ctx_attn(a_sink[l], c_lambda[l], subln3, con, lam_init)
            xc = _merge(xc, mod3, ctx_row, ya_c, yb_c, yc_c, con[11], con[12], con[13],
                        wa, wb, wc, wo)
            h2c, e1c, e2c, thrc = _route(xc, g2, mod3, ctx_row, wq, keys)
            xc = _experts(xc, mod3, ctx_row, gfin, h2c, u, v, e1c, e2c, thrc, final_norm=False)
    return x
```

```python
import functools
import math

import jax
import jax.numpy as jnp
import numpy as np
from jax import lax
from jax.experimental import pallas as pl
from jax.experimental.pallas import tpu as pltpu

F32 = jnp.float32
BF16 = jnp.bfloat16
MXU_DTYPE = BF16

LANES = 128
MXU_WIDTH = 256
VMEM_LIMIT_BYTES = 56 * 1024 * 1024

HEAD_DIM = 64
GRID_W = 64
ROPE_THETA = 10000.0
NORM_EPS = 1e-6
NEG_INF = -1e30
LOG2_E = math.log2(math.e)
A_HEADS, A_KV_HEADS, A_WINDOW, A_BLOCK = 8, 2, 128, 128
B_HEADS, NA_ROWS, NA_COLS = 8, 8, 16
C_HEADS = 4
PEER_HEADS, PEER_KEYS, PEER_TOPK = 8, 128, 16
D_MODEL = 1024
IN_SPLITS = (512, 128, 128, 512, 512, 512, 512, 512, 512, 1024, 1024, 1024)
IN_OFFS = tuple(int(v) for v in np.cumsum((0,) + IN_SPLITS))

TM_PROJ = 512
A_QBLOCKS = 4
TQ_C = 512
TK_C = 768
B_ROWS = 8
B_KROWS = 4
TT_PEER = 512
TE_PEER = 1024
N_PT = 30


def _nt(a, b):
    return lax.dot_general(a, b, (((1,), (1,)), ((), ())), preferred_element_type=F32)


def _nn(a, b):
    return jnp.dot(a, b, preferred_element_type=F32)


def _params(sem):
    return pltpu.CompilerParams(dimension_semantics=sem, vmem_limit_bytes=VMEM_LIMIT_BYTES)


def _resident(shape):
    nd = len(shape)
    return pl.BlockSpec(shape, lambda *_: (0,) * nd, pipeline_mode=pl.Buffered(1))


def _mod_kernel(c_ref, w_ref, b_ref, o_ref):
    c = c_ref[...]
    s = c * jax.nn.sigmoid(c)
    w = w_ref[0]
    s_hi = s.astype(MXU_DTYPE)
    s_lo = (s - s_hi.astype(F32)).astype(MXU_DTYPE)
    w_hi = w.astype(MXU_DTYPE)
    w_lo = (w - w_hi.astype(F32)).astype(MXU_DTYPE)
    acc = _nn(s_hi, w_hi) + _nn(s_lo, w_hi) + _nn(s_hi, w_lo)
    o_ref[0] = acc + b_ref[0]


def _modulation(cvec, w_mod, b_mod):
    depth, d, n6 = w_mod.shape
    tn = 1536
    return pl.pallas_call(
        _mod_kernel,
        out_shape=jax.ShapeDtypeStruct((depth, 8, n6), F32),
        grid=(depth, n6 // tn),
        in_specs=[pl.BlockSpec((8, d), lambda l, j: (0, 0)),
                  pl.BlockSpec((1, d, tn), lambda l, j: (l, 0, j)),
                  pl.BlockSpec((1, 1, tn), lambda l, j: (l, 0, j))],
        out_specs=pl.BlockSpec((1, 8, tn), lambda l, j: (l, 0, j)),
        compiler_params=_params(("parallel", "parallel")),
        name="modulation",
    )(cvec, w_mod, b_mod.reshape(depth, 1, n6))


def _rms_mod(x, g, sc, sh):
    y = x * lax.rsqrt(jnp.mean(x * x, axis=-1, keepdims=True) + NORM_EPS)
    return (y * g) * (1.0 + sc) + sh


def _in_proj_kernel(x_ref, g_ref, sc_ref, sh_ref, w_ref, cos_ref, sin_ref,
                    aq_ref, ak_ref, aks_ref, av_ref, avs_ref, bq_ref, bk_ref, bv_ref,
                    cq_ref, ck_ref, cv_ref, ga_ref, gb_ref, gc_ref, cvt_ref):
    h = _rms_mod(x_ref[0], g_ref[...], sc_ref[0], sh_ref[0]).astype(MXU_DTYPE)
    cos = cos_ref[...]
    sin = sin_ref[...]
    lane = lax.broadcasted_iota(jnp.int32, (1, LANES), 1)
    first_half = (lane % HEAD_DIM) < (HEAD_DIM // 2)
    scale = HEAD_DIM ** -0.5

    def proj(seg, jp):
        c0 = IN_OFFS[seg] + jp * MXU_WIDTH
        t = _nn(h, w_ref[:, c0:c0 + MXU_WIDTH])
        return t[:, :LANES], t[:, LANES:]

    def rope(v):
        rot = jnp.where(first_half, pltpu.roll(v, LANES - 32, 1), pltpu.roll(v, 32, 1))
        return v * cos + rot * sin

    def emit(ref, seg, fn=lambda v: v):
        for jp in range(IN_SPLITS[seg] // MXU_WIDTH):
            for k, part in enumerate(proj(seg, jp)):
                j = 2 * jp + k
                ref[0, :, j * LANES:(j + 1) * LANES] = fn(part).astype(ref.dtype)

    emit(aq_ref, 0, lambda v: rope(v) * scale)
    emit(bq_ref, 3, lambda v: v * scale)
    emit(bk_ref, 4)
    emit(bv_ref, 5)
    emit(cq_ref, 6, lambda v: rope(v) * (scale * LOG2_E))
    emit(ck_ref, 7, rope)
    for jp in range(IN_SPLITS[8] // MXU_WIDTH):
        for k, cv in enumerate(proj(8, jp)):
            j = 2 * jp + k
            cv_ref[0, :, j * LANES:(j + 1) * LANES] = cv.astype(cv_ref.dtype)
            cvt_ref[0, j * LANES:(j + 1) * LANES, :] = cv.T.astype(cvt_ref.dtype)
    ak, av = proj(1, 0)
    ak = rope(ak)
    ak_ref[0] = ak.astype(ak_ref.dtype)
    aks_ref[0] = pltpu.roll(ak, HEAD_DIM, 1).astype(aks_ref.dtype)
    av_ref[0] = av.astype(av_ref.dtype)
    avs_ref[0] = pltpu.roll(av, HEAD_DIM, 1).astype(avs_ref.dtype)
    emit(ga_ref, 9, jax.nn.sigmoid)
    emit(gb_ref, 10, jax.nn.sigmoid)
    emit(gc_ref, 11, jax.nn.sigmoid)


def _in_proj(x, g1, mod3, mod_row, w_in, cos_t, sin_t):
    bsz, n, d = x.shape
    tm = min(TM_PROJ, n)
    tok = lambda w: pl.BlockSpec((1, tm, w), lambda b, i: (b, i, 0))
    widths = (512, 128, 128, 128, 128, 512, 512, 512, 512, 512, 512, 1024, 1024, 1024)
    return pl.pallas_call(
        _in_proj_kernel,
        out_shape=[jax.ShapeDtypeStruct((bsz, n, w), MXU_DTYPE) for w in widths]
        + [jax.ShapeDtypeStruct((bsz, 512, n), MXU_DTYPE)],
        grid=(bsz, n // tm),
        in_specs=[tok(d),
                  pl.BlockSpec((1, d), lambda b, i: (0, 0)),
                  pl.BlockSpec((1, 1, d), lambda b, i: (mod_row(b), 0, 1)),
                  pl.BlockSpec((1, 1, d), lambda b, i: (mod_row(b), 0, 0)),
                  _resident(w_in.shape),
                  pl.BlockSpec((tm, LANES), lambda b, i: (i, 0)),
                  pl.BlockSpec((tm, LANES), lambda b, i: (i, 0))],
        out_specs=[tok(w) for w in widths] + [pl.BlockSpec((1, 512, tm), lambda b, i: (b, 0, i))],
        compiler_params=_params(("parallel", "parallel")),
        name="in_proj",
    )(x, g1, mod3, mod3, w_in, cos_t, sin_t)


def _lane_lo():
    return lax.broadcasted_iota(jnp.int32, (1, LANES), 1) < HEAD_DIM


def _softmax_pv(s_list, v_list, extra_logit=None, exp=jnp.exp):
    m = functools.reduce(jnp.maximum, [jnp.max(s, axis=-1, keepdims=True) for s in s_list])
    if extra_logit is not None:
        m = jnp.maximum(m, extra_logit)
    l = 0.0 if extra_logit is None else exp(extra_logit - m)
    o = 0.0
    for s, v in zip(s_list, v_list):
        p = exp(s - m)
        l = l + jnp.sum(p, axis=-1, keepdims=True)
        o = o + _nn(p.astype(v.dtype), v)
    return o / l


def _attn_a_kernel(sink_ref, q_ref, kp_ref, kc_ref, kn_ref, ksp_ref, ksc_ref, ksn_ref,
                   vp_ref, vc_ref, vn_ref, vsp_ref, vsc_ref, vsn_ref,
                   xk_ref, xks_ref, xv_ref, xvs_ref, o_ref, *, n_tokens):
    i = pl.program_id(1)
    lo = _lane_lo()
    cat = lambda refs: jnp.concatenate([r[0] for r in refs], axis=0)
    k_loc, ks_loc = cat((kp_ref, kc_ref, kn_ref)), cat((ksp_ref, ksc_ref, ksn_ref))
    v_loc, vs_loc = cat((vp_ref, vc_ref, vn_ref)), cat((vsp_ref, vsc_ref, vsn_ref))
    zero = jnp.zeros((), k_loc.dtype)
    k_of = {(0, 0): (jnp.where(lo, k_loc, zero), jnp.where(lo, xk_ref[0], zero)),
            (0, 1): (jnp.where(lo, zero, ks_loc), jnp.where(lo, zero, xks_ref[0])),
            (1, 0): (jnp.where(lo, ks_loc, zero), jnp.where(lo, xks_ref[0], zero)),
            (1, 1): (jnp.where(lo, zero, k_loc), jnp.where(lo, zero, xk_ref[0]))}
    v_of = {(0, 0): (v_loc, xv_ref[0]), (0, 1): (vs_loc, xvs_ref[0]),
            (1, 0): (vs_loc, xvs_ref[0]), (1, 1): (v_loc, xv_ref[0])}
    chains = []
    for qb in range(A_QBLOCKS):
        blk = i * A_QBLOCKS + qb
        qpos = blk * A_BLOCK + lax.broadcasted_iota(jnp.int32, (A_BLOCK, 1), 0)
        kpos = (blk - 1) * A_BLOCK + lax.broadcasted_iota(jnp.int32, (1, 3 * A_BLOCK), 1)
        dist = qpos - kpos
        valid = (jnp.maximum(dist, -dist) <= A_WINDOW) & (kpos >= 0) & (kpos < n_tokens)
        qrows = slice(qb * A_BLOCK, (qb + 1) * A_BLOCK)
        krows = slice(qb * A_BLOCK, (qb + 3) * A_BLOCK)
        for hp in range(A_HEADS // 2):
            qp = q_ref[0, qrows, hp * LANES:(hp + 1) * LANES]
            for half in range(2):
                h = 2 * hp + half
                g = h // (A_HEADS // A_KV_HEADS)
                (kl, kx), (vl, vx) = k_of[(g, half)], v_of[(g, half)]
                s_loc = jnp.where(valid, _nt(qp, kl[krows]), NEG_INF)
                chains.append(([s_loc, _nt(qp, kx)], [vl[krows], vx], sink_ref[h]))
    outs = [_softmax_pv(s, v, extra_logit=sink) for s, v, sink in chains]
    for qb in range(A_QBLOCKS):
        qrows = slice(qb * A_BLOCK, (qb + 1) * A_BLOCK)
        for hp in range(A_HEADS // 2):
            even, odd = outs[(qb * (A_HEADS // 2) + hp) * 2:(qb * (A_HEADS // 2) + hp) * 2 + 2]
            o_ref[0, qrows, hp * LANES:(hp + 1) * LANES] = jnp.where(lo, even, odd).astype(o_ref.dtype)


def _attn_a(sink, aq, ak, aks, av, avs, xak, xaks, xav, xavs):
    bsz, n, _ = aq.shape
    nb = n // A_BLOCK
    tq = A_QBLOCKS * A_BLOCK
    ctx = xak.shape[1]
    prev = pl.BlockSpec((1, A_BLOCK, LANES),
                        lambda b, i: (b, jnp.maximum(i * A_QBLOCKS - 1, 0), 0))
    cur = pl.BlockSpec((1, tq, LANES), lambda b, i: (b, i, 0))
    nxt = pl.BlockSpec((1, A_BLOCK, LANES),
                       lambda b, i: (b, jnp.minimum((i + 1) * A_QBLOCKS, nb - 1), 0))
    cx = pl.BlockSpec((1, ctx, LANES), lambda b, i: (b, 0, 0))
    return pl.pallas_call(
        functools.partial(_attn_a_kernel, n_tokens=n),
        out_shape=jax.ShapeDtypeStruct(aq.shape, MXU_DTYPE),
        grid=(bsz, n // tq),
        in_specs=[pl.BlockSpec(memory_space=pltpu.SMEM),
                  pl.BlockSpec((1, tq, 512), lambda b, i: (b, i, 0)),
                  prev, cur, nxt, prev, cur, nxt, prev, cur, nxt, prev, cur, nxt, cx, cx, cx, cx],
        out_specs=pl.BlockSpec((1, tq, 512), lambda b, i: (b, i, 0)),
        compiler_params=_params(("parallel", "parallel")),
        name="attn_window",
    )(sink, aq, ak, ak, ak, aks, aks, aks, av, av, av, avs, avs, avs, xak, xaks, xav, xavs)


def _attn_b_kernel(rpb_ref, q_ref, k0_ref, k1_ref, k2_ref, k3_ref, v0_ref, v1_ref, v2_ref, v3_ref,
                   xk_ref, xv_ref, o_ref, pt_ref, *, n_rows):
    pair, i = pl.program_id(0), pl.program_id(2)
    n_a = 2 * NA_ROWS - 1
    n_b = 2 * NA_COLS - 1

    @pl.when((pl.program_id(1) == 0) & (i == 0))
    def _build_bias_tables():
        qc = lax.broadcasted_iota(jnp.int32, (GRID_W, LANES), 0)
        ln = lax.broadcasted_iota(jnp.int32, (GRID_W, LANES), 1)
        kc = ln % GRID_W
        hi = ln >= GRID_W
        cstart = jnp.clip(qc - NA_COLS // 2, 0, GRID_W - NA_COLS)
        col_ok = (kc >= cstart) & (kc < cstart + NA_COLS)
        d = kc - qc + (NA_COLS - 1)
        for hh in range(2):
            h = pair * 2 + hh

            def body(ai, carry):
                a = ai - 8
                a_ok, a1_ok = (a >= 0) & (a < n_a), (a + 1 >= 0) & (a + 1 < n_a)
                ra, ra1 = h * n_a + jnp.clip(a, 0, n_a - 1), h * n_a + jnp.clip(a + 1, 0, n_a - 1)
                t = jnp.full((GRID_W, LANES), NEG_INF, F32)
                for b in range(n_b):
                    va = jnp.where(a_ok, rpb_ref[ra, b], NEG_INF)
                    va1 = jnp.where(a1_ok, rpb_ref[ra1, b], NEG_INF)
                    t = jnp.where(d == b, jnp.where(hi, va1, va), t)
                pt_ref[hh, ai] = jnp.where(col_ok, t, NEG_INF)
                return carry

            lax.fori_loop(0, N_PT, body, 0)

    lo = _lane_lo()
    r0 = i * B_ROWS
    kb0 = jnp.clip(2 * i - 1, 0, n_rows // B_KROWS - 4)
    krow_lane = lax.broadcasted_iota(jnp.int32, (1, B_KROWS * GRID_W), 1) // GRID_W
    k_refs, v_refs = (k0_ref, k1_ref, k2_ref, k3_ref), (v0_ref, v1_ref, v2_ref, v3_ref)
    qp = q_ref[0]
    zero = jnp.zeros((), qp.dtype)
    raw = []
    for hh in range(2):
        keep = lambda t: jnp.where(lo, t, zero) if hh == 0 else jnp.where(lo, zero, t)
        raw.append([_nt(qp, keep(k_refs[j][0, 0])) for j in range(4)]
                   + [_nt(qp, keep(xk_ref[0]))])
    outs = []
    for hh in range(2):
        s_tiles = []
        for j in range(4):
            s = raw[hh][j]
            kr0 = (kb0 + j) * B_KROWS
            rows = []
            for qr in range(B_ROWS):
                r = r0 + qr
                rs = jnp.clip(r - NA_ROWS // 2, 0, n_rows - NA_ROWS)
                a0 = kr0 - r + (NA_ROWS - 1)
                bias = jnp.concatenate([pt_ref[hh, a0 + 8], pt_ref[hh, a0 + 10]], axis=1)
                row_ok = (kr0 + krow_lane >= rs) & (kr0 + krow_lane < rs + NA_ROWS)
                rows.append(jnp.where(row_ok, s[qr * GRID_W:(qr + 1) * GRID_W] + bias, NEG_INF))
            s_tiles.append(jnp.concatenate(rows, axis=0))
        s_tiles.append(raw[hh][4])
        outs.append(_softmax_pv(s_tiles, [v_refs[j][0, 0] for j in range(4)] + [xv_ref[0]]))
    o_ref[0] = jnp.where(lo, outs[0], outs[1]).astype(o_ref.dtype)


def _attn_b(rpb, bq, bk, bv, xbk, xbv):
    bsz, n, _ = bq.shape
    n_rows = n // GRID_W
    tq = B_ROWS * GRID_W
    tkb = B_KROWS * GRID_W
    ctx = xbk.shape[1]
    bk4 = bk.reshape(bsz, n // tkb, tkb, 512)
    bv4 = bv.reshape(bsz, n // tkb, tkb, 512)
    nkb = n // tkb

    def kspec(j):
        return pl.BlockSpec((1, 1, tkb, LANES),
                            lambda p, b, i: (b, jnp.clip(2 * i - 1, 0, nkb - 4) + j, 0, p))

    cx = pl.BlockSpec((1, ctx, LANES), lambda p, b, i: (b, 0, p))
    return pl.pallas_call(
        functools.partial(_attn_b_kernel, n_rows=n_rows),
        out_shape=jax.ShapeDtypeStruct(bq.shape, MXU_DTYPE),
        grid=(B_HEADS // 2, bsz, n // tq),
        in_specs=[pl.BlockSpec(memory_space=pltpu.SMEM),
                  pl.BlockSpec((1, tq, LANES), lambda p, b, i: (b, i, p)),
                  kspec(0), kspec(1), kspec(2), kspec(3), kspec(0), kspec(1), kspec(2), kspec(3),
                  cx, cx],
        out_specs=pl.BlockSpec((1, tq, LANES), lambda p, b, i: (b, i, p)),
        scratch_shapes=[pltpu.VMEM((2, N_PT, GRID_W, LANES), F32)],
        compiler_params=_params(("arbitrary", "arbitrary", "arbitrary")),
        name="attn_neighbourhood",
    )(rpb.reshape(B_HEADS * (2 * NA_ROWS - 1), 2 * NA_COLS - 1), bq,
      bk4, bk4, bk4, bk4, bv4, bv4, bv4, bv4, xbk, xbv)


def _diff_lambda(lam_ref, lam_init):
    lam = lam_ref[...]
    a = jnp.sum(lam[0:1] * lam[1:2], axis=-1, keepdims=True)
    b = jnp.sum(lam[2:3] * lam[3:4], axis=-1, keepdims=True)
    return jnp.exp(a) - jnp.exp(b) + lam_init


def _head_norm(o, g, lam_init):
    y = o * lax.rsqrt(jnp.mean(o * o, axis=-1, keepdims=True) + NORM_EPS)
    return (y * g) * (1.0 - lam_init)


def _attn_c_kernel(lam_ref, g_ref, q_ref, qn_ref, k_ref, vt_ref, xk_ref, xvt_ref, o_ref,
                   k1_ref, k2_ref, vall_ref, s_ref, p_ref, stat_ref, acc_ref,
                   *, lam_init, n_tokens, tk, n_chunks):
    lo = _lane_lo()

    @pl.when(pl.program_id(2) == 0)
    def _split_keys():
        zero = jnp.zeros((), k1_ref.dtype)
        k1_ref[0:n_tokens] = jnp.where(lo, k_ref[0], zero)
        k2_ref[0:n_tokens] = jnp.where(lo, zero, k_ref[0])
        k1_ref[n_tokens:] = jnp.where(lo, xk_ref[0], zero)
        k2_ref[n_tokens:] = jnp.where(lo, zero, xk_ref[0])

        vall_ref[:, 0:n_tokens] = vt_ref[0]
        vall_ref[:, n_tokens:] = xvt_ref[0]

    tq = q_ref.shape[1]
    key_refs = (k1_ref, k2_ref)
    n = n_chunks

    def reset(bp):
        stat_ref[bp, :, 0:1, :] = jnp.full((2, 1, tq), -jnp.inf, F32)
        stat_ref[bp, :, 1:2, :] = jnp.zeros((2, 1, tq), F32)
        acc_ref[bp] = jnp.zeros(acc_ref.shape[1:], F32)

    def scores(q, bp, t, slot):
        off = pl.multiple_of(t * tk, LANES)
        for c in range(2):
            s = _nt(key_refs[c][pl.ds(off, tk), :], q)
            s_ref[slot, c] = s
            stat_ref[bp, c, 4 + slot:5 + slot, :] = jnp.max(s, axis=0, keepdims=True)

    def softmax(bp, slot):
        for c in range(2):
            s = s_ref[slot, c]
            m_old = stat_ref[bp, c, 0:1, :]
            m_new = jnp.maximum(m_old, stat_ref[bp, c, 4 + slot:5 + slot, :])
            alpha = jnp.exp2(m_old - m_new)
            p = jnp.exp2(s - m_new)
            p_ref[slot, c] = p.astype(p_ref.dtype)
            stat_ref[bp, c, 0:1, :] = m_new
            stat_ref[bp, c, 1:2, :] = (alpha * stat_ref[bp, c, 1:2, :]
                                       + jnp.sum(p, axis=0, keepdims=True))
            stat_ref[bp, c, 2 + slot:3 + slot, :] = alpha

    def values(bp, t, slot):
        off = pl.multiple_of(t * tk, LANES)
        vt = vall_ref[:, pl.ds(off, tk)]
        for c in range(2):
            acc_ref[bp, c] = (stat_ref[bp, c, 2 + slot:3 + slot, :] * acc_ref[bp, c]
                              + _nn(vt, p_ref[slot, c]))

    def finish(bp):
        inv_l = 1.0 / stat_ref[bp, :, 1:2, :]
        ot = (acc_ref[bp, 0] * inv_l[0]
              - _diff_lambda(lam_ref, lam_init) * (acc_ref[bp, 1] * inv_l[1]))
        o_ref[0] = _head_norm(ot.T, g_ref[0], lam_init).astype(o_ref.dtype)

    @pl.when(pl.program_id(2) == 0)
    def _fill():
        q = q_ref[0]
        reset(0)
        scores(q, 0, 0, 0)
        scores(q, 0, 1, 1)
        softmax(0, 0)

    def run(bp):
        q, q_next = q_ref[0], qn_ref[0]
        base, nbase = (bp * n) % 2, ((bp + 1) * n) % 2
        reset(1 - bp)

        def iteration(t, par):
            scores(q, bp, t, (base + par) % 2)
            values(bp, t - 2, (base + par) % 2)
            softmax(bp, (base + par + 1) % 2)

        def pair(k, carry):
            iteration(2 + 2 * k, 0)
            iteration(3 + 2 * k, 1)
            return carry

        lax.fori_loop(0, (n - 2) // 2, pair, 0)
        for t in range(2 + 2 * ((n - 2) // 2), n):
            iteration(t, t % 2)
        scores(q_next, 1 - bp, 0, nbase)
        values(bp, n - 2, (base + n - 2) % 2)
        softmax(bp, (base + n - 1) % 2)
        scores(q_next, 1 - bp, 1, (nbase + 1) % 2)
        values(bp, n - 1, (base + n - 1) % 2)
        softmax(1 - bp, nbase)
        finish(bp)

    @pl.when(pl.program_id(2) % 2 == 0)
    def _even():
        run(0)

    @pl.when(pl.program_id(2) % 2 == 1)
    def _odd():
        run(1)


def _key_chunk(n_keys):
    return max(c for c in range(LANES, TK_C + 1, LANES) if n_keys % c == 0)


def _attn_c(c_lambda, subln3, cq, ck, cvt, xck, xcvt, lam_init):
    bsz, n, _ = cq.shape
    ctx = xck.shape[1]
    tq = min(TQ_C, n)
    tk = _key_chunk(n + ctx)
    return pl.pallas_call(
        functools.partial(_attn_c_kernel, lam_init=lam_init, n_tokens=n, tk=tk,
                          n_chunks=(n + ctx) // tk),
        out_shape=jax.ShapeDtypeStruct(cq.shape, MXU_DTYPE),
        grid=(bsz, C_HEADS, n // tq),
        in_specs=[pl.BlockSpec((4, HEAD_DIM), lambda b, h, i: (0, 0)),
                  pl.BlockSpec((1, 1, LANES), lambda b, h, i: (h, 0, 0)),
                  pl.BlockSpec((1, tq, LANES), lambda b, h, i: (b, i, h)),
                  pl.BlockSpec((1, tq, LANES),
                               lambda b, h, i: (b, jnp.minimum(i + 1, n // tq - 1), h)),
                  pl.BlockSpec((1, n, LANES), lambda b, h, i: (b, 0, h)),
                  pl.BlockSpec((1, LANES, n), lambda b, h, i: (b, h, 0)),
                  pl.BlockSpec((1, ctx, LANES), lambda b, h, i: (b, 0, h)),
                  pl.BlockSpec((1, LANES, ctx), lambda b, h, i: (b, h, 0))],
        out_specs=pl.BlockSpec((1, tq, LANES), lambda b, h, i: (b, i, h)),
        scratch_shapes=[pltpu.VMEM((n + ctx, LANES), MXU_DTYPE),
                        pltpu.VMEM((n + ctx, LANES), MXU_DTYPE),
                        pltpu.VMEM((LANES, n + ctx), MXU_DTYPE),
                        pltpu.VMEM((2, 2, tk, tq), F32),
                        pltpu.VMEM((2, 2, tk, tq), MXU_DTYPE),
                        pltpu.VMEM((2, 2, 8, tq), F32),
                        pltpu.VMEM((2, 2, LANES, tq), F32)],
        compiler_params=_params(("arbitrary", "arbitrary", "arbitrary")),
        name="attn_differential",
    )(c_lambda, subln3, cq, cq, ck, cvt, xck, xcvt)


def _ctx_attn_kernel(sink_ref, lam_ref, g_ref, aq_ref, ak_ref, aks_ref, av_ref, avs_ref,
                     bq_ref, bk_ref, bv_ref, cq_ref, ck_ref, cv_ref, ya_ref, yb_ref, yc_ref,
                     *, lam_init):
    lo = _lane_lo()
    zero = jnp.zeros((), ak_ref.dtype)
    sel = lambda t, half: jnp.where(lo, t, zero) if half == 0 else jnp.where(lo, zero, t)
    k_of = {(0, 0): sel(ak_ref[0], 0), (0, 1): sel(aks_ref[0], 1),
            (1, 0): sel(aks_ref[0], 0), (1, 1): sel(ak_ref[0], 1)}
    v_of = {(0, 0): av_ref[0], (0, 1): avs_ref[0], (1, 0): avs_ref[0], (1, 1): av_ref[0]}
    for hp in range(A_HEADS // 2):
        cols = slice(hp * LANES, (hp + 1) * LANES)
        qa, qb = aq_ref[0, :, cols], bq_ref[0, :, cols]
        kb, vb = bk_ref[0, :, cols], bv_ref[0, :, cols]
        oa, ob = [], []
        for half in range(2):
            h = 2 * hp + half
            g = h // (A_HEADS // A_KV_HEADS)
            oa.append(_softmax_pv([_nt(qa, k_of[(g, half)])], [v_of[(g, half)]],
                                  extra_logit=sink_ref[h]))
            ob.append(_softmax_pv([_nt(qb, sel(kb, half))], [vb]))
        ya_ref[0, :, cols] = jnp.where(lo, oa[0], oa[1]).astype(ya_ref.dtype)
        yb_ref[0, :, cols] = jnp.where(lo, ob[0], ob[1]).astype(yb_ref.dtype)
    lam = _diff_lambda(lam_ref, lam_init)
    for h in range(C_HEADS):
        cols = slice(h * LANES, (h + 1) * LANES)
        q, k, v = cq_ref[0, :, cols], ck_ref[0, :, cols], cv_ref[0, :, cols]
        o = (_softmax_pv([_nt(q, sel(k, 0))], [v], exp=jnp.exp2)
             - lam * _softmax_pv([_nt(q, sel(k, 1))], [v], exp=jnp.exp2))
        yc_ref[0, :, cols] = _head_norm(o, g_ref[h], lam_init).astype(yc_ref.dtype)


def _ctx_attn(sink, c_lambda, subln3, con, lam_init):
    aq, ak, aks, av, avs, bq, bk, bv, cq, ck, cv = con[:11]
    bsz, ctx, _ = aq.shape
    wide = pl.BlockSpec((1, ctx, 512), lambda b: (b, 0, 0))
    nar = pl.BlockSpec((1, ctx, LANES), lambda b: (b, 0, 0))
    return pl.pallas_call(
        functools.partial(_ctx_attn_kernel, lam_init=lam_init),
        out_shape=[jax.ShapeDtypeStruct(aq.shape, MXU_DTYPE)] * 3,
        grid=(bsz,),
        in_specs=[pl.BlockSpec(memory_space=pltpu.SMEM),
                  pl.BlockSpec((4, HEAD_DIM), lambda b: (0, 0)),
                  pl.BlockSpec((C_HEADS, 1, LANES), lambda b: (0, 0, 0)),
                  wide, nar, nar, nar, nar, wide, wide, wide, wide, wide, wide],
        out_specs=[wide] * 3,
        compiler_params=_params(("parallel",)),
        name="attn_context",
    )(sink, c_lambda, subln3, aq, ak, aks, av, avs, bq, bk, bv, cq, ck, cv)


def _merge_kernel(x_ref, gt_ref, ya_ref, yb_ref, yc_ref, ga_ref, gb_ref, gc_ref,
                  wa_ref, wb_ref, wc_ref, wo_ref, o_ref):
    m = (ga_ref[0].astype(F32) * _nn(ya_ref[0], wa_ref[...])
         + gb_ref[0].astype(F32) * _nn(yb_ref[0], wb_ref[...])
         + gc_ref[0].astype(F32) * _nn(yc_ref[0], wc_ref[...]))
    o_ref[0] = x_ref[0] + gt_ref[0] * _nn(m.astype(MXU_DTYPE), wo_ref[...])


def _merge(x, mod3, mod_row, ya, yb, yc, ga, gb, gc, wa, wb, wc, wo):
    bsz, n, d = x.shape
    tm = min(TM_PROJ, n)
    tok = lambda w: pl.BlockSpec((1, tm, w), lambda b, i: (b, i, 0))
    return pl.pallas_call(
        _merge_kernel,
        out_shape=jax.ShapeDtypeStruct(x.shape, F32),
        grid=(bsz, n // tm),
        in_specs=[tok(d), pl.BlockSpec((1, 1, d), lambda b, i: (mod_row(b), 0, 2)),
                  tok(512), tok(512), tok(512), tok(d), tok(d), tok(d),
                  _resident(wa.shape), _resident(wb.shape), _resident(wc.shape), _resident(wo.shape)],
        out_specs=tok(d),
        compiler_params=_params(("parallel", "parallel")),
        name="merge",
    )(x, mod3, ya, yb, yc, ga, gb, gc, wa, wb, wc, wo)


def _top_values(ref, row0, n_rows, count, out_ref, out_row0, floor):
    for k in range(count):
        cur = ref[row0:row0 + n_rows]
        m = jnp.max(cur, axis=0, keepdims=True)
        out_ref[out_row0 + k:out_row0 + k + 1] = m
        if k + 1 < count:
            ref[row0:row0 + n_rows] = jnp.where(cur == m, floor, cur)


def _batcher_network(n):
    pairs = []
    p = 1
    while p < n:
        k = p
        while k >= 1:
            for j in range(k % p, n - k, 2 * k):
                for i in range(min(k, n - j - k)):
                    if (i + j) // (2 * p) == (i + j + k) // (2 * p):
                        pairs.append((i + j, i + j + k))
            k //= 2
        p *= 2
    return tuple(pairs)


N_TOP = PEER_TOPK + 1
SUBLANES = 8
_SORT_NET = _batcher_network(PEER_KEYS // SUBLANES)
_CAND_COUNTS = tuple(N_TOP // (a + 1) for a in range(N_TOP))
_CAND_OFFS = tuple(int(v) for v in np.cumsum((0,) + _CAND_COUNTS))
N_CAND = -(-_CAND_OFFS[-1] // SUBLANES) * SUBLANES


def _top_sorted(s_ref, lanes, out_ref, slot):
    n_grp = PEER_KEYS // SUBLANES
    lists = [s_ref[r * SUBLANES:(r + 1) * SUBLANES, lanes] for r in range(n_grp)]
    for a, b in _SORT_NET:
        lists[a], lists[b] = jnp.maximum(lists[a], lists[b]), jnp.minimum(lists[a], lists[b])
    for t in range(N_TOP):
        head = lists[0]
        m = jnp.max(head, axis=0, keepdims=True)
        out_ref[slot, t:t + 1, lanes] = m
        remaining = N_TOP - 1 - t
        hit = head == m
        for k in range(remaining):
            below = lists[k + 1] if k + 1 < n_grp else -jnp.inf
            lists[k] = jnp.where(hit, below, lists[k])


def _route_kernel(x_ref, g_ref, sc_ref, sh_ref, wq_ref, keys_ref,
                  h_ref, e1_ref, e2_ref, thr_ref, s_ref, top_ref, cand_ref):
    tt = x_ref.shape[1]
    h2 = _rms_mod(x_ref[0], g_ref[...], sc_ref[0], sh_ref[0]).astype(MXU_DTYPE)
    h_ref[0] = h2
    q = _nn(h2, wq_ref[...]).astype(MXU_DTYPE)
    top_ref[:, N_TOP:, :] = jnp.full((3, top_ref.shape[1] - N_TOP, tt), -jnp.inf, F32)
    cand_ref[_CAND_OFFS[-1]:, :] = jnp.full((N_CAND - _CAND_OFFS[-1], tt), -1.0, F32)
    for h in range(PEER_HEADS):
        tops = []
        for p, e_ref in ((0, e1_ref), (1, e2_ref)):
            hp = 2 * h + p
            s_ref[...] = _nt(keys_ref[hp], q[:, hp * LANES:(hp + 1) * LANES])
            for tc in range(tt // LANES):
                _top_sorted(s_ref, slice(tc * LANES, (tc + 1) * LANES), top_ref, p)
            m = top_ref[p, 0:1]
            e_ref[0, h] = jnp.exp(s_ref[...] - m)
            tops.append(jnp.exp(top_ref[p] - m))
        e1_top, e2_top = tops
        for a in range(N_TOP):
            cand_ref[_CAND_OFFS[a]:_CAND_OFFS[a + 1]] = e1_top[a:a + 1] * e2_top[0:_CAND_COUNTS[a]]
        _top_values(cand_ref, 0, N_CAND, N_TOP, top_ref.at[2], 0, -1.0)
        best = top_ref[2]
        inv_z = 1.0 / jnp.sum(best[0:PEER_TOPK], axis=0, keepdims=True)
        e1_ref[0, h] = e1_ref[0, h] * inv_z
        thr_ref[0, h:h + 1] = 0.5 * (best[PEER_TOPK - 1:PEER_TOPK] + best[PEER_TOPK:N_TOP]) * inv_z


def _route(x, g2, mod3, mod_row, wq, keys):
    bsz, n, d = x.shape
    tt = min(TT_PEER, n)
    gate_shape = jax.ShapeDtypeStruct((bsz, PEER_HEADS, PEER_KEYS, n), F32)
    gate_spec = pl.BlockSpec((1, PEER_HEADS, PEER_KEYS, tt), lambda b, i: (b, 0, 0, i))
    return pl.pallas_call(
        _route_kernel,
        out_shape=[jax.ShapeDtypeStruct((bsz, n, d), MXU_DTYPE), gate_shape, gate_shape,
                   jax.ShapeDtypeStruct((bsz, PEER_HEADS, n), F32)],
        grid=(bsz, n // tt),
        in_specs=[pl.BlockSpec((1, tt, d), lambda b, i: (b, i, 0)),
                  pl.BlockSpec((1, d), lambda b, i: (0, 0)),
                  pl.BlockSpec((1, 1, d), lambda b, i: (mod_row(b), 0, 4)),
                  pl.BlockSpec((1, 1, d), lambda b, i: (mod_row(b), 0, 3)),
                  _resident(wq.shape), _resident(keys.shape)],
        out_specs=[pl.BlockSpec((1, tt, d), lambda b, i: (b, i, 0)), gate_spec, gate_spec,
                   pl.BlockSpec((1, PEER_HEADS, tt), lambda b, i: (b, 0, i))],
        scratch_shapes=[pltpu.VMEM((PEER_KEYS, tt), F32),
                        pltpu.VMEM((3, 3 * SUBLANES, tt), F32),
                        pltpu.VMEM((N_CAND, tt), F32)],
        compiler_params=_params(("parallel", "parallel")),
        name="peer_route",
    )(x, g2, mod3, mod3, wq, keys)


def _expert_kernel(x_ref, gt_ref, gf_ref, h_ref, u_ref, v_ref, e1_ref, e2_ref, thr_ref, o_ref,
                   acc_ref, a_ref, w_ref, *, final_norm):
    eb = pl.program_id(2)
    tt = h_ref.shape[1]

    @pl.when(eb == 0)
    def _zero():
        acc_ref[...] = jnp.zeros_like(acc_ref)

    a_ref[...] = _nt(u_ref[...], h_ref[0])
    sub = 64
    n_i = TE_PEER // PEER_KEYS
    i0 = pl.multiple_of(eb * n_i, n_i)
    for il in range(n_i):
        for tc in range(tt // LANES):
            lanes = slice(tc * LANES, (tc + 1) * LANES)
            e1_rows = [e1_ref[0, h, pl.ds(i0, n_i), lanes][il:il + 1] for h in range(PEER_HEADS)]
            thr_rows = [thr_ref[0, h:h + 1, lanes] for h in range(PEER_HEADS)]
            for js in range(PEER_KEYS // sub):
                rows = slice(js * sub, (js + 1) * sub)
                gate = None
                for h in range(PEER_HEADS):
                    prod = e2_ref[0, h, rows, lanes] * e1_rows[h]
                    sel = jnp.where(prod >= thr_rows[h], prod, 0.0)
                    gate = sel if gate is None else gate + sel
                erows = slice(il * PEER_KEYS + js * sub, il * PEER_KEYS + (js + 1) * sub)
                a = a_ref[erows, lanes]
                act = (0.5 * a) * (1.0 + lax.erf(a * math.sqrt(0.5)))
                w_ref[erows, lanes] = gate * act
    acc_ref[...] += _nn(w_ref[...].T.astype(MXU_DTYPE), v_ref[...])

    @pl.when(eb == pl.num_programs(2) - 1)
    def _finish():
        y = x_ref[0] + gt_ref[0] * acc_ref[...]
        if final_norm:
            y = (y * lax.rsqrt(jnp.mean(y * y, axis=-1, keepdims=True) + NORM_EPS)) * gf_ref[...]
        o_ref[0] = y


def _experts(x, mod3, mod_row, g_final, h2, u, v, e1, e2, thr, final_norm):
    bsz, n, d = x.shape
    tt = min(TT_PEER, n)
    n_exp = u.shape[0]
    gate_spec = pl.BlockSpec((1, PEER_HEADS, PEER_KEYS, tt), lambda b, i, e: (b, 0, 0, i))
    return pl.pallas_call(
        functools.partial(_expert_kernel, final_norm=final_norm),
        out_shape=jax.ShapeDtypeStruct(x.shape, F32),
        grid=(bsz, n // tt, n_exp // TE_PEER),
        in_specs=[pl.BlockSpec((1, tt, d), lambda b, i, e: (b, i, 0)),
                  pl.BlockSpec((1, 1, d), lambda b, i, e: (mod_row(b), 0, 5)),
                  pl.BlockSpec((1, d), lambda b, i, e: (0, 0)),
                  pl.BlockSpec((1, tt, d), lambda b, i, e: (b, i, 0)),
                  pl.BlockSpec((TE_PEER, d), lambda b, i, e: (e, 0)),
                  pl.BlockSpec((TE_PEER, d), lambda b, i, e: (e, 0)),
                  gate_spec, gate_spec,
                  pl.BlockSpec((1, PEER_HEADS, tt), lambda b, i, e: (b, 0, i))],
        out_specs=pl.BlockSpec((1, tt, d), lambda b, i, e: (b, i, 0)),
        scratch_shapes=[pltpu.VMEM((tt, d), F32),
                        pltpu.VMEM((TE_PEER, tt), F32),
                        pltpu.VMEM((TE_PEER, tt), F32)],
        compiler_params=_params(("parallel", "parallel", "arbitrary")),
        name="peer_experts",
    )(x, mod3, g_final, h2, u, v, e1, e2, thr)


def _rope_tables(n):
    t = jnp.arange(n, dtype=jnp.int32)
    row = (t // GRID_W).astype(F32)
    col = (t % GRID_W).astype(F32)
    n_freq = HEAD_DIM // 4
    inv = ROPE_THETA ** (-jnp.arange(n_freq, dtype=F32) / n_freq)
    ang = jnp.concatenate([row[:, None] * inv, col[:, None] * inv], axis=-1)
    cos, sin = jnp.cos(ang), jnp.sin(ang)
    return jnp.tile(cos, (1, 4)), jnp.tile(jnp.concatenate([-sin, sin], axis=-1), (1, 2))


def kernel(x, c, ctx, c_ctx, w_mod, b_mod, g_norm1, g_norm2, w_in, a_sink, b_rpb, c_lambda,
           c_subln, w_branch_a, w_branch_b, w_branch_c, w_out, peer_wq, peer_keys, peer_u,
           peer_v, g_final):
    bsz, n, d = x.shape
    depth = w_mod.shape[0]
    n_ctx = ctx.shape[1]
    assert d == D_MODEL and bsz + 1 <= 8
    assert n % (B_ROWS * GRID_W) == 0 and n // GRID_W >= 4 * B_KROWS
    assert n % (A_QBLOCKS * A_BLOCK) == 0 and (n + n_ctx) % LANES == 0

    cvec = jnp.zeros((8, d), F32).at[:bsz].set(c).at[bsz].set(c_ctx)
    mod = _modulation(cvec, w_mod, b_mod)
    lat_row = lambda b: b
    ctx_row = lambda b: bsz
    cos_t, sin_t = _rope_tables(n)
    ones_t, zeros_t = jnp.ones((n_ctx, LANES), F32), jnp.zeros((n_ctx, LANES), F32)
    cast = lambda w: w.astype(MXU_DTYPE)
    gfin = g_final.reshape(1, d)

    xc = ctx
    for l in range(depth):
        last = l == depth - 1
        lam_init = 0.8 - 0.6 * math.exp(-0.3 * l)
        mod3 = mod[l].reshape(8, 1, 6 * d)
        g1, g2 = g_norm1[l].reshape(1, d), g_norm2[l].reshape(1, d)
        w_in_l = cast(w_in[l])
        subln3 = c_subln[l].reshape(C_HEADS, 1, LANES)
        wa, wb, wc, wo = cast(w_branch_a[l]), cast(w_branch_b[l]), cast(w_branch_c[l]), cast(w_out[l])
        wq = cast(peer_wq[l])
        keys = cast(peer_keys[l].reshape(2 * PEER_HEADS, PEER_KEYS, LANES))
        u, v = cast(peer_u[l]), cast(peer_v[l])

        lat = _in_proj(x, g1, mod3, lat_row, w_in_l, cos_t, sin_t)
        con = _in_proj(xc, g1, mod3, ctx_row, w_in_l, ones_t, zeros_t)
        (aq, ak, aks, av, avs, bq, bk, bv, cq, ck, _, ga, gb, gc, cvt) = lat
        ya = _attn_a(a_sink[l], aq, ak, aks, av, avs, con[1], con[2], con[3], con[4])
        yb = _attn_b(b_rpb[l], bq, bk, bv, con[6], con[7])
        yc = _attn_c(c_lambda[l], subln3, cq, ck, cvt, con[9], con[14], lam_init)
        x = _merge(x, mod3, lat_row, ya, yb, yc, ga, gb, gc, wa, wb, wc, wo)
        h2, e1, e2, thr = _route(x, g2, mod3, lat_row, wq, keys)
        x = _experts(x, mod3, lat_row, gfin, h2, u, v, e1, e2, thr, final_norm=last)

        if not last:
            ya_c, yb_c, yc_c = _ctx_attn(a_sink[l], c_lambda[l], subln3, con, lam_init)
            xc = _merge(xc, mod3, ctx_row, ya_c, yb_c, yc_c, con[11], con[12], con[13],
                        wa, wb, wc, wo)
            h2c, e1c, e2c, thrc = _route(xc, g2, mod3, ctx_row, wq, keys)
            xc = _experts(xc, mod3, ctx_row, gfin, h2c, u, v, e1c, e2c, thrc, final_norm=False)
    return x
```

```python
import functools
import math

import jax
import jax.numpy as jnp
import numpy as np
from jax import lax
from jax.experimental import pallas as pl
from jax.experimental.pallas import tpu as pltpu

F32 = jnp.float32
BF16 = jnp.bfloat16
MXU_DTYPE = BF16

LANES = 128
MXU_WIDTH = 256
VMEM_LIMIT_BYTES = 56 * 1024 * 1024

HEAD_DIM = 64
GRID_W = 64
ROPE_THETA = 10000.0
NORM_EPS = 1e-6
NEG_INF = -1e30
LOG2_E = math.log2(math.e)
A_HEADS, A_KV_HEADS, A_WINDOW, A_BLOCK = 8, 2, 128, 128
B_HEADS, NA_ROWS, NA_COLS = 8, 8, 16
C_HEADS = 4
PEER_HEADS, PEER_KEYS, PEER_TOPK = 8, 128, 16
D_MODEL = 1024
IN_SPLITS = (512, 128, 128, 512, 512, 512, 512, 512, 512, 1024, 1024, 1024)
IN_OFFS = tuple(int(v) for v in np.cumsum((0,) + IN_SPLITS))

TM_PROJ = 512
A_QBLOCKS = 4
TQ_C = 512
TK_C = 768
B_ROWS = 8
B_KROWS = 4
TT_PEER = 512
TE_PEER = 1024
N_PT = 30


def _nt(a, b):
    return lax.dot_general(a, b, (((1,), (1,)), ((), ())), preferred_element_type=F32)


def _nn(a, b):
    return jnp.dot(a, b, preferred_element_type=F32)


def _params(sem):
    return pltpu.CompilerParams(dimension_semantics=sem, vmem_limit_bytes=VMEM_LIMIT_BYTES)


def _resident(shape):
    nd = len(shape)
    return pl.BlockSpec(shape, lambda *_: (0,) * nd, pipeline_mode=pl.Buffered(1))


def _mod_kernel(c_ref, w_ref, b_ref, o_ref):
    c = c_ref[...]
    s = c * jax.nn.sigmoid(c)
    w = w_ref[0]
    s_hi = s.astype(MXU_DTYPE)
    s_lo = (s - s_hi.astype(F32)).astype(MXU_DTYPE)
    w_hi = w.astype(MXU_DTYPE)
    w_lo = (w - w_hi.astype(F32)).astype(MXU_DTYPE)
    acc = _nn(s_hi, w_hi) + _nn(s_lo, w_hi) + _nn(s_hi, w_lo)
    o_ref[0] = acc + b_ref[0]


def _modulation(cvec, w_mod, b_mod):
    depth, d, n6 = w_mod.shape
    tn = 1536
    return pl.pallas_call(
        _mod_kernel,
        out_shape=jax.ShapeDtypeStruct((depth, 8, n6), F32),
        grid=(depth, n6 // tn),
        in_specs=[pl.BlockSpec((8, d), lambda l, j: (0, 0)),
                  pl.BlockSpec((1, d, tn), lambda l, j: (l, 0, j)),
                  pl.BlockSpec((1, 1, tn), lambda l, j: (l, 0, j))],
        out_specs=pl.BlockSpec((1, 8, tn), lambda l, j: (l, 0, j)),
        compiler_params=_params(("parallel", "parallel")),
        name="modulation",
    )(cvec, w_mod, b_mod.reshape(depth, 1, n6))


def _rms_mod(x, g, sc, sh):
    y = x * lax.rsqrt(jnp.mean(x * x, axis=-1, keepdims=True) + NORM_EPS)
    return (y * g) * (1.0 + sc) + sh


def _in_proj_kernel(x_ref, g_ref, sc_ref, sh_ref, w_ref, cos_ref, sin_ref,
                    aq_ref, ak_ref, aks_ref, av_ref, avs_ref, bq_ref, bk_ref, bv_ref,
                    cq_ref, ck_ref, cv_ref, ga_ref, gb_ref, gc_ref, cvt_ref):
    h = _rms_mod(x_ref[0], g_ref[...], sc_ref[0], sh_ref[0]).astype(MXU_DTYPE)
    cos = cos_ref[...]
    sin = sin_ref[...]
    lane = lax.broadcasted_iota(jnp.int32, (1, LANES), 1)
    first_half = (lane % HEAD_DIM) < (HEAD_DIM // 2)
    scale = HEAD_DIM ** -0.5

    def proj(seg, jp):
        c0 = IN_OFFS[seg] + jp * MXU_WIDTH
        t = _nn(h, w_ref[:, c0:c0 + MXU_WIDTH])
        return t[:, :LANES], t[:, LANES:]

    def rope(v):
        rot = jnp.where(first_half, pltpu.roll(v, LANES - 32, 1), pltpu.roll(v, 32, 1))
        return v * cos + rot * sin

    def emit(ref, seg, fn=lambda v: v):
        for jp in range(IN_SPLITS[seg] // MXU_WIDTH):
            for k, part in enumerate(proj(seg, jp)):
                j = 2 * jp + k
                ref[0, :, j * LANES:(j + 1) * LANES] = fn(part).astype(ref.dtype)

    emit(aq_ref, 0, lambda v: rope(v) * scale)
    emit(bq_ref, 3, lambda v: v * scale)
    emit(bk_ref, 4)
    emit(bv_ref, 5)
    emit(cq_ref, 6, lambda v: rope(v) * (scale * LOG2_E))
    emit(ck_ref, 7, rope)
    for jp in range(IN_SPLITS[8] // MXU_WIDTH):
        for k, cv in enumerate(proj(8, jp)):
            j = 2 * jp + k
            cv_ref[0, :, j * LANES:(j + 1) * LANES] = cv.astype(cv_ref.dtype)
            cvt_ref[0, j * LANES:(j + 1) * LANES, :] = cv.T.astype(cvt_ref.dtype)
    ak, av = proj(1, 0)
    ak = rope(ak)
    ak_ref[0] = ak.astype(ak_ref.dtype)
    aks_ref[0] = pltpu.roll(ak, HEAD_DIM, 1).astype(aks_ref.dtype)
    av_ref[0] = av.astype(av_ref.dtype)
    avs_ref[0] = pltpu.roll(av, HEAD_DIM, 1).astype(avs_ref.dtype)
    emit(ga_ref, 9, jax.nn.sigmoid)
    emit(gb_ref, 10, jax.nn.sigmoid)
    emit(gc_ref, 11, jax.nn.sigmoid)


def _in_proj(x, g1, mod3, mod_row, w_in, cos_t, sin_t):
    bsz, n, d = x.shape
    tm = min(TM_PROJ, n)
    tok = lambda w: pl.BlockSpec((1, tm, w), lambda b, i: (b, i, 0))
    widths = (512, 128, 128, 128, 128, 512, 512, 512, 512, 512, 512, 1024, 1024, 1024)
    return pl.pallas_call(
        _in_proj_kernel,
        out_shape=[jax.ShapeDtypeStruct((bsz, n, w), MXU_DTYPE) for w in widths]
        + [jax.ShapeDtypeStruct((bsz, 512, n), MXU_DTYPE)],
        grid=(bsz, n // tm),
        in_specs=[tok(d),
                  pl.BlockSpec((1, d), lambda b, i: (0, 0)),
                  pl.BlockSpec((1, 1, d), lambda b, i: (mod_row(b), 0, 1)),
                  pl.BlockSpec((1, 1, d), lambda b, i: (mod_row(b), 0, 0)),
                  _resident(w_in.shape),
                  pl.BlockSpec((tm, LANES), lambda b, i: (i, 0)),
                  pl.BlockSpec((tm, LANES), lambda b, i: (i, 0))],
        out_specs=[tok(w) for w in widths] + [pl.BlockSpec((1, 512, tm), lambda b, i: (b, 0, i))],
        compiler_params=_params(("parallel", "parallel")),
        name="in_proj",
    )(x, g1, mod3, mod3, w_in, cos_t, sin_t)


def _lane_lo():
    return lax.broadcasted_iota(jnp.int32, (1, LANES), 1) < HEAD_DIM


def _softmax_pv(s_list, v_list, extra_logit=None, exp=jnp.exp):
    m = functools.reduce(jnp.maximum, [jnp.max(s, axis=-1, keepdims=True) for s in s_list])
    if extra_logit is not None:
        m = jnp.maximum(m, extra_logit)
    l = 0.0 if extra_logit is None else exp(extra_logit - m)
    o = 0.0
    for s, v in zip(s_list, v_list):
        p = exp(s - m)
        l = l + jnp.sum(p, axis=-1, keepdims=True)
        o = o + _nn(p.astype(v.dtype), v)
    return o / l


def _attn_a_kernel(sink_ref, q_ref, kp_ref, kc_ref, kn_ref, ksp_ref, ksc_ref, ksn_ref,
                   vp_ref, vc_ref, vn_ref, vsp_ref, vsc_ref, vsn_ref,
                   xk_ref, xks_ref, xv_ref, xvs_ref, o_ref, *, n_tokens):
    i = pl.program_id(1)
    lo = _lane_lo()
    cat = lambda refs: jnp.concatenate([r[0] for r in refs], axis=0)
    k_loc, ks_loc = cat((kp_ref, kc_ref, kn_ref)), cat((ksp_ref, ksc_ref, ksn_ref))
    v_loc, vs_loc = cat((vp_ref, vc_ref, vn_ref)), cat((vsp_ref, vsc_ref, vsn_ref))
    zero = jnp.zeros((), k_loc.dtype)
    k_of = {(0, 0): (jnp.where(lo, k_loc, zero), jnp.where(lo, xk_ref[0], zero)),
            (0, 1): (jnp.where(lo, zero, ks_loc), jnp.where(lo, zero, xks_ref[0])),
            (1, 0): (jnp.where(lo, ks_loc, zero), jnp.where(lo, xks_ref[0], zero)),
            (1, 1): (jnp.where(lo, zero, k_loc), jnp.where(lo, zero, xk_ref[0]))}
    v_of = {(0, 0): (v_loc, xv_ref[0]), (0, 1): (vs_loc, xvs_ref[0]),
            (1, 0): (vs_loc, xvs_ref[0]), (1, 1): (v_loc, xv_ref[0])}
    chains = []
    for qb in range(A_QBLOCKS):
        blk = i * A_QBLOCKS + qb
        qpos = blk * A_BLOCK + lax.broadcasted_iota(jnp.int32, (A_BLOCK, 1), 0)
        kpos = (blk - 1) * A_BLOCK + lax.broadcasted_iota(jnp.int32, (1, 3 * A_BLOCK), 1)
        dist = qpos - kpos
        valid = (jnp.maximum(dist, -dist) <= A_WINDOW) & (kpos >= 0) & (kpos < n_tokens)
        qrows = slice(qb * A_BLOCK, (qb + 1) * A_BLOCK)
        krows = slice(qb * A_BLOCK, (qb + 3) * A_BLOCK)
        for hp in range(A_HEADS // 2):
            qp = q_ref[0, qrows, hp * LANES:(hp + 1) * LANES]
            for half in range(2):
                h = 2 * hp + half
                g = h // (A_HEADS // A_KV_HEADS)
                (kl, kx), (vl, vx) = k_of[(g, half)], v_of[(g, half)]
                s_loc = jnp.where(valid, _nt(qp, kl[krows]), NEG_INF)
                chains.append(([s_loc, _nt(qp, kx)], [vl[krows], vx], sink_ref[h]))
    outs = [_softmax_pv(s, v, extra_logit=sink) for s, v, sink in chains]
    for qb in range(A_QBLOCKS):
        qrows = slice(qb * A_BLOCK, (qb + 1) * A_BLOCK)
        for hp in range(A_HEADS // 2):
            even, odd = outs[(qb * (A_HEADS // 2) + hp) * 2:(qb * (A_HEADS // 2) + hp) * 2 + 2]
            o_ref[0, qrows, hp * LANES:(hp + 1) * LANES] = jnp.where(lo, even, odd).astype(o_ref.dtype)


def _attn_a(sink, aq, ak, aks, av, avs, xak, xaks, xav, xavs):
    bsz, n, _ = aq.shape
    nb = n // A_BLOCK
    tq = A_QBLOCKS * A_BLOCK
    ctx = xak.shape[1]
    prev = pl.BlockSpec((1, A_BLOCK, LANES),
                        lambda b, i: (b, jnp.maximum(i * A_QBLOCKS - 1, 0), 0))
    cur = pl.BlockSpec((1, tq, LANES), lambda b, i: (b, i, 0))
    nxt = pl.BlockSpec((1, A_BLOCK, LANES),
                       lambda b, i: (b, jnp.minimum((i + 1) * A_QBLOCKS, nb - 1), 0))
    cx = pl.BlockSpec((1, ctx, LANES), lambda b, i: (b, 0, 0))
    return pl.pallas_call(
        functools.partial(_attn_a_kernel, n_tokens=n),
        out_shape=jax.ShapeDtypeStruct(aq.shape, MXU_DTYPE),
        grid=(bsz, n // tq),
        in_specs=[pl.BlockSpec(memory_space=pltpu.SMEM),
                  pl.BlockSpec((1, tq, 512), lambda b, i: (b, i, 0)),
                  prev, cur, nxt, prev, cur, nxt, prev, cur, nxt, prev, cur, nxt, cx, cx, cx, cx],
        out_specs=pl.BlockSpec((1, tq, 512), lambda b, i: (b, i, 0)),
        compiler_params=_params(("parallel", "parallel")),
        name="attn_window",
    )(sink, aq, ak, ak, ak, aks, aks, aks, av, av, av, avs, avs, avs, xak, xaks, xav, xavs)


def _attn_b_kernel(rpb_ref, q_ref, k0_ref, k1_ref, k2_ref, k3_ref, v0_ref, v1_ref, v2_ref, v3_ref,
                   xk_ref, xv_ref, o_ref, pt_ref, *, n_rows):
    pair, i = pl.program_id(0), pl.program_id(2)
    n_a = 2 * NA_ROWS - 1
    n_b = 2 * NA_COLS - 1

    @pl.when((pl.program_id(1) == 0) & (i == 0))
    def _build_bias_tables():
        qc = lax.broadcasted_iota(jnp.int32, (GRID_W, LANES), 0)
        ln = lax.broadcasted_iota(jnp.int32, (GRID_W, LANES), 1)
        kc = ln % GRID_W
        hi = ln >= GRID_W
        cstart = jnp.clip(qc - NA_COLS // 2, 0, GRID_W - NA_COLS)
        col_ok = (kc >= cstart) & (kc < cstart + NA_COLS)
        d = kc - qc + (NA_COLS - 1)
        for hh in range(2):
            h = pair * 2 + hh

            def body(ai, carry):
                a = ai - 8
                a_ok, a1_ok = (a >= 0) & (a < n_a), (a + 1 >= 0) & (a + 1 < n_a)
                ra, ra1 = h * n_a + jnp.clip(a, 0, n_a - 1), h * n_a + jnp.clip(a + 1, 0, n_a - 1)
                t = jnp.full((GRID_W, LANES), NEG_INF, F32)
                for b in range(n_b):
                    va = jnp.where(a_ok, rpb_ref[ra, b], NEG_INF)
                    va1 = jnp.where(a1_ok, rpb_ref[ra1, b], NEG_INF)
                    t = jnp.where(d == b, jnp.where(hi, va1, va), t)
                pt_ref[hh, ai] = jnp.where(col_ok, t, NEG_INF)
                return carry

            lax.fori_loop(0, N_PT, body, 0)

    lo = _lane_lo()
    r0 = i * B_ROWS
    kb0 = jnp.clip(2 * i - 1, 0, n_rows // B_KROWS - 4)
    krow_lane = lax.broadcasted_iota(jnp.int32, (1, B_KROWS * GRID_W), 1) // GRID_W
    k_refs, v_refs = (k0_ref, k1_ref, k2_ref, k3_ref), (v0_ref, v1_ref, v2_ref, v3_ref)
    qp = q_ref[0]
    zero = jnp.zeros((), qp.dtype)
    raw = []
    for hh in range(2):
        keep = lambda t: jnp.where(lo, t, zero) if hh == 0 else jnp.where(lo, zero, t)
        raw.append([_nt(qp, keep(k_refs[j][0, 0])) for j in range(4)]
                   + [_nt(qp, keep(xk_ref[0]))])
    outs = []
    for hh in range(2):
        s_tiles = []
        for j in range(4):
            s = raw[hh][j]
            kr0 = (kb0 + j) * B_KROWS
            rows = []
            for qr in range(B_ROWS):
                r = r0 + qr
                rs = jnp.clip(r - NA_ROWS // 2, 0, n_rows - NA_ROWS)
                a0 = kr0 - r + (NA_ROWS - 1)
                bias = jnp.concatenate([pt_ref[hh, a0 + 8], pt_ref[hh, a0 + 10]], axis=1)
                row_ok = (kr0 + krow_lane >= rs) & (kr0 + krow_lane < rs + NA_ROWS)
                rows.append(jnp.where(row_ok, s[qr * GRID_W:(qr + 1) * GRID_W] + bias, NEG_INF))
            s_tiles.append(jnp.concatenate(rows, axis=0))
        s_tiles.append(raw[hh][4])
        outs.append(_softmax_pv(s_tiles, [v_refs[j][0, 0] for j in range(4)] + [xv_ref[0]]))
    o_ref[0] = jnp.where(lo, outs[0], outs[1]).astype(o_ref.dtype)


def _attn_b(rpb, bq, bk, bv, xbk, xbv):
    bsz, n, _ = bq.shape
    n_rows = n // GRID_W
    tq = B_ROWS * GRID_W
    tkb = B_KROWS * GRID_W
    ctx = xbk.shape[1]
    bk4 = bk.reshape(bsz, n // tkb, tkb, 512)
    bv4 = bv.reshape(bsz, n // tkb, tkb, 512)
    nkb = n // tkb

    def kspec(j):
        return pl.BlockSpec((1, 1, tkb, LANES),
                            lambda p, b, i: (b, jnp.clip(2 * i - 1, 0, nkb - 4) + j, 0, p))

    cx = pl.BlockSpec((1, ctx, LANES), lambda p, b, i: (b, 0, p))
    return pl.pallas_call(
        functools.partial(_attn_b_kernel, n_rows=n_rows),
        out_shape=jax.ShapeDtypeStruct(bq.shape, MXU_DTYPE),
        grid=(B_HEADS // 2, bsz, n // tq),
        in_specs=[pl.BlockSpec(memory_space=pltpu.SMEM),
                  pl.BlockSpec((1, tq, LANES), lambda p, b, i: (b, i, p)),
                  kspec(0), kspec(1), kspec(2), kspec(3), kspec(0), kspec(1), kspec(2), kspec(3),
                  cx, cx],
        out_specs=pl.BlockSpec((1, tq, LANES), lambda p, b, i: (b, i, p)),
        scratch_shapes=[pltpu.VMEM((2, N_PT, GRID_W, LANES), F32)],
        compiler_params=_params(("arbitrary", "arbitrary", "arbitrary")),
        name="attn_neighbourhood",
    )(rpb.reshape(B_HEADS * (2 * NA_ROWS - 1), 2 * NA_COLS - 1), bq,
      bk4, bk4, bk4, bk4, bv4, bv4, bv4, bv4, xbk, xbv)


def _diff_lambda(lam_ref, lam_init):
    lam = lam_ref[...]
    a = jnp.sum(lam[0:1] * lam[1:2], axis=-1, keepdims=True)
    b = jnp.sum(lam[2:3] * lam[3:4], axis=-1, keepdims=True)
    return jnp.exp(a) - jnp.exp(b) + lam_init


def _head_norm(o, g, lam_init):
    y = o * lax.rsqrt(jnp.mean(o * o, axis=-1, keepdims=True) + NORM_EPS)
    return (y * g) * (1.0 - lam_init)


def _attn_c_kernel(lam_ref, g_ref, q_ref, k_ref, vt_ref, xk_ref, xvt_ref, o_ref,
                   k1_ref, k2_ref, vall_ref, s_ref, p_ref, stat_ref, acc_ref,
                   *, lam_init, n_tokens, tk, n_chunks):
    lo = _lane_lo()

    @pl.when(pl.program_id(2) == 0)
    def _split_keys():
        zero = jnp.zeros((), k1_ref.dtype)
        k1_ref[0:n_tokens] = jnp.where(lo, k_ref[0], zero)
        k2_ref[0:n_tokens] = jnp.where(lo, zero, k_ref[0])
        k1_ref[n_tokens:] = jnp.where(lo, xk_ref[0], zero)
        k2_ref[n_tokens:] = jnp.where(lo, zero, xk_ref[0])

        vall_ref[:, 0:n_tokens] = vt_ref[0]
        vall_ref[:, n_tokens:] = xvt_ref[0]

    q = q_ref[0]
    tq = q.shape[0]
    key_refs = (k1_ref, k2_ref)
    n = n_chunks
    j = pl.program_id(2)
    n_q = pl.num_programs(2) - 1
    bp = j % 2

    slot_of = lambda c: 2 if c == 0 else c % 2

    def reset(par):
        for c in range(2):
            stat_ref[2 * par + c, 0:1, :] = jnp.full((1, tq), -jnp.inf, F32)
            stat_ref[2 * par + c, 1:2, :] = jnp.zeros((1, tq), F32)
            acc_ref[2 * par + c] = jnp.zeros(acc_ref.shape[1:], F32)

    def scores(par, t, slot):
        off = pl.multiple_of(t * tk, LANES)
        for c in range(2):
            s = _nt(key_refs[c][pl.ds(off, tk), :], q)
            s_ref[slot, c] = s
            stat_ref[2 * par + c, 5 + slot:6 + slot, :] = jnp.max(s, axis=0, keepdims=True)

    def softmax(par, slot):
        for c in range(2):
            st = stat_ref.at[2 * par + c]
            s = s_ref[slot, c]
            m_old = st[0:1, :]
            m_new = jnp.maximum(m_old, st[5 + slot:6 + slot, :])
            alpha = jnp.exp2(m_old - m_new)
            p = jnp.exp2(s - m_new)
            p_ref[slot, c] = p.astype(p_ref.dtype)
            st[0:1, :] = m_new
            st[1:2, :] = alpha * st[1:2, :] + jnp.sum(p, axis=0, keepdims=True)
            st[2 + slot:3 + slot, :] = alpha

    def values(par, t, slot):
        off = pl.multiple_of(t * tk, LANES)
        vt = vall_ref[:, pl.ds(off, tk)]
        for c in range(2):
            k = 2 * par + c
            acc_ref[k] = stat_ref[k, 2 + slot:3 + slot, :] * acc_ref[k] + _nn(vt, p_ref[slot, c])

    def finish(par):
        o1 = acc_ref[2 * par] * (1.0 / stat_ref[2 * par, 1:2, :])
        o2 = acc_ref[2 * par + 1] * (1.0 / stat_ref[2 * par + 1, 1:2, :])
        ot = o1 - _diff_lambda(lam_ref, lam_init) * o2
        o_ref[0] = _head_norm(ot.T, g_ref[0], lam_init).astype(o_ref.dtype)

    @pl.when(j == 0)
    def _first_fill():
        reset(bp)
        scores(bp, 0, slot_of(0))
        scores(bp, 1, slot_of(1))
        softmax(bp, slot_of(0))

    @pl.when((j > 0) & (j < n_q))
    def _fill_and_drain():
        reset(bp)
        scores(bp, 0, slot_of(0))
        values(1 - bp, n - 2, slot_of(n - 2))
        softmax(1 - bp, slot_of(n - 1))
        scores(bp, 1, slot_of(1))
        values(1 - bp, n - 1, slot_of(n - 1))
        softmax(bp, slot_of(0))
        finish(1 - bp)

    @pl.when(j == n_q)
    def _last_drain():
        values(1 - bp, n - 2, slot_of(n - 2))
        softmax(1 - bp, slot_of(n - 1))
        values(1 - bp, n - 1, slot_of(n - 1))
        finish(1 - bp)

    @pl.when(j < n_q)
    def _steady():
        def iteration(t, t_static_slot):
            scores(bp, t, t_static_slot)
            values(bp, t - 2, slot_of(t - 2) if isinstance(t, int) else t_static_slot)
            softmax(bp, 1 - t_static_slot if not isinstance(t, int) or t - 1 > 0 else slot_of(0))

        if n > 2:
            iteration(2, slot_of(2))

        def pair(k, carry):
            iteration(3 + 2 * k, 1)
            iteration(4 + 2 * k, 0)
            return carry

        n_pairs = max(n - 3, 0) // 2
        lax.fori_loop(0, n_pairs, pair, 0)
        for t in range(3 + 2 * n_pairs, n):
            iteration(t, slot_of(t))


def _key_chunk(n_keys):
    return max(c for c in range(LANES, TK_C + 1, LANES) if n_keys % c == 0)


def _attn_c(c_lambda, subln3, cq, ck, cvt, xck, xcvt, lam_init):
    bsz, n, _ = cq.shape
    ctx = xck.shape[1]
    tq = min(TQ_C, n)
    tk = _key_chunk(n + ctx)
    return pl.pallas_call(
        functools.partial(_attn_c_kernel, lam_init=lam_init, n_tokens=n, tk=tk,
                          n_chunks=(n + ctx) // tk),
        out_shape=jax.ShapeDtypeStruct(cq.shape, MXU_DTYPE),
        grid=(bsz, C_HEADS, n // tq + 1),
        in_specs=[pl.BlockSpec((4, HEAD_DIM), lambda b, h, i: (0, 0)),
                  pl.BlockSpec((1, 1, LANES), lambda b, h, i: (h, 0, 0)),
                  pl.BlockSpec((1, tq, LANES), lambda b, h, i: (b, jnp.minimum(i, n // tq - 1), h)),
                  pl.BlockSpec((1, n, LANES), lambda b, h, i: (b, 0, h)),
                  pl.BlockSpec((1, LANES, n), lambda b, h, i: (b, h, 0)),
                  pl.BlockSpec((1, ctx, LANES), lambda b, h, i: (b, 0, h)),
                  pl.BlockSpec((1, LANES, ctx), lambda b, h, i: (b, h, 0))],
        out_specs=pl.BlockSpec((1, tq, LANES), lambda b, h, i: (b, jnp.maximum(i - 1, 0), h)),
        scratch_shapes=[pltpu.VMEM((n + ctx, LANES), MXU_DTYPE),
                        pltpu.VMEM((n + ctx, LANES), MXU_DTYPE),
                        pltpu.VMEM((LANES, n + ctx), MXU_DTYPE),
                        pltpu.VMEM((3, 2, tk, tq), F32),
                        pltpu.VMEM((3, 2, tk, tq), MXU_DTYPE),
                        pltpu.VMEM((4, 8, tq), F32),
                        pltpu.VMEM((4, LANES, tq), F32)],
        compiler_params=_params(("arbitrary", "arbitrary", "arbitrary")),
        name="attn_differential",
    )(c_lambda, subln3, cq, ck, cvt, xck, xcvt)


def _ctx_attn_kernel(sink_ref, lam_ref, g_ref, aq_ref, ak_ref, aks_ref, av_ref, avs_ref,
                     bq_ref, bk_ref, bv_ref, cq_ref, ck_ref, cv_ref, ya_ref, yb_ref, yc_ref,
                     *, lam_init):
    lo = _lane_lo()
    zero = jnp.zeros((), ak_ref.dtype)
    sel = lambda t, half: jnp.where(lo, t, zero) if half == 0 else jnp.where(lo, zero, t)
    k_of = {(0, 0): sel(ak_ref[0], 0), (0, 1): sel(aks_ref[0], 1),
            (1, 0): sel(aks_ref[0], 0), (1, 1): sel(ak_ref[0], 1)}
    v_of = {(0, 0): av_ref[0], (0, 1): avs_ref[0], (1, 0): avs_ref[0], (1, 1): av_ref[0]}
    for hp in range(A_HEADS // 2):
        cols = slice(hp * LANES, (hp + 1) * LANES)
        qa, qb = aq_ref[0, :, cols], bq_ref[0, :, cols]
        kb, vb = bk_ref[0, :, cols], bv_ref[0, :, cols]
        oa, ob = [], []
        for half in range(2):
            h = 2 * hp + half
            g = h // (A_HEADS // A_KV_HEADS)
            oa.append(_softmax_pv([_nt(qa, k_of[(g, half)])], [v_of[(g, half)]],
                                  extra_logit=sink_ref[h]))
            ob.append(_softmax_pv([_nt(qb, sel(kb, half))], [vb]))
        ya_ref[0, :, cols] = jnp.where(lo, oa[0], oa[1]).astype(ya_ref.dtype)
        yb_ref[0, :, cols] = jnp.where(lo, ob[0], ob[1]).astype(yb_ref.dtype)
    lam = _diff_lambda(lam_ref, lam_init)
    for h in range(C_HEADS):
        cols = slice(h * LANES, (h + 1) * LANES)
        q, k, v = cq_ref[0, :, cols], ck_ref[0, :, cols], cv_ref[0, :, cols]
        o = (_softmax_pv([_nt(q, sel(k, 0))], [v], exp=jnp.exp2)
             - lam * _softmax_pv([_nt(q, sel(k, 1))], [v], exp=jnp.exp2))
        yc_ref[0, :, cols] = _head_norm(o, g_ref[h], lam_init).astype(yc_ref.dtype)


def _ctx_attn(sink, c_lambda, subln3, con, lam_init):
    aq, ak, aks, av, avs, bq, bk, bv, cq, ck, cv = con[:11]
    bsz, ctx, _ = aq.shape
    wide = pl.BlockSpec((1, ctx, 512), lambda b: (b, 0, 0))
    nar = pl.BlockSpec((1, ctx, LANES), lambda b: (b, 0, 0))
    return pl.pallas_call(
        functools.partial(_ctx_attn_kernel, lam_init=lam_init),
        out_shape=[jax.ShapeDtypeStruct(aq.shape, MXU_DTYPE)] * 3,
        grid=(bsz,),
        in_specs=[pl.BlockSpec(memory_space=pltpu.SMEM),
                  pl.BlockSpec((4, HEAD_DIM), lambda b: (0, 0)),
                  pl.BlockSpec((C_HEADS, 1, LANES), lambda b: (0, 0, 0)),
                  wide, nar, nar, nar, nar, wide, wide, wide, wide, wide, wide],
        out_specs=[wide] * 3,
        compiler_params=_params(("parallel",)),
        name="attn_context",
    )(sink, c_lambda, subln3, aq, ak, aks, av, avs, bq, bk, bv, cq, ck, cv)


def _merge_kernel(x_ref, gt_ref, ya_ref, yb_ref, yc_ref, ga_ref, gb_ref, gc_ref,
                  wa_ref, wb_ref, wc_ref, wo_ref, o_ref):
    m = (ga_ref[0].astype(F32) * _nn(ya_ref[0], wa_ref[...])
         + gb_ref[0].astype(F32) * _nn(yb_ref[0], wb_ref[...])
         + gc_ref[0].astype(F32) * _nn(yc_ref[0], wc_ref[...]))
    o_ref[0] = x_ref[0] + gt_ref[0] * _nn(m.astype(MXU_DTYPE), wo_ref[...])


def _merge(x, mod3, mod_row, ya, yb, yc, ga, gb, gc, wa, wb, wc, wo):
    bsz, n, d = x.shape
    tm = min(TM_PROJ, n)
    tok = lambda w: pl.BlockSpec((1, tm, w), lambda b, i: (b, i, 0))
    return pl.pallas_call(
        _merge_kernel,
        out_shape=jax.ShapeDtypeStruct(x.shape, F32),
        grid=(bsz, n // tm),
        in_specs=[tok(d), pl.BlockSpec((1, 1, d), lambda b, i: (mod_row(b), 0, 2)),
                  tok(512), tok(512), tok(512), tok(d), tok(d), tok(d),
                  _resident(wa.shape), _resident(wb.shape), _resident(wc.shape), _resident(wo.shape)],
        out_specs=tok(d),
        compiler_params=_params(("parallel", "parallel")),
        name="merge",
    )(x, mod3, ya, yb, yc, ga, gb, gc, wa, wb, wc, wo)


def _top_values(ref, row0, n_rows, count, out_ref, out_row0, floor):
    for k in range(count):
        cur = ref[row0:row0 + n_rows]
        m = jnp.max(cur, axis=0, keepdims=True)
        out_ref[out_row0 + k:out_row0 + k + 1] = m
        if k + 1 < count:
            ref[row0:row0 + n_rows] = jnp.where(cur == m, floor, cur)


def _batcher_network(n):
    pairs = []
    p = 1
    while p < n:
        k = p
        while k >= 1:
            for j in range(k % p, n - k, 2 * k):
                for i in range(min(k, n - j - k)):
                    if (i + j) // (2 * p) == (i + j + k) // (2 * p):
                        pairs.append((i + j, i + j + k))
            k //= 2
        p *= 2
    return tuple(pairs)


N_TOP = PEER_TOPK + 1
SUBLANES = 8
_SORT_NET = _batcher_network(PEER_KEYS // SUBLANES)
_CAND_COUNTS = tuple(N_TOP // (a + 1) for a in range(N_TOP))
_CAND_OFFS = tuple(int(v) for v in np.cumsum((0,) + _CAND_COUNTS))
N_CAND = -(-_CAND_OFFS[-1] // SUBLANES) * SUBLANES


def _top_sorted(s_ref, lanes, out_ref, slot):
    n_grp = PEER_KEYS // SUBLANES
    lists = [s_ref[r * SUBLANES:(r + 1) * SUBLANES, lanes] for r in range(n_grp)]
    for a, b in _SORT_NET:
        lists[a], lists[b] = jnp.maximum(lists[a], lists[b]), jnp.minimum(lists[a], lists[b])
    for t in range(N_TOP):
        head = lists[0]
        m = jnp.max(head, axis=0, keepdims=True)
        out_ref[slot, t:t + 1, lanes] = m
        remaining = N_TOP - 1 - t
        hit = head == m
        for k in range(remaining):
            below = lists[k + 1] if k + 1 < n_grp else -jnp.inf
            lists[k] = jnp.where(hit, below, lists[k])


def _route_kernel(x_ref, g_ref, sc_ref, sh_ref, wq_ref, keys_ref,
                  h_ref, e1_ref, e2_ref, thr_ref, s_ref, top_ref, cand_ref):
    tt = x_ref.shape[1]
    h2 = _rms_mod(x_ref[0], g_ref[...], sc_ref[0], sh_ref[0]).astype(MXU_DTYPE)
    h_ref[0] = h2
    q = _nn(h2, wq_ref[...]).astype(MXU_DTYPE)
    top_ref[:, N_TOP:, :] = jnp.full((3, top_ref.shape[1] - N_TOP, tt), -jnp.inf, F32)
    cand_ref[_CAND_OFFS[-1]:, :] = jnp.full((N_CAND - _CAND_OFFS[-1], tt), -1.0, F32)
    for h in range(PEER_HEADS):
        tops = []
        for p, e_ref in ((0, e1_ref), (1, e2_ref)):
            hp = 2 * h + p
            s_ref[...] = _nt(keys_ref[hp], q[:, hp * LANES:(hp + 1) * LANES])
            for tc in range(tt // LANES):
                _top_sorted(s_ref, slice(tc * LANES, (tc + 1) * LANES), top_ref, p)
            m = top_ref[p, 0:1]
            e_ref[0, h] = jnp.exp(s_ref[...] - m)
            tops.append(jnp.exp(top_ref[p] - m))
        e1_top, e2_top = tops
        for a in range(N_TOP):
            cand_ref[_CAND_OFFS[a]:_CAND_OFFS[a + 1]] = e1_top[a:a + 1] * e2_top[0:_CAND_COUNTS[a]]
        _top_values(cand_ref, 0, N_CAND, N_TOP, top_ref.at[2], 0, -1.0)
        best = top_ref[2]
        inv_z = 1.0 / jnp.sum(best[0:PEER_TOPK], axis=0, keepdims=True)
        e1_ref[0, h] = e1_ref[0, h] * inv_z
        thr_ref[0, h:h + 1] = 0.5 * (best[PEER_TOPK - 1:PEER_TOPK] + best[PEER_TOPK:N_TOP]) * inv_z


def _route(x, g2, mod3, mod_row, wq, keys):
    bsz, n, d = x.shape
    tt = min(TT_PEER, n)
    gate_shape = jax.ShapeDtypeStruct((bsz, PEER_HEADS, PEER_KEYS, n), F32)
    gate_spec = pl.BlockSpec((1, PEER_HEADS, PEER_KEYS, tt), lambda b, i: (b, 0, 0, i))
    return pl.pallas_call(
        _route_kernel,
        out_shape=[jax.ShapeDtypeStruct((bsz, n, d), MXU_DTYPE), gate_shape, gate_shape,
                   jax.ShapeDtypeStruct((bsz, PEER_HEADS, n), F32)],
        grid=(bsz, n // tt),
        in_specs=[pl.BlockSpec((1, tt, d), lambda b, i: (b, i, 0)),
                  pl.BlockSpec((1, d), lambda b, i: (0, 0)),
                  pl.BlockSpec((1, 1, d), lambda b, i: (mod_row(b), 0, 4)),
                  pl.BlockSpec((1, 1, d), lambda b, i: (mod_row(b), 0, 3)),
                  _resident(wq.shape), _resident(keys.shape)],
        out_specs=[pl.BlockSpec((1, tt, d), lambda b, i: (b, i, 0)), gate_spec, gate_spec,
                   pl.BlockSpec((1, PEER_HEADS, tt), lambda b, i: (b, 0, i))],
        scratch_shapes=[pltpu.VMEM((PEER_KEYS, tt), F32),
                        pltpu.VMEM((3, 3 * SUBLANES, tt), F32),
                        pltpu.VMEM((N_CAND, tt), F32)],
        compiler_params=_params(("parallel", "parallel")),
        name="peer_route",
    )(x, g2, mod3, mod3, wq, keys)


def _expert_kernel(x_ref, gt_ref, gf_ref, h_ref, u_ref, v_ref, e1_ref, e2_ref, thr_ref, o_ref,
                   acc_ref, a_ref, w_ref, *, final_norm):
    eb = pl.program_id(2)
    tt = h_ref.shape[1]

    @pl.when(eb == 0)
    def _zero():
        acc_ref[...] = jnp.zeros_like(acc_ref)

    a_ref[...] = _nt(u_ref[...], h_ref[0])
    sub = 64
    n_i = TE_PEER // PEER_KEYS
    i0 = pl.multiple_of(eb * n_i, n_i)
    for il in range(n_i):
        for tc in range(tt // LANES):
            lanes = slice(tc * LANES, (tc + 1) * LANES)
            e1_rows = [e1_ref[0, h, pl.ds(i0, n_i), lanes][il:il + 1] for h in range(PEER_HEADS)]
            thr_rows = [thr_ref[0, h:h + 1, lanes] for h in range(PEER_HEADS)]
            for js in range(PEER_KEYS // sub):
                rows = slice(js * sub, (js + 1) * sub)
                gate = None
                for h in range(PEER_HEADS):
                    prod = e2_ref[0, h, rows, lanes] * e1_rows[h]
                    sel = jnp.where(prod >= thr_rows[h], prod, 0.0)
                    gate = sel if gate is None else gate + sel
                erows = slice(il * PEER_KEYS + js * sub, il * PEER_KEYS + (js + 1) * sub)
                a = a_ref[erows, lanes]
                act = (0.5 * a) * (1.0 + lax.erf(a * math.sqrt(0.5)))
                w_ref[erows, lanes] = gate * act
    acc_ref[...] += _nn(w_ref[...].T.astype(MXU_DTYPE), v_ref[...])

    @pl.when(eb == pl.num_programs(2) - 1)
    def _finish():
        y = x_ref[0] + gt_ref[0] * acc_ref[...]
        if final_norm:
            y = (y * lax.rsqrt(jnp.mean(y * y, axis=-1, keepdims=True) + NORM_EPS)) * gf_ref[...]
        o_ref[0] = y


def _experts(x, mod3, mod_row, g_final, h2, u, v, e1, e2, thr, final_norm):
    bsz, n, d = x.shape
    tt = min(TT_PEER, n)
    n_exp = u.shape[0]
    gate_spec = pl.BlockSpec((1, PEER_HEADS, PEER_KEYS, tt), lambda b, i, e: (b, 0, 0, i))
    return pl.pallas_call(
        functools.partial(_expert_kernel, final_norm=final_norm),
        out_shape=jax.ShapeDtypeStruct(x.shape, F32),
        grid=(bsz, n // tt, n_exp // TE_PEER),
        in_specs=[pl.BlockSpec((1, tt, d), lambda b, i, e: (b, i, 0)),
                  pl.BlockSpec((1, 1, d), lambda b, i, e: (mod_row(b), 0, 5)),
                  pl.BlockSpec((1, d), lambda b, i, e: (0, 0)),
                  pl.BlockSpec((1, tt, d), lambda b, i, e: (b, i, 0)),
                  pl.BlockSpec((TE_PEER, d), lambda b, i, e: (e, 0)),
                  pl.BlockSpec((TE_PEER, d), lambda b, i, e: (e, 0)),
                  gate_spec, gate_spec,
                  pl.BlockSpec((1, PEER_HEADS, tt), lambda b, i, e: (b, 0, i))],
        out_specs=pl.BlockSpec((1, tt, d), lambda b, i, e: (b, i, 0)),
        scratch_shapes=[pltpu.VMEM((tt, d), F32),
                        pltpu.VMEM((TE_PEER, tt), F32),
                        pltpu.VMEM((TE_PEER, tt), F32)],
        compiler_params=_params(("parallel", "parallel", "arbitrary")),
        name="peer_experts",
    )(x, mod3, g_final, h2, u, v, e1, e2, thr)


def _rope_tables(n):
    t = jnp.arange(n, dtype=jnp.int32)
    row = (t // GRID_W).astype(F32)
    col = (t % GRID_W).astype(F32)
    n_freq = HEAD_DIM // 4
    inv = ROPE_THETA ** (-jnp.arange(n_freq, dtype=F32) / n_freq)
    ang = jnp.concatenate([row[:, None] * inv, col[:, None] * inv], axis=-1)
    cos, sin = jnp.cos(ang), jnp.sin(ang)
    return jnp.tile(cos, (1, 4)), jnp.tile(jnp.concatenate([-sin, sin], axis=-1), (1, 2))


def kernel(x, c, ctx, c_ctx, w_mod, b_mod, g_norm1, g_norm2, w_in, a_sink, b_rpb, c_lambda,
           c_subln, w_branch_a, w_branch_b, w_branch_c, w_out, peer_wq, peer_keys, peer_u,
           peer_v, g_final):
    bsz, n, d = x.shape
    depth = w_mod.shape[0]
    n_ctx = ctx.shape[1]
    assert d == D_MODEL and bsz + 1 <= 8
    assert n % (B_ROWS * GRID_W) == 0 and n // GRID_W >= 4 * B_KROWS
    assert n % (A_QBLOCKS * A_BLOCK) == 0 and (n + n_ctx) % LANES == 0

    cvec = jnp.zeros((8, d), F32).at[:bsz].set(c).at[bsz].set(c_ctx)
    mod = _modulation(cvec, w_mod, b_mod)
    lat_row = lambda b: b
    ctx_row = lambda b: bsz
    cos_t, sin_t = _rope_tables(n)
    ones_t, zeros_t = jnp.ones((n_ctx, LANES), F32), jnp.zeros((n_ctx, LANES), F32)
    cast = lambda w: w.astype(MXU_DTYPE)
    gfin = g_final.reshape(1, d)

    xc = ctx
    for l in range(depth):
        last = l == depth - 1
        lam_init = 0.8 - 0.6 * math.exp(-0.3 * l)
        mod3 = mod[l].reshape(8, 1, 6 * d)
        g1, g2 = g_norm1[l].reshape(1, d), g_norm2[l].reshape(1, d)
        w_in_l = cast(w_in[l])
        subln3 = c_subln[l].reshape(C_HEADS, 1, LANES)
        wa, wb, wc, wo = cast(w_branch_a[l]), cast(w_branch_b[l]), cast(w_branch_c[l]), cast(w_out[l])
        wq = cast(peer_wq[l])
        keys = cast(peer_keys[l].reshape(2 * PEER_HEADS, PEER_KEYS, LANES))
        u, v = cast(peer_u[l]), cast(peer_v[l])

        lat = _in_proj(x, g1, mod3, lat_row, w_in_l, cos_t, sin_t)
        con = _in_proj(xc, g1, mod3, ctx_row, w_in_l, ones_t, zeros_t)
        (aq, ak, aks, av, avs, bq, bk, bv, cq, ck, _, ga, gb, gc, cvt) = lat
        ya = _attn_a(a_sink[l], aq, ak, aks, av, avs, con[1], con[2], con[3], con[4])
        yb = _attn_b(b_rpb[l], bq, bk, bv, con[6], con[7])
        yc = _attn_c(c_lambda[l], subln3, cq, ck, cvt, con[9], con[14], lam_init)
        x = _merge(x, mod3, lat_row, ya, yb, yc, ga, gb, gc, wa, wb, wc, wo)
        h2, e1, e2, thr = _route(x, g2, mod3, lat_row, wq, keys)
        x = _experts(x, mod3, lat_row, gfin, h2, u, v, e1, e2, thr, final_norm=last)

        if not last:
            ya_c, yb_c, yc_c = _ctx_attn(a_sink[l], c_lambda[l], subln3, con, lam_init)
            xc = _merge(xc, mod3, ctx_row, ya_c, yb_c, yc_c, con[11], con[12], con[13],
                        wa, wb, wc, wo)
            h2c, e1c, e2c, thrc = _route(xc, g2, mod3, ctx_row, wq, keys)
            xc = _experts(xc, mod3, ctx_row, gfin, h2c, u, v, e1c, e2c, thrc, final_norm=False)
    return x
```

```python
import functools
import math

import jax
import jax.numpy as jnp
import numpy as np
from jax import lax
from jax.experimental import pallas as pl
from jax.experimental.pallas import tpu as pltpu

F32 = jnp.float32
BF16 = jnp.bfloat16
MXU_DTYPE = BF16

LANES = 128
MXU_WIDTH = 256
VMEM_LIMIT_BYTES = 56 * 1024 * 1024

HEAD_DIM = 64
GRID_W = 64
ROPE_THETA = 10000.0
NORM_EPS = 1e-6
NEG_INF = -1e30
LOG2_E = math.log2(math.e)
A_HEADS, A_KV_HEADS, A_WINDOW, A_BLOCK = 8, 2, 128, 128
B_HEADS, NA_ROWS, NA_COLS = 8, 8, 16
C_HEADS = 4
PEER_HEADS, PEER_KEYS, PEER_TOPK = 8, 128, 16
D_MODEL = 1024
IN_SPLITS = (512, 128, 128, 512, 512, 512, 512, 512, 512, 1024, 1024, 1024)
IN_OFFS = tuple(int(v) for v in np.cumsum((0,) + IN_SPLITS))

TM_PROJ = 512
A_QBLOCKS = 4
TQ_C = 1024
TK_C = 768
B_ROWS = 8
B_KROWS = 4
TT_PEER = 512
TE_PEER = 1024
N_PT = 30


def _nt(a, b):
    return lax.dot_general(a, b, (((1,), (1,)), ((), ())), preferred_element_type=F32)


def _nn(a, b):
    return jnp.dot(a, b, preferred_element_type=F32)


def _params(sem):
    return pltpu.CompilerParams(dimension_semantics=sem, vmem_limit_bytes=VMEM_LIMIT_BYTES)


def _resident(shape):
    nd = len(shape)
    return pl.BlockSpec(shape, lambda *_: (0,) * nd, pipeline_mode=pl.Buffered(1))


def _mod_kernel(c_ref, w_ref, b_ref, o_ref):
    c = c_ref[...]
    s = c * jax.nn.sigmoid(c)
    w = w_ref[0]
    s_hi = s.astype(MXU_DTYPE)
    s_lo = (s - s_hi.astype(F32)).astype(MXU_DTYPE)
    w_hi = w.astype(MXU_DTYPE)
    w_lo = (w - w_hi.astype(F32)).astype(MXU_DTYPE)
    acc = _nn(s_hi, w_hi) + _nn(s_lo, w_hi) + _nn(s_hi, w_lo)
    o_ref[0] = acc + b_ref[0]


def _modulation(cvec, w_mod, b_mod):
    depth, d, n6 = w_mod.shape
    tn = 1536
    return pl.pallas_call(
        _mod_kernel,
        out_shape=jax.ShapeDtypeStruct((depth, 8, n6), F32),
        grid=(depth, n6 // tn),
        in_specs=[pl.BlockSpec((8, d), lambda l, j: (0, 0)),
                  pl.BlockSpec((1, d, tn), lambda l, j: (l, 0, j)),
                  pl.BlockSpec((1, 1, tn), lambda l, j: (l, 0, j))],
        out_specs=pl.BlockSpec((1, 8, tn), lambda l, j: (l, 0, j)),
        compiler_params=_params(("parallel", "parallel")),
        name="modulation",
    )(cvec, w_mod, b_mod.reshape(depth, 1, n6))


def _rms_mod(x, g, sc, sh):
    y = x * lax.rsqrt(jnp.mean(x * x, axis=-1, keepdims=True) + NORM_EPS)
    return (y * g) * (1.0 + sc) + sh


def _in_proj_kernel(x_ref, g_ref, sc_ref, sh_ref, w_ref, cos_ref, sin_ref,
                    aq_ref, ak_ref, aks_ref, av_ref, avs_ref, bq_ref, bk_ref, bv_ref,
                    cq_ref, ck_ref, cv_ref, ga_ref, gb_ref, gc_ref, cvt_ref):
    h = _rms_mod(x_ref[0], g_ref[...], sc_ref[0], sh_ref[0]).astype(MXU_DTYPE)
    cos = cos_ref[...]
    sin = sin_ref[...]
    lane = lax.broadcasted_iota(jnp.int32, (1, LANES), 1)
    first_half = (lane % HEAD_DIM) < (HEAD_DIM // 2)
    scale = HEAD_DIM ** -0.5

    def proj(seg, jp):
        c0 = IN_OFFS[seg] + jp * MXU_WIDTH
        t = _nn(h, w_ref[:, c0:c0 + MXU_WIDTH])
        return t[:, :LANES], t[:, LANES:]

    def rope(v):
        rot = jnp.where(first_half, pltpu.roll(v, LANES - 32, 1), pltpu.roll(v, 32, 1))
        return v * cos + rot * sin

    def emit(ref, seg, fn=lambda v: v):
        for jp in range(IN_SPLITS[seg] // MXU_WIDTH):
            for k, part in enumerate(proj(seg, jp)):
                j = 2 * jp + k
                ref[0, :, j * LANES:(j + 1) * LANES] = fn(part).astype(ref.dtype)

    emit(aq_ref, 0, lambda v: rope(v) * scale)
    emit(bq_ref, 3, lambda v: v * scale)
    emit(bk_ref, 4)
    emit(bv_ref, 5)
    emit(cq_ref, 6, lambda v: rope(v) * (scale * LOG2_E))
    emit(ck_ref, 7, rope)
    for jp in range(IN_SPLITS[8] // MXU_WIDTH):
        for k, cv in enumerate(proj(8, jp)):
            j = 2 * jp + k
            cv_ref[0, :, j * LANES:(j + 1) * LANES] = cv.astype(cv_ref.dtype)
            cvt_ref[0, j * LANES:(j + 1) * LANES, :] = cv.T.astype(cvt_ref.dtype)
    ak, av = proj(1, 0)
    ak = rope(ak)
    ak_ref[0] = ak.astype(ak_ref.dtype)
    aks_ref[0] = pltpu.roll(ak, HEAD_DIM, 1).astype(aks_ref.dtype)
    av_ref[0] = av.astype(av_ref.dtype)
    avs_ref[0] = pltpu.roll(av, HEAD_DIM, 1).astype(avs_ref.dtype)
    emit(ga_ref, 9, jax.nn.sigmoid)
    emit(gb_ref, 10, jax.nn.sigmoid)
    emit(gc_ref, 11, jax.nn.sigmoid)


def _in_proj(x, g1, mod3, mod_row, w_in, cos_t, sin_t):
    bsz, n, d = x.shape
    tm = min(TM_PROJ, n)
    tok = lambda w: pl.BlockSpec((1, tm, w), lambda b, i: (b, i, 0))
    widths = (512, 128, 128, 128, 128, 512, 512, 512, 512, 512, 512, 1024, 1024, 1024)
    return pl.pallas_call(
        _in_proj_kernel,
        out_shape=[jax.ShapeDtypeStruct((bsz, n, w), MXU_DTYPE) for w in widths]
        + [jax.ShapeDtypeStruct((bsz, 512, n), MXU_DTYPE)],
        grid=(bsz, n // tm),
        in_specs=[tok(d),
                  pl.BlockSpec((1, d), lambda b, i: (0, 0)),
                  pl.BlockSpec((1, 1, d), lambda b, i: (mod_row(b), 0, 1)),
                  pl.BlockSpec((1, 1, d), lambda b, i: (mod_row(b), 0, 0)),
                  _resident(w_in.shape),
                  pl.BlockSpec((tm, LANES), lambda b, i: (i, 0)),
                  pl.BlockSpec((tm, LANES), lambda b, i: (i, 0))],
        out_specs=[tok(w) for w in widths] + [pl.BlockSpec((1, 512, tm), lambda b, i: (b, 0, i))],
        compiler_params=_params(("parallel", "parallel")),
        name="in_proj",
    )(x, g1, mod3, mod3, w_in, cos_t, sin_t)


def _lane_lo():
    return lax.broadcasted_iota(jnp.int32, (1, LANES), 1) < HEAD_DIM


def _softmax_pv(s_list, v_list, extra_logit=None, exp=jnp.exp):
    m = functools.reduce(jnp.maximum, [jnp.max(s, axis=-1, keepdims=True) for s in s_list])
    if extra_logit is not None:
        m = jnp.maximum(m, extra_logit)
    l = 0.0 if extra_logit is None else exp(extra_logit - m)
    o = 0.0
    for s, v in zip(s_list, v_list):
        p = exp(s - m)
        l = l + jnp.sum(p, axis=-1, keepdims=True)
        o = o + _nn(p.astype(v.dtype), v)
    return o / l


def _attn_a_kernel(sink_ref, q_ref, kp_ref, kc_ref, kn_ref, ksp_ref, ksc_ref, ksn_ref,
                   vp_ref, vc_ref, vn_ref, vsp_ref, vsc_ref, vsn_ref,
                   xk_ref, xks_ref, xv_ref, xvs_ref, o_ref, *, n_tokens):
    i = pl.program_id(1)
    lo = _lane_lo()
    cat = lambda refs: jnp.concatenate([r[0] for r in refs], axis=0)
    k_loc, ks_loc = cat((kp_ref, kc_ref, kn_ref)), cat((ksp_ref, ksc_ref, ksn_ref))
    v_loc, vs_loc = cat((vp_ref, vc_ref, vn_ref)), cat((vsp_ref, vsc_ref, vsn_ref))
    zero = jnp.zeros((), k_loc.dtype)
    k_of = {(0, 0): (jnp.where(lo, k_loc, zero), jnp.where(lo, xk_ref[0], zero)),
            (0, 1): (jnp.where(lo, zero, ks_loc), jnp.where(lo, zero, xks_ref[0])),
            (1, 0): (jnp.where(lo, ks_loc, zero), jnp.where(lo, xks_ref[0], zero)),
            (1, 1): (jnp.where(lo, zero, k_loc), jnp.where(lo, zero, xk_ref[0]))}
    v_of = {(0, 0): (v_loc, xv_ref[0]), (0, 1): (vs_loc, xvs_ref[0]),
            (1, 0): (vs_loc, xvs_ref[0]), (1, 1): (v_loc, xv_ref[0])}
    chains = []
    for qb in range(A_QBLOCKS):
        blk = i * A_QBLOCKS + qb
        qpos = blk * A_BLOCK + lax.broadcasted_iota(jnp.int32, (A_BLOCK, 1), 0)
        kpos = (blk - 1) * A_BLOCK + lax.broadcasted_iota(jnp.int32, (1, 3 * A_BLOCK), 1)
        dist = qpos - kpos
        valid = (jnp.maximum(dist, -dist) <= A_WINDOW) & (kpos >= 0) & (kpos < n_tokens)
        qrows = slice(qb * A_BLOCK, (qb + 1) * A_BLOCK)
        krows = slice(qb * A_BLOCK, (qb + 3) * A_BLOCK)
        for hp in range(A_HEADS // 2):
            qp = q_ref[0, qrows, hp * LANES:(hp + 1) * LANES]
            for half in range(2):
                h = 2 * hp + half
                g = h // (A_HEADS // A_KV_HEADS)
                (kl, kx), (vl, vx) = k_of[(g, half)], v_of[(g, half)]
                s_loc = jnp.where(valid, _nt(qp, kl[krows]), NEG_INF)
                chains.append(([s_loc, _nt(qp, kx)], [vl[krows], vx], sink_ref[h]))
    outs = [_softmax_pv(s, v, extra_logit=sink) for s, v, sink in chains]
    for qb in range(A_QBLOCKS):
        qrows = slice(qb * A_BLOCK, (qb + 1) * A_BLOCK)
        for hp in range(A_HEADS // 2):
            even, odd = outs[(qb * (A_HEADS // 2) + hp) * 2:(qb * (A_HEADS // 2) + hp) * 2 + 2]
            o_ref[0, qrows, hp * LANES:(hp + 1) * LANES] = jnp.where(lo, even, odd).astype(o_ref.dtype)


def _attn_a(sink, aq, ak, aks, av, avs, xak, xaks, xav, xavs):
    bsz, n, _ = aq.shape
    nb = n // A_BLOCK
    tq = A_QBLOCKS * A_BLOCK
    ctx = xak.shape[1]
    prev = pl.BlockSpec((1, A_BLOCK, LANES),
                        lambda b, i: (b, jnp.maximum(i * A_QBLOCKS - 1, 0), 0))
    cur = pl.BlockSpec((1, tq, LANES), lambda b, i: (b, i, 0))
    nxt = pl.BlockSpec((1, A_BLOCK, LANES),
                       lambda b, i: (b, jnp.minimum((i + 1) * A_QBLOCKS, nb - 1), 0))
    cx = pl.BlockSpec((1, ctx, LANES), lambda b, i: (b, 0, 0))
    return pl.pallas_call(
        functools.partial(_attn_a_kernel, n_tokens=n),
        out_shape=jax.ShapeDtypeStruct(aq.shape, MXU_DTYPE),
        grid=(bsz, n // tq),
        in_specs=[pl.BlockSpec(memory_space=pltpu.SMEM),
                  pl.BlockSpec((1, tq, 512), lambda b, i: (b, i, 0)),
                  prev, cur, nxt, prev, cur, nxt, prev, cur, nxt, prev, cur, nxt, cx, cx, cx, cx],
        out_specs=pl.BlockSpec((1, tq, 512), lambda b, i: (b, i, 0)),
        compiler_params=_params(("parallel", "parallel")),
        name="attn_window",
    )(sink, aq, ak, ak, ak, aks, aks, aks, av, av, av, avs, avs, avs, xak, xaks, xav, xavs)


def _attn_b_kernel(rpb_ref, q_ref, k0_ref, k1_ref, k2_ref, k3_ref, v0_ref, v1_ref, v2_ref, v3_ref,
                   xk_ref, xv_ref, o_ref, pt_ref, *, n_rows):
    pair, i = pl.program_id(0), pl.program_id(2)
    n_a = 2 * NA_ROWS - 1
    n_b = 2 * NA_COLS - 1

    @pl.when((pl.program_id(1) == 0) & (i == 0))
    def _build_bias_tables():
        qc = lax.broadcasted_iota(jnp.int32, (GRID_W, LANES), 0)
        ln = lax.broadcasted_iota(jnp.int32, (GRID_W, LANES), 1)
        kc = ln % GRID_W
        hi = ln >= GRID_W
        cstart = jnp.clip(qc - NA_COLS // 2, 0, GRID_W - NA_COLS)
        col_ok = (kc >= cstart) & (kc < cstart + NA_COLS)
        d = kc - qc + (NA_COLS - 1)
        for hh in range(2):
            h = pair * 2 + hh

            def body(ai, carry):
                a = ai - 8
                a_ok, a1_ok = (a >= 0) & (a < n_a), (a + 1 >= 0) & (a + 1 < n_a)
                ra, ra1 = h * n_a + jnp.clip(a, 0, n_a - 1), h * n_a + jnp.clip(a + 1, 0, n_a - 1)
                t = jnp.full((GRID_W, LANES), NEG_INF, F32)
                for b in range(n_b):
                    va = jnp.where(a_ok, rpb_ref[ra, b], NEG_INF)
                    va1 = jnp.where(a1_ok, rpb_ref[ra1, b], NEG_INF)
                    t = jnp.where(d == b, jnp.where(hi, va1, va), t)
                pt_ref[hh, ai] = jnp.where(col_ok, t, NEG_INF)
                return carry

            lax.fori_loop(0, N_PT, body, 0)

    lo = _lane_lo()
    r0 = i * B_ROWS
    kb0 = jnp.clip(2 * i - 1, 0, n_rows // B_KROWS - 4)
    krow_lane = lax.broadcasted_iota(jnp.int32, (1, B_KROWS * GRID_W), 1) // GRID_W
    k_refs, v_refs = (k0_ref, k1_ref, k2_ref, k3_ref), (v0_ref, v1_ref, v2_ref, v3_ref)
    qp = q_ref[0]
    zero = jnp.zeros((), qp.dtype)
    raw = []
    for hh in range(2):
        keep = lambda t: jnp.where(lo, t, zero) if hh == 0 else jnp.where(lo, zero, t)
        raw.append([_nt(qp, keep(k_refs[j][0, 0])) for j in range(4)]
                   + [_nt(qp, keep(xk_ref[0]))])
    outs = []
    for hh in range(2):
        s_tiles = []
        for j in range(4):
            s = raw[hh][j]
            kr0 = (kb0 + j) * B_KROWS
            rows = []
            for qr in range(B_ROWS):
                r = r0 + qr
                rs = jnp.clip(r - NA_ROWS // 2, 0, n_rows - NA_ROWS)
                a0 = kr0 - r + (NA_ROWS - 1)
                bias = jnp.concatenate([pt_ref[hh, a0 + 8], pt_ref[hh, a0 + 10]], axis=1)
                row_ok = (kr0 + krow_lane >= rs) & (kr0 + krow_lane < rs + NA_ROWS)
                rows.append(jnp.where(row_ok, s[qr * GRID_W:(qr + 1) * GRID_W] + bias, NEG_INF))
            s_tiles.append(jnp.concatenate(rows, axis=0))
        s_tiles.append(raw[hh][4])
        outs.append(_softmax_pv(s_tiles, [v_refs[j][0, 0] for j in range(4)] + [xv_ref[0]]))
    o_ref[0] = jnp.where(lo, outs[0], outs[1]).astype(o_ref.dtype)


def _attn_b(rpb, bq, bk, bv, xbk, xbv):
    bsz, n, _ = bq.shape
    n_rows = n // GRID_W
    tq = B_ROWS * GRID_W
    tkb = B_KROWS * GRID_W
    ctx = xbk.shape[1]
    bk4 = bk.reshape(bsz, n // tkb, tkb, 512)
    bv4 = bv.reshape(bsz, n // tkb, tkb, 512)
    nkb = n // tkb

    def kspec(j):
        return pl.BlockSpec((1, 1, tkb, LANES),
                            lambda p, b, i: (b, jnp.clip(2 * i - 1, 0, nkb - 4) + j, 0, p))

    cx = pl.BlockSpec((1, ctx, LANES), lambda p, b, i: (b, 0, p))
    return pl.pallas_call(
        functools.partial(_attn_b_kernel, n_rows=n_rows),
        out_shape=jax.ShapeDtypeStruct(bq.shape, MXU_DTYPE),
        grid=(B_HEADS // 2, bsz, n // tq),
        in_specs=[pl.BlockSpec(memory_space=pltpu.SMEM),
                  pl.BlockSpec((1, tq, LANES), lambda p, b, i: (b, i, p)),
                  kspec(0), kspec(1), kspec(2), kspec(3), kspec(0), kspec(1), kspec(2), kspec(3),
                  cx, cx],
        out_specs=pl.BlockSpec((1, tq, LANES), lambda p, b, i: (b, i, p)),
        scratch_shapes=[pltpu.VMEM((2, N_PT, GRID_W, LANES), F32)],
        compiler_params=_params(("arbitrary", "arbitrary", "arbitrary")),
        name="attn_neighbourhood",
    )(rpb.reshape(B_HEADS * (2 * NA_ROWS - 1), 2 * NA_COLS - 1), bq,
      bk4, bk4, bk4, bk4, bv4, bv4, bv4, bv4, xbk, xbv)


def _diff_lambda(lam_ref, lam_init):
    lam = lam_ref[...]
    a = jnp.sum(lam[0:1] * lam[1:2], axis=-1, keepdims=True)
    b = jnp.sum(lam[2:3] * lam[3:4], axis=-1, keepdims=True)
    return jnp.exp(a) - jnp.exp(b) + lam_init


def _head_norm(o, g, lam_init):
    y = o * lax.rsqrt(jnp.mean(o * o, axis=-1, keepdims=True) + NORM_EPS)
    return (y * g) * (1.0 - lam_init)


def _attn_c_kernel(lam_ref, g_ref, q_ref, k_ref, vt_ref, xk_ref, xvt_ref, o_ref,
                   k1_ref, k2_ref, vall_ref, s_ref, p_ref, stat_ref, acc_ref,
                   *, lam_init, n_tokens, tk, n_chunks):
    lo = _lane_lo()

    @pl.when(pl.program_id(2) == 0)
    def _split_keys():
        zero = jnp.zeros((), k1_ref.dtype)
        k1_ref[0:n_tokens] = jnp.where(lo, k_ref[0], zero)
        k2_ref[0:n_tokens] = jnp.where(lo, zero, k_ref[0])
        k1_ref[n_tokens:] = jnp.where(lo, xk_ref[0], zero)
        k2_ref[n_tokens:] = jnp.where(lo, zero, xk_ref[0])

        vall_ref[:, 0:n_tokens] = vt_ref[0]
        vall_ref[:, n_tokens:] = xvt_ref[0]

    q = q_ref[0]
    tq = q.shape[0]
    key_refs = (k1_ref, k2_ref)
    n = n_chunks
    j = pl.program_id(2)
    n_q = pl.num_programs(2) - 1
    bp = j % 2

    slot_of = lambda c: 2 if c == 0 else c % 2

    def reset(par):
        for c in range(2):
            stat_ref[2 * par + c, 0:1, :] = jnp.full((1, tq), -jnp.inf, F32)
            stat_ref[2 * par + c, 1:2, :] = jnp.zeros((1, tq), F32)
            acc_ref[2 * par + c] = jnp.zeros(acc_ref.shape[1:], F32)

    def scores(par, t, slot):
        off = pl.multiple_of(t * tk, LANES)
        for c in range(2):
            s = _nt(key_refs[c][pl.ds(off, tk), :], q)
            s_ref[slot, c] = s
            stat_ref[2 * par + c, 5 + slot:6 + slot, :] = jnp.max(s, axis=0, keepdims=True)

    def softmax(par, slot):
        for c in range(2):
            st = stat_ref.at[2 * par + c]
            s = s_ref[slot, c]
            m_old = st[0:1, :]
            m_new = jnp.maximum(m_old, st[5 + slot:6 + slot, :])
            alpha = jnp.exp2(m_old - m_new)
            p = jnp.exp2(s - m_new)
            p_ref[slot, c] = p.astype(p_ref.dtype)
            st[0:1, :] = m_new
            st[1:2, :] = alpha * st[1:2, :] + jnp.sum(p, axis=0, keepdims=True)
            st[2 + slot:3 + slot, :] = alpha

    def values(par, t, slot):
        off = pl.multiple_of(t * tk, LANES)
        vt = vall_ref[:, pl.ds(off, tk)]
        for c in range(2):
            k = 2 * par + c
            acc_ref[k] = stat_ref[k, 2 + slot:3 + slot, :] * acc_ref[k] + _nn(vt, p_ref[slot, c])

    def finish(par):
        o1 = acc_ref[2 * par] * (1.0 / stat_ref[2 * par, 1:2, :])
        o2 = acc_ref[2 * par + 1] * (1.0 / stat_ref[2 * par + 1, 1:2, :])
        ot = o1 - _diff_lambda(lam_ref, lam_init) * o2
        o_ref[0] = _head_norm(ot.T, g_ref[0], lam_init).astype(o_ref.dtype)

    @pl.when(j == 0)
    def _first_fill():
        reset(bp)
        scores(bp, 0, slot_of(0))
        scores(bp, 1, slot_of(1))
        softmax(bp, slot_of(0))

    @pl.when((j > 0) & (j < n_q))
    def _fill_and_drain():
        reset(bp)
        scores(bp, 0, slot_of(0))
        values(1 - bp, n - 2, slot_of(n - 2))
        softmax(1 - bp, slot_of(n - 1))
        scores(bp, 1, slot_of(1))
        values(1 - bp, n - 1, slot_of(n - 1))
        softmax(bp, slot_of(0))
        finish(1 - bp)

    @pl.when(j == n_q)
    def _last_drain():
        values(1 - bp, n - 2, slot_of(n - 2))
        softmax(1 - bp, slot_of(n - 1))
        values(1 - bp, n - 1, slot_of(n - 1))
        finish(1 - bp)

    @pl.when(j < n_q)
    def _steady():
        def iteration(t, t_static_slot):
            scores(bp, t, t_static_slot)
            values(bp, t - 2, slot_of(t - 2) if isinstance(t, int) else t_static_slot)
            softmax(bp, 1 - t_static_slot if not isinstance(t, int) or t - 1 > 0 else slot_of(0))

        if n > 2:
            iteration(2, slot_of(2))

        def pair(k, carry):
            iteration(3 + 2 * k, 1)
            iteration(4 + 2 * k, 0)
            return carry

        n_pairs = max(n - 3, 0) // 2
        lax.fori_loop(0, n_pairs, pair, 0)
        for t in range(3 + 2 * n_pairs, n):
            iteration(t, slot_of(t))


def _key_chunk(n_keys):
    return max(c for c in range(LANES, TK_C + 1, LANES) if n_keys % c == 0)


def _attn_c(c_lambda, subln3, cq, ck, cvt, xck, xcvt, lam_init):
    bsz, n, _ = cq.shape
    ctx = xck.shape[1]
    tq = min(TQ_C, n)
    tk = _key_chunk(n + ctx)
    return pl.pallas_call(
        functools.partial(_attn_c_kernel, lam_init=lam_init, n_tokens=n, tk=tk,
                          n_chunks=(n + ctx) // tk),
        out_shape=jax.ShapeDtypeStruct(cq.shape, MXU_DTYPE),
        grid=(bsz, C_HEADS, n // tq + 1),
        in_specs=[pl.BlockSpec((4, HEAD_DIM), lambda b, h, i: (0, 0)),
                  pl.BlockSpec((1, 1, LANES), lambda b, h, i: (h, 0, 0)),
                  pl.BlockSpec((1, tq, LANES), lambda b, h, i: (b, jnp.minimum(i, n // tq - 1), h)),
                  pl.BlockSpec((1, n, LANES), lambda b, h, i: (b, 0, h)),
                  pl.BlockSpec((1, LANES, n), lambda b, h, i: (b, h, 0)),
                  pl.BlockSpec((1, ctx, LANES), lambda b, h, i: (b, 0, h)),
                  pl.BlockSpec((1, LANES, ctx), lambda b, h, i: (b, h, 0))],
        out_specs=pl.BlockSpec((1, tq, LANES), lambda b, h, i: (b, jnp.maximum(i - 1, 0), h)),
        scratch_shapes=[pltpu.VMEM((n + ctx, LANES), MXU_DTYPE),
                        pltpu.VMEM((n + ctx, LANES), MXU_DTYPE),
                        pltpu.VMEM((LANES, n + ctx), MXU_DTYPE),
                        pltpu.VMEM((3, 2, tk, tq), F32),
                        pltpu.VMEM((3, 2, tk, tq), MXU_DTYPE),
                        pltpu.VMEM((4, 8, tq), F32),
                        pltpu.VMEM((4, LANES, tq), F32)],
        compiler_params=_params(("arbitrary", "arbitrary", "arbitrary")),
        name="attn_differential",
    )(c_lambda, subln3, cq, ck, cvt, xck, xcvt)


def _ctx_attn_kernel(sink_ref, lam_ref, g_ref, aq_ref, ak_ref, aks_ref, av_ref, avs_ref,
                     bq_ref, bk_ref, bv_ref, cq_ref, ck_ref, cv_ref, ya_ref, yb_ref, yc_ref,
                     *, lam_init):
    lo = _lane_lo()
    zero = jnp.zeros((), ak_ref.dtype)
    sel = lambda t, half: jnp.where(lo, t, zero) if half == 0 else jnp.where(lo, zero, t)
    k_of = {(0, 0): sel(ak_ref[0], 0), (0, 1): sel(aks_ref[0], 1),
            (1, 0): sel(aks_ref[0], 0), (1, 1): sel(ak_ref[0], 1)}
    v_of = {(0, 0): av_ref[0], (0, 1): avs_ref[0], (1, 0): avs_ref[0], (1, 1): av_ref[0]}
    for hp in range(A_HEADS // 2):
        cols = slice(hp * LANES, (hp + 1) * LANES)
        qa, qb = aq_ref[0, :, cols], bq_ref[0, :, cols]
        kb, vb = bk_ref[0, :, cols], bv_ref[0, :, cols]
        oa, ob = [], []
        for half in range(2):
            h = 2 * hp + half
            g = h // (A_HEADS // A_KV_HEADS)
            oa.append(_softmax_pv([_nt(qa, k_of[(g, half)])], [v_of[(g, half)]],
                                  extra_logit=sink_ref[h]))
            ob.append(_softmax_pv([_nt(qb, sel(kb, half))], [vb]))
        ya_ref[0, :, cols] = jnp.where(lo, oa[0], oa[1]).astype(ya_ref.dtype)
        yb_ref[0, :, cols] = jnp.where(lo, ob[0], ob[1]).astype(yb_ref.dtype)
    lam = _diff_lambda(lam_ref, lam_init)
    for h in range(C_HEADS):
        cols = slice(h * LANES, (h + 1) * LANES)
        q, k, v = cq_ref[0, :, cols], ck_ref[0, :, cols], cv_ref[0, :, cols]
        o = (_softmax_pv([_nt(q, sel(k, 0))], [v], exp=jnp.exp2)
             - lam * _softmax_pv([_nt(q, sel(k, 1))], [v], exp=jnp.exp2))
        yc_ref[0, :, cols] = _head_norm(o, g_ref[h], lam_init).astype(yc_ref.dtype)


def _ctx_attn(sink, c_lambda, subln3, con, lam_init):
    aq, ak, aks, av, avs, bq, bk, bv, cq, ck, cv = con[:11]
    bsz, ctx, _ = aq.shape
    wide = pl.BlockSpec((1, ctx, 512), lambda b: (b, 0, 0))
    nar = pl.BlockSpec((1, ctx, LANES), lambda b: (b, 0, 0))
    return pl.pallas_call(
        functools.partial(_ctx_attn_kernel, lam_init=lam_init),
        out_shape=[jax.ShapeDtypeStruct(aq.shape, MXU_DTYPE)] * 3,
        grid=(bsz,),
        in_specs=[pl.BlockSpec(memory_space=pltpu.SMEM),
                  pl.BlockSpec((4, HEAD_DIM), lambda b: (0, 0)),
                  pl.BlockSpec((C_HEADS, 1, LANES), lambda b: (0, 0, 0)),
                  wide, nar, nar, nar, nar, wide, wide, wide, wide, wide, wide],
        out_specs=[wide] * 3,
        compiler_params=_params(("parallel",)),
        name="attn_context",
    )(sink, c_lambda, subln3, aq, ak, aks, av, avs, bq, bk, bv, cq, ck, cv)


def _merge_kernel(x_ref, gt_ref, ya_ref, yb_ref, yc_ref, ga_ref, gb_ref, gc_ref,
                  wa_ref, wb_ref, wc_ref, wo_ref, o_ref):
    m = (ga_ref[0].astype(F32) * _nn(ya_ref[0], wa_ref[...])
         + gb_ref[0].astype(F32) * _nn(yb_ref[0], wb_ref[...])
         + gc_ref[0].astype(F32) * _nn(yc_ref[0], wc_ref[...]))
    o_ref[0] = x_ref[0] + gt_ref[0] * _nn(m.astype(MXU_DTYPE), wo_ref[...])


def _merge(x, mod3, mod_row, ya, yb, yc, ga, gb, gc, wa, wb, wc, wo):
    bsz, n, d = x.shape
    tm = min(TM_PROJ, n)
    tok = lambda w: pl.BlockSpec((1, tm, w), lambda b, i: (b, i, 0))
    return pl.pallas_call(
        _merge_kernel,
        out_shape=jax.ShapeDtypeStruct(x.shape, F32),
        grid=(bsz, n // tm),
        in_specs=[tok(d), pl.BlockSpec((1, 1, d), lambda b, i: (mod_row(b), 0, 2)),
                  tok(512), tok(512), tok(512), tok(d), tok(d), tok(d),
                  _resident(wa.shape), _resident(wb.shape), _resident(wc.shape), _resident(wo.shape)],
        out_specs=tok(d),
        compiler_params=_params(("parallel", "parallel")),
        name="merge",
    )(x, mod3, ya, yb, yc, ga, gb, gc, wa, wb, wc, wo)


def _top_values(ref, row0, n_rows, count, out_ref, out_row0, floor):
    for k in range(count):
        cur = ref[row0:row0 + n_rows]
        m = jnp.max(cur, axis=0, keepdims=True)
        out_ref[out_row0 + k:out_row0 + k + 1] = m
        if k + 1 < count:
            ref[row0:row0 + n_rows] = jnp.where(cur == m, floor, cur)


def _batcher_network(n):
    pairs = []
    p = 1
    while p < n:
        k = p
        while k >= 1:
            for j in range(k % p, n - k, 2 * k):
                for i in range(min(k, n - j - k)):
                    if (i + j) // (2 * p) == (i + j + k) // (2 * p):
                        pairs.append((i + j, i + j + k))
            k //= 2
        p *= 2
    return tuple(pairs)


N_TOP = PEER_TOPK + 1
SUBLANES = 8
_SORT_NET = _batcher_network(PEER_KEYS // SUBLANES)
_CAND_COUNTS = tuple(N_TOP // (a + 1) for a in range(N_TOP))
_CAND_OFFS = tuple(int(v) for v in np.cumsum((0,) + _CAND_COUNTS))
N_CAND = -(-_CAND_OFFS[-1] // SUBLANES) * SUBLANES


def _top_sorted(s_ref, lanes, out_ref, slot):
    n_grp = PEER_KEYS // SUBLANES
    lists = [s_ref[r * SUBLANES:(r + 1) * SUBLANES, lanes] for r in range(n_grp)]
    for a, b in _SORT_NET:
        lists[a], lists[b] = jnp.maximum(lists[a], lists[b]), jnp.minimum(lists[a], lists[b])
    for t in range(N_TOP):
        head = lists[0]
        m = jnp.max(head, axis=0, keepdims=True)
        out_ref[slot, t:t + 1, lanes] = m
        remaining = N_TOP - 1 - t
        hit = head == m
        for k in range(remaining):
            below = lists[k + 1] if k + 1 < n_grp else -jnp.inf
            lists[k] = jnp.where(hit, below, lists[k])


def _route_kernel(x_ref, g_ref, sc_ref, sh_ref, wq_ref, keys_ref,
                  h_ref, e1_ref, e2_ref, thr_ref, s_ref, top_ref, cand_ref):
    tt = x_ref.shape[1]
    h2 = _rms_mod(x_ref[0], g_ref[...], sc_ref[0], sh_ref[0]).astype(MXU_DTYPE)
    h_ref[0] = h2
    q = _nn(h2, wq_ref[...]).astype(MXU_DTYPE)
    top_ref[:, N_TOP:, :] = jnp.full((3, top_ref.shape[1] - N_TOP, tt), -jnp.inf, F32)
    cand_ref[_CAND_OFFS[-1]:, :] = jnp.full((N_CAND - _CAND_OFFS[-1], tt), -1.0, F32)
    for h in range(PEER_HEADS):
        tops = []
        for p, e_ref in ((0, e1_ref), (1, e2_ref)):
            hp = 2 * h + p
            s_ref[...] = _nt(keys_ref[hp], q[:, hp * LANES:(hp + 1) * LANES])
            for tc in range(tt // LANES):
                _top_sorted(s_ref, slice(tc * LANES, (tc + 1) * LANES), top_ref, p)
            m = top_ref[p, 0:1]
            e_ref[0, h] = jnp.exp(s_ref[...] - m)
            tops.append(jnp.exp(top_ref[p] - m))
        e1_top, e2_top = tops
        for a in range(N_TOP):
            cand_ref[_CAND_OFFS[a]:_CAND_OFFS[a + 1]] = e1_top[a:a + 1] * e2_top[0:_CAND_COUNTS[a]]
        _top_values(cand_ref, 0, N_CAND, N_TOP, top_ref.at[2], 0, -1.0)
        best = top_ref[2]
        inv_z = 1.0 / jnp.sum(best[0:PEER_TOPK], axis=0, keepdims=True)
        e1_ref[0, h] = e1_ref[0, h] * inv_z
        thr_ref[0, h:h + 1] = 0.5 * (best[PEER_TOPK - 1:PEER_TOPK] + best[PEER_TOPK:N_TOP]) * inv_z


def _route(x, g2, mod3, mod_row, wq, keys):
    bsz, n, d = x.shape
    tt = min(TT_PEER, n)
    gate_shape = jax.ShapeDtypeStruct((bsz, PEER_HEADS, PEER_KEYS, n), F32)
    gate_spec = pl.BlockSpec((1, PEER_HEADS, PEER_KEYS, tt), lambda b, i: (b, 0, 0, i))
    return pl.pallas_call(
        _route_kernel,
        out_shape=[jax.ShapeDtypeStruct((bsz, n, d), MXU_DTYPE), gate_shape, gate_shape,
                   jax.ShapeDtypeStruct((bsz, PEER_HEADS, n), F32)],
        grid=(bsz, n // tt),
        in_specs=[pl.BlockSpec((1, tt, d), lambda b, i: (b, i, 0)),
                  pl.BlockSpec((1, d), lambda b, i: (0, 0)),
                  pl.BlockSpec((1, 1, d), lambda b, i: (mod_row(b), 0, 4)),
                  pl.BlockSpec((1, 1, d), lambda b, i: (mod_row(b), 0, 3)),
                  _resident(wq.shape), _resident(keys.shape)],
        out_specs=[pl.BlockSpec((1, tt, d), lambda b, i: (b, i, 0)), gate_spec, gate_spec,
                   pl.BlockSpec((1, PEER_HEADS, tt), lambda b, i: (b, 0, i))],
        scratch_shapes=[pltpu.VMEM((PEER_KEYS, tt), F32),
                        pltpu.VMEM((3, 3 * SUBLANES, tt), F32),
                        pltpu.VMEM((N_CAND, tt), F32)],
        compiler_params=_params(("parallel", "parallel")),
        name="peer_route",
    )(x, g2, mod3, mod3, wq, keys)


def _expert_kernel(x_ref, gt_ref, gf_ref, h_ref, u_ref, v_ref, e1_ref, e2_ref, thr_ref, o_ref,
                   acc_ref, a_ref, w_ref, *, final_norm):
    eb = pl.program_id(2)
    tt = h_ref.shape[1]

    @pl.when(eb == 0)
    def _zero():
        acc_ref[...] = jnp.zeros_like(acc_ref)

    a_ref[...] = _nt(u_ref[...], h_ref[0])
    sub = 64
    n_i = TE_PEER // PEER_KEYS
    i0 = pl.multiple_of(eb * n_i, n_i)
    for il in range(n_i):
        for tc in range(tt // LANES):
            lanes = slice(tc * LANES, (tc + 1) * LANES)
            e1_rows = [e1_ref[0, h, pl.ds(i0, n_i), lanes][il:il + 1] for h in range(PEER_HEADS)]
            thr_rows = [thr_ref[0, h:h + 1, lanes] for h in range(PEER_HEADS)]
            for js in range(PEER_KEYS // sub):
                rows = slice(js * sub, (js + 1) * sub)
                gate = None
                for h in range(PEER_HEADS):
                    prod = e2_ref[0, h, rows, lanes] * e1_rows[h]
                    sel = jnp.where(prod >= thr_rows[h], prod, 0.0)
                    gate = sel if gate is None else gate + sel
                erows = slice(il * PEER_KEYS + js * sub, il * PEER_KEYS + (js + 1) * sub)
                a = a_ref[erows, lanes]
                act = (0.5 * a) * (1.0 + lax.erf(a * math.sqrt(0.5)))
                w_ref[erows, lanes] = gate * act
    acc_ref[...] += _nn(w_ref[...].T.astype(MXU_DTYPE), v_ref[...])

    @pl.when(eb == pl.num_programs(2) - 1)
    def _finish():
        y = x_ref[0] + gt_ref[0] * acc_ref[...]
        if final_norm:
            y = (y * lax.rsqrt(jnp.mean(y * y, axis=-1, keepdims=True) + NORM_EPS)) * gf_ref[...]
        o_ref[0] = y


def _experts(x, mod3, mod_row, g_final, h2, u, v, e1, e2, thr, final_norm):
    bsz, n, d = x.shape
    tt = min(TT_PEER, n)
    n_exp = u.shape[0]
    gate_spec = pl.BlockSpec((1, PEER_HEADS, PEER_KEYS, tt), lambda b, i, e: (b, 0, 0, i))
    return pl.pallas_call(
        functools.partial(_expert_kernel, final_norm=final_norm),
        out_shape=jax.ShapeDtypeStruct(x.shape, F32),
        grid=(bsz, n // tt, n_exp // TE_PEER),
        in_specs=[pl.BlockSpec((1, tt, d), lambda b, i, e: (b, i, 0)),
                  pl.BlockSpec((1, 1, d), lambda b, i, e: (mod_row(b), 0, 5)),
                  pl.BlockSpec((1, d), lambda b, i, e: (0, 0)),
                  pl.BlockSpec((1, tt, d), lambda b, i, e: (b, i, 0)),
                  pl.BlockSpec((TE_PEER, d), lambda b, i, e: (e, 0)),
                  pl.BlockSpec((TE_PEER, d), lambda b, i, e: (e, 0)),
                  gate_spec, gate_spec,
                  pl.BlockSpec((1, PEER_HEADS, tt), lambda b, i, e: (b, 0, i))],
        out_specs=pl.BlockSpec((1, tt, d), lambda b, i, e: (b, i, 0)),
        scratch_shapes=[pltpu.VMEM((tt, d), F32),
                        pltpu.VMEM((TE_PEER, tt), F32),
                        pltpu.VMEM((TE_PEER, tt), F32)],
        compiler_params=_params(("parallel", "parallel", "arbitrary")),
        name="peer_experts",
    )(x, mod3, g_final, h2, u, v, e1, e2, thr)


def _rope_tables(n):
    t = jnp.arange(n, dtype=jnp.int32)
    row = (t // GRID_W).astype(F32)
    col = (t % GRID_W).astype(F32)
    n_freq = HEAD_DIM // 4
    inv = ROPE_THETA ** (-jnp.arange(n_freq, dtype=F32) / n_freq)
    ang = jnp.concatenate([row[:, None] * inv, col[:, None] * inv], axis=-1)
    cos, sin = jnp.cos(ang), jnp.sin(ang)
    return jnp.tile(cos, (1, 4)), jnp.tile(jnp.concatenate([-sin, sin], axis=-1), (1, 2))


def kernel(x, c, ctx, c_ctx, w_mod, b_mod, g_norm1, g_norm2, w_in, a_sink, b_rpb, c_lambda,
           c_subln, w_branch_a, w_branch_b, w_branch_c, w_out, peer_wq, peer_keys, peer_u,
           peer_v, g_final):
    bsz, n, d = x.shape
    depth = w_mod.shape[0]
    n_ctx = ctx.shape[1]
    assert d == D_MODEL and bsz + 1 <= 8
    assert n % (B_ROWS * GRID_W) == 0 and n // GRID_W >= 4 * B_KROWS
    assert n % (A_QBLOCKS * A_BLOCK) == 0 and (n + n_ctx) % LANES == 0

    cvec = jnp.zeros((8, d), F32).at[:bsz].set(c).at[bsz].set(c_ctx)
    mod = _modulation(cvec, w_mod, b_mod)
    lat_row = lambda b: b
    ctx_row = lambda b: bsz
    cos_t, sin_t = _rope_tables(n)
    ones_t, zeros_t = jnp.ones((n_ctx, LANES), F32), jnp.zeros((n_ctx, LANES), F32)
    cast = lambda w: w.astype(MXU_DTYPE)
    gfin = g_final.reshape(1, d)

    xc = ctx
    for l in range(depth):
        last = l == depth - 1
        lam_init = 0.8 - 0.6 * math.exp(-0.3 * l)
        mod3 = mod[l].reshape(8, 1, 6 * d)
        g1, g2 = g_norm1[l].reshape(1, d), g_norm2[l].reshape(1, d)
        w_in_l = cast(w_in[l])
        subln3 = c_subln[l].reshape(C_HEADS, 1, LANES)
        wa, wb, wc, wo = cast(w_branch_a[l]), cast(w_branch_b[l]), cast(w_branch_c[l]), cast(w_out[l])
        wq = cast(peer_wq[l])
        keys = cast(peer_keys[l].reshape(2 * PEER_HEADS, PEER_KEYS, LANES))
        u, v = cast(peer_u[l]), cast(peer_v[l])

        lat = _in_proj(x, g1, mod3, lat_row, w_in_l, cos_t, sin_t)
        con = _in_proj(xc, g1, mod3, ctx_row, w_in_l, ones_t, zeros_t)
        (aq, ak, aks, av, avs, bq, bk, bv, cq, ck, _, ga, gb, gc, cvt) = lat
        ya = _attn_a(a_sink[l], aq, ak, aks, av, avs, con[1], con[2], con[3], con[4])
        yb = _attn_b(b_rpb[l], bq, bk, bv, con[6], con[7])
        yc = _attn_c(c_lambda[l], subln3, cq, ck, cvt, con[9], con[14], lam_init)
        x = _merge(x, mod3, lat_row, ya, yb, yc, ga, gb, gc, wa, wb, wc, wo)
        h2, e1, e2, thr = _route(x, g2, mod3, lat_row, wq, keys)
        x = _experts(x, mod3, lat_row, gfin, h2, u, v, e1, e2, thr, final_norm=last)

        if not last:
            ya_c, yb_c, yc_c = _ctx_attn(a_sink[l], c_lambda[l], subln3, con, lam_init)
            xc = _merge(xc, mod3, ctx_row, ya_c, yb_c, yc_c, con[11], con[12], con[13],
                        wa, wb, wc, wo)
            h2c, e1c, e2c, thrc = _route(xc, g2, mod3, ctx_row, wq, keys)
            xc = _experts(xc, mod3, ctx_row, gfin, h2c, u, v, e1c, e2c, thrc, final_norm=False)
    return x
```

```python
import functools
import math

import jax
import jax.numpy as jnp
import numpy as np
from jax import lax
from jax.experimental import pallas as pl
from jax.experimental.pallas import tpu as pltpu

F32 = jnp.float32
BF16 = jnp.bfloat16
MXU_DTYPE = BF16

LANES = 128
MXU_WIDTH = 256
VMEM_LIMIT_BYTES = 56 * 1024 * 1024

HEAD_DIM = 64
GRID_W = 64
ROPE_THETA = 10000.0
NORM_EPS = 1e-6
NEG_INF = -1e30
LOG2_E = math.log2(math.e)
SQRT_HALF = math.sqrt(0.5)
A_HEADS, A_KV_HEADS, A_WINDOW, A_BLOCK = 8, 2, 128, 128
B_HEADS, NA_ROWS, NA_COLS = 8, 8, 16
C_HEADS = 4
PEER_HEADS, PEER_KEYS, PEER_TOPK = 8, 128, 16
D_MODEL = 1024
IN_SPLITS = (512, 128, 128, 512, 512, 512, 512, 512, 512, 1024, 1024, 1024)
IN_OFFS = tuple(int(v) for v in np.cumsum((0,) + IN_SPLITS))

TM_PROJ = 512
A_QBLOCKS = 4
TQ_C = 1024
TK_C = 768
B_ROWS = 8
B_KROWS = 4
TT_PEER = 512
TE_PEER = 1024
N_PT = 30


def _nt(a, b):
    return lax.dot_general(a, b, (((1,), (1,)), ((), ())), preferred_element_type=F32)


def _nn(a, b):
    return jnp.dot(a, b, preferred_element_type=F32)


def _params(sem):
    return pltpu.CompilerParams(dimension_semantics=sem, vmem_limit_bytes=VMEM_LIMIT_BYTES)


def _resident(shape):
    nd = len(shape)
    return pl.BlockSpec(shape, lambda *_: (0,) * nd, pipeline_mode=pl.Buffered(1))


def _mod_kernel(c_ref, w_ref, b_ref, o_ref):
    c = c_ref[...]
    s = c * jax.nn.sigmoid(c)
    w = w_ref[0]
    s_hi = s.astype(MXU_DTYPE)
    s_lo = (s - s_hi.astype(F32)).astype(MXU_DTYPE)
    w_hi = w.astype(MXU_DTYPE)
    w_lo = (w - w_hi.astype(F32)).astype(MXU_DTYPE)
    acc = _nn(s_hi, w_hi) + _nn(s_lo, w_hi) + _nn(s_hi, w_lo)
    o_ref[0] = acc + b_ref[0]


def _modulation(cvec, w_mod, b_mod):
    depth, d, n6 = w_mod.shape
    tn = 1536
    return pl.pallas_call(
        _mod_kernel,
        out_shape=jax.ShapeDtypeStruct((depth, 8, n6), F32),
        grid=(depth, n6 // tn),
        in_specs=[pl.BlockSpec((8, d), lambda l, j: (0, 0)),
                  pl.BlockSpec((1, d, tn), lambda l, j: (l, 0, j)),
                  pl.BlockSpec((1, 1, tn), lambda l, j: (l, 0, j))],
        out_specs=pl.BlockSpec((1, 8, tn), lambda l, j: (l, 0, j)),
        compiler_params=_params(("parallel", "parallel")),
        name="modulation",
    )(cvec, w_mod, b_mod.reshape(depth, 1, n6))


def _rms_mod(x, g, sc, sh):
    y = x * lax.rsqrt(jnp.mean(x * x, axis=-1, keepdims=True) + NORM_EPS)
    return (y * g) * (1.0 + sc) + sh


def _in_proj_kernel(x_ref, g_ref, sc_ref, sh_ref, w_ref, cos_ref, sin_ref,
                    aq_ref, ak_ref, aks_ref, av_ref, avs_ref, bq_ref, bk_ref, bv_ref,
                    cq_ref, ck_ref, cv_ref, ga_ref, gb_ref, gc_ref, cvt_ref):
    h = _rms_mod(x_ref[0], g_ref[...], sc_ref[0], sh_ref[0]).astype(MXU_DTYPE)
    cos = cos_ref[...]
    sin = sin_ref[...]
    lane = lax.broadcasted_iota(jnp.int32, (1, LANES), 1)
    first_half = (lane % HEAD_DIM) < (HEAD_DIM // 2)
    scale = HEAD_DIM ** -0.5

    def proj(seg, jp):
        c0 = IN_OFFS[seg] + jp * MXU_WIDTH
        t = _nn(h, w_ref[:, c0:c0 + MXU_WIDTH])
        return t[:, :LANES], t[:, LANES:]

    def rope(v):
        rot = jnp.where(first_half, pltpu.roll(v, LANES - 32, 1), pltpu.roll(v, 32, 1))
        return v * cos + rot * sin

    def emit(ref, seg, fn=lambda v: v):
        for jp in range(IN_SPLITS[seg] // MXU_WIDTH):
            for k, part in enumerate(proj(seg, jp)):
                j = 2 * jp + k
                ref[0, :, j * LANES:(j + 1) * LANES] = fn(part).astype(ref.dtype)

    emit(aq_ref, 0, lambda v: rope(v) * scale)
    emit(bq_ref, 3, lambda v: v * scale)
    emit(bk_ref, 4)
    emit(bv_ref, 5)
    emit(cq_ref, 6, lambda v: rope(v) * (scale * LOG2_E))
    emit(ck_ref, 7, rope)
    for jp in range(IN_SPLITS[8] // MXU_WIDTH):
        for k, cv in enumerate(proj(8, jp)):
            j = 2 * jp + k
            cv_ref[0, :, j * LANES:(j + 1) * LANES] = cv.astype(cv_ref.dtype)
            cvt_ref[0, j * LANES:(j + 1) * LANES, :] = cv.T.astype(cvt_ref.dtype)
    ak, av = proj(1, 0)
    ak = rope(ak)
    ak_ref[0] = ak.astype(ak_ref.dtype)
    aks_ref[0] = pltpu.roll(ak, HEAD_DIM, 1).astype(aks_ref.dtype)
    av_ref[0] = av.astype(av_ref.dtype)
    avs_ref[0] = pltpu.roll(av, HEAD_DIM, 1).astype(avs_ref.dtype)
    emit(ga_ref, 9, jax.nn.sigmoid)
    emit(gb_ref, 10, jax.nn.sigmoid)
    emit(gc_ref, 11, jax.nn.sigmoid)


def _in_proj(x, g1, mod3, mod_row, w_in, cos_t, sin_t):
    bsz, n, d = x.shape
    tm = min(TM_PROJ, n)
    tok = lambda w: pl.BlockSpec((1, tm, w), lambda b, i: (b, i, 0))
    widths = (512, 128, 128, 128, 128, 512, 512, 512, 512, 512, 512, 1024, 1024, 1024)
    return pl.pallas_call(
        _in_proj_kernel,
        out_shape=[jax.ShapeDtypeStruct((bsz, n, w), MXU_DTYPE) for w in widths]
        + [jax.ShapeDtypeStruct((bsz, 512, n), MXU_DTYPE)],
        grid=(bsz, n // tm),
        in_specs=[tok(d),
                  pl.BlockSpec((1, d), lambda b, i: (0, 0)),
                  pl.BlockSpec((1, 1, d), lambda b, i: (mod_row(b), 0, 1)),
                  pl.BlockSpec((1, 1, d), lambda b, i: (mod_row(b), 0, 0)),
                  _resident(w_in.shape),
                  pl.BlockSpec((tm, LANES), lambda b, i: (i, 0)),
                  pl.BlockSpec((tm, LANES), lambda b, i: (i, 0))],
        out_specs=[tok(w) for w in widths] + [pl.BlockSpec((1, 512, tm), lambda b, i: (b, 0, i))],
        compiler_params=_params(("parallel", "parallel")),
        name="in_proj",
    )(x, g1, mod3, mod3, w_in, cos_t, sin_t)


def _lane_lo():
    return lax.broadcasted_iota(jnp.int32, (1, LANES), 1) < HEAD_DIM


def _softmax_pv(s_list, v_list, extra_logit=None, exp=jnp.exp):
    m = functools.reduce(jnp.maximum, [jnp.max(s, axis=-1, keepdims=True) for s in s_list])
    if extra_logit is not None:
        m = jnp.maximum(m, extra_logit)
    l = 0.0 if extra_logit is None else exp(extra_logit - m)
    o = 0.0
    for s, v in zip(s_list, v_list):
        p = exp(s - m)
        l = l + jnp.sum(p, axis=-1, keepdims=True)
        o = o + _nn(p.astype(v.dtype), v)
    return o / l


def _attn_a_kernel(sink_ref, q_ref, kp_ref, kc_ref, kn_ref, ksp_ref, ksc_ref, ksn_ref,
                   vp_ref, vc_ref, vn_ref, vsp_ref, vsc_ref, vsn_ref,
                   xk_ref, xks_ref, xv_ref, xvs_ref, o_ref, *, n_tokens):
    i = pl.program_id(1)
    lo = _lane_lo()
    cat = lambda refs: jnp.concatenate([r[0] for r in refs], axis=0)
    k_loc, ks_loc = cat((kp_ref, kc_ref, kn_ref)), cat((ksp_ref, ksc_ref, ksn_ref))
    v_loc, vs_loc = cat((vp_ref, vc_ref, vn_ref)), cat((vsp_ref, vsc_ref, vsn_ref))
    zero = jnp.zeros((), k_loc.dtype)
    k_of = {(0, 0): (jnp.where(lo, k_loc, zero), jnp.where(lo, xk_ref[0], zero)),
            (0, 1): (jnp.where(lo, zero, ks_loc), jnp.where(lo, zero, xks_ref[0])),
            (1, 0): (jnp.where(lo, ks_loc, zero), jnp.where(lo, xks_ref[0], zero)),
            (1, 1): (jnp.where(lo, zero, k_loc), jnp.where(lo, zero, xk_ref[0]))}
    v_of = {(0, 0): (v_loc, xv_ref[0]), (0, 1): (vs_loc, xvs_ref[0]),
            (1, 0): (vs_loc, xvs_ref[0]), (1, 1): (v_loc, xv_ref[0])}
    chains = []
    for qb in range(A_QBLOCKS):
        blk = i * A_QBLOCKS + qb
        qpos = blk * A_BLOCK + lax.broadcasted_iota(jnp.int32, (A_BLOCK, 1), 0)
        kpos = (blk - 1) * A_BLOCK + lax.broadcasted_iota(jnp.int32, (1, 3 * A_BLOCK), 1)
        dist = qpos - kpos
        valid = (jnp.maximum(dist, -dist) <= A_WINDOW) & (kpos >= 0) & (kpos < n_tokens)
        qrows = slice(qb * A_BLOCK, (qb + 1) * A_BLOCK)
        krows = slice(qb * A_BLOCK, (qb + 3) * A_BLOCK)
        for hp in range(A_HEADS // 2):
            qp = q_ref[0, qrows, hp * LANES:(hp + 1) * LANES]
            for half in range(2):
                h = 2 * hp + half
                g = h // (A_HEADS // A_KV_HEADS)
                (kl, kx), (vl, vx) = k_of[(g, half)], v_of[(g, half)]
                s_loc = jnp.where(valid, _nt(qp, kl[krows]), NEG_INF)
                chains.append(([s_loc, _nt(qp, kx)], [vl[krows], vx], sink_ref[h]))
    outs = [_softmax_pv(s, v, extra_logit=sink) for s, v, sink in chains]
    for qb in range(A_QBLOCKS):
        qrows = slice(qb * A_BLOCK, (qb + 1) * A_BLOCK)
        for hp in range(A_HEADS // 2):
            even, odd = outs[(qb * (A_HEADS // 2) + hp) * 2:(qb * (A_HEADS // 2) + hp) * 2 + 2]
            o_ref[0, qrows, hp * LANES:(hp + 1) * LANES] = jnp.where(lo, even, odd).astype(o_ref.dtype)


def _attn_a(sink, aq, ak, aks, av, avs, xak, xaks, xav, xavs):
    bsz, n, _ = aq.shape
    nb = n // A_BLOCK
    tq = A_QBLOCKS * A_BLOCK
    ctx = xak.shape[1]
    prev = pl.BlockSpec((1, A_BLOCK, LANES),
                        lambda b, i: (b, jnp.maximum(i * A_QBLOCKS - 1, 0), 0))
    cur = pl.BlockSpec((1, tq, LANES), lambda b, i: (b, i, 0))
    nxt = pl.BlockSpec((1, A_BLOCK, LANES),
                       lambda b, i: (b, jnp.minimum((i + 1) * A_QBLOCKS, nb - 1), 0))
    cx = pl.BlockSpec((1, ctx, LANES), lambda b, i: (b, 0, 0))
    return pl.pallas_call(
        functools.partial(_attn_a_kernel, n_tokens=n),
        out_shape=jax.ShapeDtypeStruct(aq.shape, MXU_DTYPE),
        grid=(bsz, n // tq),
        in_specs=[pl.BlockSpec(memory_space=pltpu.SMEM),
                  pl.BlockSpec((1, tq, 512), lambda b, i: (b, i, 0)),
                  prev, cur, nxt, prev, cur, nxt, prev, cur, nxt, prev, cur, nxt, cx, cx, cx, cx],
        out_specs=pl.BlockSpec((1, tq, 512), lambda b, i: (b, i, 0)),
        compiler_params=_params(("parallel", "parallel")),
        name="attn_window",
    )(sink, aq, ak, ak, ak, aks, aks, aks, av, av, av, avs, avs, avs, xak, xaks, xav, xavs)


def _attn_b_kernel(rpb_ref, q_ref, k0_ref, k1_ref, k2_ref, k3_ref, v0_ref, v1_ref, v2_ref, v3_ref,
                   xk_ref, xv_ref, o_ref, pt_ref, *, n_rows):
    pair, i = pl.program_id(0), pl.program_id(2)
    n_a = 2 * NA_ROWS - 1
    n_b = 2 * NA_COLS - 1

    @pl.when((pl.program_id(1) == 0) & (i == 0))
    def _build_bias_tables():
        qc = lax.broadcasted_iota(jnp.int32, (GRID_W, LANES), 0)
        ln = lax.broadcasted_iota(jnp.int32, (GRID_W, LANES), 1)
        kc = ln % GRID_W
        hi = ln >= GRID_W
        cstart = jnp.clip(qc - NA_COLS // 2, 0, GRID_W - NA_COLS)
        col_ok = (kc >= cstart) & (kc < cstart + NA_COLS)
        d = kc - qc + (NA_COLS - 1)
        for hh in range(2):
            h = pair * 2 + hh

            def body(ai, carry):
                a = ai - 8
                a_ok, a1_ok = (a >= 0) & (a < n_a), (a + 1 >= 0) & (a + 1 < n_a)
                ra, ra1 = h * n_a + jnp.clip(a, 0, n_a - 1), h * n_a + jnp.clip(a + 1, 0, n_a - 1)
                t = jnp.full((GRID_W, LANES), NEG_INF, F32)
                for b in range(n_b):
                    va = jnp.where(a_ok, rpb_ref[ra, b], NEG_INF)
                    va1 = jnp.where(a1_ok, rpb_ref[ra1, b], NEG_INF)
                    t = jnp.where(d == b, jnp.where(hi, va1, va), t)
                pt_ref[hh, ai] = jnp.where(col_ok, t, NEG_INF)
                return carry

            lax.fori_loop(0, N_PT, body, 0)

    lo = _lane_lo()
    r0 = i * B_ROWS
    kb0 = jnp.clip(2 * i - 1, 0, n_rows // B_KROWS - 4)
    krow_lane = lax.broadcasted_iota(jnp.int32, (1, B_KROWS * GRID_W), 1) // GRID_W
    k_refs, v_refs = (k0_ref, k1_ref, k2_ref, k3_ref), (v0_ref, v1_ref, v2_ref, v3_ref)
    qp = q_ref[0]
    zero = jnp.zeros((), qp.dtype)
    raw = []
    for hh in range(2):
        keep = lambda t: jnp.where(lo, t, zero) if hh == 0 else jnp.where(lo, zero, t)
        raw.append([_nt(qp, keep(k_refs[j][0, 0])) for j in range(4)]
                   + [_nt(qp, keep(xk_ref[0]))])
    outs = []
    for hh in range(2):
        s_tiles = []
        for j in range(4):
            s = raw[hh][j]
            kr0 = (kb0 + j) * B_KROWS
            rows = []
            for qr in range(B_ROWS):
                r = r0 + qr
                rs = jnp.clip(r - NA_ROWS // 2, 0, n_rows - NA_ROWS)
                a0 = kr0 - r + (NA_ROWS - 1)
                bias = jnp.concatenate([pt_ref[hh, a0 + 8], pt_ref[hh, a0 + 10]], axis=1)
                row_ok = (kr0 + krow_lane >= rs) & (kr0 + krow_lane < rs + NA_ROWS)
                rows.append(jnp.where(row_ok, s[qr * GRID_W:(qr + 1) * GRID_W] + bias, NEG_INF))
            s_tiles.append(jnp.concatenate(rows, axis=0))
        s_tiles.append(raw[hh][4])
        outs.append(_softmax_pv(s_tiles, [v_refs[j][0, 0] for j in range(4)] + [xv_ref[0]]))
    o_ref[0] = jnp.where(lo, outs[0], outs[1]).astype(o_ref.dtype)


def _attn_b(rpb, bq, bk, bv, xbk, xbv):
    bsz, n, _ = bq.shape
    n_rows = n // GRID_W
    tq = B_ROWS * GRID_W
    tkb = B_KROWS * GRID_W
    ctx = xbk.shape[1]
    bk4 = bk.reshape(bsz, n // tkb, tkb, 512)
    bv4 = bv.reshape(bsz, n // tkb, tkb, 512)
    nkb = n // tkb

    def kspec(j):
        return pl.BlockSpec((1, 1, tkb, LANES),
                            lambda p, b, i: (b, jnp.clip(2 * i - 1, 0, nkb - 4) + j, 0, p))

    cx = pl.BlockSpec((1, ctx, LANES), lambda p, b, i: (b, 0, p))
    return pl.pallas_call(
        functools.partial(_attn_b_kernel, n_rows=n_rows),
        out_shape=jax.ShapeDtypeStruct(bq.shape, MXU_DTYPE),
        grid=(B_HEADS // 2, bsz, n // tq),
        in_specs=[pl.BlockSpec(memory_space=pltpu.SMEM),
                  pl.BlockSpec((1, tq, LANES), lambda p, b, i: (b, i, p)),
                  kspec(0), kspec(1), kspec(2), kspec(3), kspec(0), kspec(1), kspec(2), kspec(3),
                  cx, cx],
        out_specs=pl.BlockSpec((1, tq, LANES), lambda p, b, i: (b, i, p)),
        scratch_shapes=[pltpu.VMEM((2, N_PT, GRID_W, LANES), F32)],
        compiler_params=_params(("arbitrary", "arbitrary", "arbitrary")),
        name="attn_neighbourhood",
    )(rpb.reshape(B_HEADS * (2 * NA_ROWS - 1), 2 * NA_COLS - 1), bq,
      bk4, bk4, bk4, bk4, bv4, bv4, bv4, bv4, xbk, xbv)


def _diff_lambda(lam_ref, lam_init):
    lam = lam_ref[...]
    a = jnp.sum(lam[0:1] * lam[1:2], axis=-1, keepdims=True)
    b = jnp.sum(lam[2:3] * lam[3:4], axis=-1, keepdims=True)
    return jnp.exp(a) - jnp.exp(b) + lam_init


def _head_norm(o, g, lam_init):
    y = o * lax.rsqrt(jnp.mean(o * o, axis=-1, keepdims=True) + NORM_EPS)
    return (y * g) * (1.0 - lam_init)


def _attn_c_kernel(lam_ref, g_ref, q_ref, k_ref, vt_ref, xk_ref, xvt_ref, o_ref,
                   k1_ref, k2_ref, vall_ref, s_ref, p_ref, stat_ref, acc_ref,
                   *, lam_init, n_tokens, tk, n_chunks):
    lo = _lane_lo()

    @pl.when(pl.program_id(2) == 0)
    def _split_keys():
        zero = jnp.zeros((), k1_ref.dtype)
        k1_ref[0:n_tokens] = jnp.where(lo, k_ref[0], zero)
        k2_ref[0:n_tokens] = jnp.where(lo, zero, k_ref[0])
        k1_ref[n_tokens:] = jnp.where(lo, xk_ref[0], zero)
        k2_ref[n_tokens:] = jnp.where(lo, zero, xk_ref[0])

        vall_ref[:, 0:n_tokens] = vt_ref[0]
        vall_ref[:, n_tokens:] = xvt_ref[0]

    q = q_ref[0]
    tq = q.shape[0]
    key_refs = (k1_ref, k2_ref)
    n = n_chunks
    j = pl.program_id(2)
    n_q = pl.num_programs(2) - 1
    bp = j % 2

    slot_of = lambda c: 2 if c == 0 else c % 2

    def reset(par):
        for c in range(2):
            stat_ref[2 * par + c, 0:1, :] = jnp.full((1, tq), -jnp.inf, F32)
            stat_ref[2 * par + c, 1:2, :] = jnp.zeros((1, tq), F32)
            acc_ref[2 * par + c] = jnp.zeros(acc_ref.shape[1:], F32)

    def scores(par, t, slot):
        off = pl.multiple_of(t * tk, LANES)
        for c in range(2):
            s = _nt(key_refs[c][pl.ds(off, tk), :], q)
            s_ref[slot, c] = s
            stat_ref[2 * par + c, 5 + slot:6 + slot, :] = jnp.max(s, axis=0, keepdims=True)

    def softmax(par, slot):
        for c in range(2):
            st = stat_ref.at[2 * par + c]
            s = s_ref[slot, c]
            m_old = st[0:1, :]
            m_new = jnp.maximum(m_old, st[5 + slot:6 + slot, :])
            alpha = jnp.exp2(m_old - m_new)
            p = jnp.exp2(s - m_new)
            p_ref[slot, c] = p.astype(p_ref.dtype)
            st[0:1, :] = m_new
            st[1:2, :] = alpha * st[1:2, :] + jnp.sum(p, axis=0, keepdims=True)
            st[2 + slot:3 + slot, :] = alpha

    def values(par, t, slot):
        off = pl.multiple_of(t * tk, LANES)
        vt = vall_ref[:, pl.ds(off, tk)]
        for c in range(2):
            k = 2 * par + c
            acc_ref[k] = stat_ref[k, 2 + slot:3 + slot, :] * acc_ref[k] + _nn(vt, p_ref[slot, c])

    def finish(par):
        o1 = acc_ref[2 * par] * (1.0 / stat_ref[2 * par, 1:2, :])
        o2 = acc_ref[2 * par + 1] * (1.0 / stat_ref[2 * par + 1, 1:2, :])
        ot = o1 - _diff_lambda(lam_ref, lam_init) * o2
        o_ref[0] = _head_norm(ot.T, g_ref[0], lam_init).astype(o_ref.dtype)

    @pl.when(j == 0)
    def _first_fill():
        reset(bp)
        scores(bp, 0, slot_of(0))
        scores(bp, 1, slot_of(1))
        softmax(bp, slot_of(0))

    @pl.when((j > 0) & (j < n_q))
    def _fill_and_drain():
        reset(bp)
        scores(bp, 0, slot_of(0))
        values(1 - bp, n - 2, slot_of(n - 2))
        softmax(1 - bp, slot_of(n - 1))
        scores(bp, 1, slot_of(1))
        values(1 - bp, n - 1, slot_of(n - 1))
        softmax(bp, slot_of(0))
        finish(1 - bp)

    @pl.when(j == n_q)
    def _last_drain():
        values(1 - bp, n - 2, slot_of(n - 2))
        softmax(1 - bp, slot_of(n - 1))
        values(1 - bp, n - 1, slot_of(n - 1))
        finish(1 - bp)

    @pl.when(j < n_q)
    def _steady():
        def iteration(t, t_static_slot):
            scores(bp, t, t_static_slot)
            values(bp, t - 2, slot_of(t - 2) if isinstance(t, int) else t_static_slot)
            softmax(bp, 1 - t_static_slot if not isinstance(t, int) or t - 1 > 0 else slot_of(0))

        if n > 2:
            iteration(2, slot_of(2))

        def pair(k, carry):
            iteration(3 + 2 * k, 1)
            iteration(4 + 2 * k, 0)
            return carry

        n_pairs = max(n - 3, 0) // 2
        lax.fori_loop(0, n_pairs, pair, 0)
        for t in range(3 + 2 * n_pairs, n):
            iteration(t, slot_of(t))


def _key_chunk(n_keys):
    return max(c for c in range(LANES, TK_C + 1, LANES) if n_keys % c == 0)


def _attn_c(c_lambda, subln3, cq, ck, cvt, xck, xcvt, lam_init):
    bsz, n, _ = cq.shape
    ctx = xck.shape[1]
    tq = min(TQ_C, n)
    tk = _key_chunk(n + ctx)
    return pl.pallas_call(
        functools.partial(_attn_c_kernel, lam_init=lam_init, n_tokens=n, tk=tk,
                          n_chunks=(n + ctx) // tk),
        out_shape=jax.ShapeDtypeStruct(cq.shape, MXU_DTYPE),
        grid=(bsz, C_HEADS, n // tq + 1),
        in_specs=[pl.BlockSpec((4, HEAD_DIM), lambda b, h, i: (0, 0)),
                  pl.BlockSpec((1, 1, LANES), lambda b, h, i: (h, 0, 0)),
                  pl.BlockSpec((1, tq, LANES), lambda b, h, i: (b, jnp.minimum(i, n // tq - 1), h)),
                  pl.BlockSpec((1, n, LANES), lambda b, h, i: (b, 0, h)),
                  pl.BlockSpec((1, LANES, n), lambda b, h, i: (b, h, 0)),
                  pl.BlockSpec((1, ctx, LANES), lambda b, h, i: (b, 0, h)),
                  pl.BlockSpec((1, LANES, ctx), lambda b, h, i: (b, h, 0))],
        out_specs=pl.BlockSpec((1, tq, LANES), lambda b, h, i: (b, jnp.maximum(i - 1, 0), h)),
        scratch_shapes=[pltpu.VMEM((n + ctx, LANES), MXU_DTYPE),
                        pltpu.VMEM((n + ctx, LANES), MXU_DTYPE),
                        pltpu.VMEM((LANES, n + ctx), MXU_DTYPE),
                        pltpu.VMEM((3, 2, tk, tq), F32),
                        pltpu.VMEM((3, 2, tk, tq), MXU_DTYPE),
                        pltpu.VMEM((4, 8, tq), F32),
                        pltpu.VMEM((4, LANES, tq), F32)],
        compiler_params=_params(("arbitrary", "arbitrary", "arbitrary")),
        name="attn_differential",
    )(c_lambda, subln3, cq, ck, cvt, xck, xcvt)


def _ctx_attn_kernel(sink_ref, lam_ref, g_ref, aq_ref, ak_ref, aks_ref, av_ref, avs_ref,
                     bq_ref, bk_ref, bv_ref, cq_ref, ck_ref, cv_ref, ya_ref, yb_ref, yc_ref,
                     *, lam_init):
    lo = _lane_lo()
    zero = jnp.zeros((), ak_ref.dtype)
    sel = lambda t, half: jnp.where(lo, t, zero) if half == 0 else jnp.where(lo, zero, t)
    k_of = {(0, 0): sel(ak_ref[0], 0), (0, 1): sel(aks_ref[0], 1),
            (1, 0): sel(aks_ref[0], 0), (1, 1): sel(ak_ref[0], 1)}
    v_of = {(0, 0): av_ref[0], (0, 1): avs_ref[0], (1, 0): avs_ref[0], (1, 1): av_ref[0]}
    for hp in range(A_HEADS // 2):
        cols = slice(hp * LANES, (hp + 1) * LANES)
        qa, qb = aq_ref[0, :, cols], bq_ref[0, :, cols]
        kb, vb = bk_ref[0, :, cols], bv_ref[0, :, cols]
        oa, ob = [], []
        for half in range(2):
            h = 2 * hp + half
            g = h // (A_HEADS // A_KV_HEADS)
            oa.append(_softmax_pv([_nt(qa, k_of[(g, half)])], [v_of[(g, half)]],
                                  extra_logit=sink_ref[h]))
            ob.append(_softmax_pv([_nt(qb, sel(kb, half))], [vb]))
        ya_ref[0, :, cols] = jnp.where(lo, oa[0], oa[1]).astype(ya_ref.dtype)
        yb_ref[0, :, cols] = jnp.where(lo, ob[0], ob[1]).astype(yb_ref.dtype)
    lam = _diff_lambda(lam_ref, lam_init)
    for h in range(C_HEADS):
        cols = slice(h * LANES, (h + 1) * LANES)
        q, k, v = cq_ref[0, :, cols], ck_ref[0, :, cols], cv_ref[0, :, cols]
        o = (_softmax_pv([_nt(q, sel(k, 0))], [v], exp=jnp.exp2)
             - lam * _softmax_pv([_nt(q, sel(k, 1))], [v], exp=jnp.exp2))
        yc_ref[0, :, cols] = _head_norm(o, g_ref[h], lam_init).astype(yc_ref.dtype)


def _ctx_attn(sink, c_lambda, subln3, con, lam_init):
    aq, ak, aks, av, avs, bq, bk, bv, cq, ck, cv = con[:11]
    bsz, ctx, _ = aq.shape
    wide = pl.BlockSpec((1, ctx, 512), lambda b: (b, 0, 0))
    nar = pl.BlockSpec((1, ctx, LANES), lambda b: (b, 0, 0))
    return pl.pallas_call(
        functools.partial(_ctx_attn_kernel, lam_init=lam_init),
        out_shape=[jax.ShapeDtypeStruct(aq.shape, MXU_DTYPE)] * 3,
        grid=(bsz,),
        in_specs=[pl.BlockSpec(memory_space=pltpu.SMEM),
                  pl.BlockSpec((4, HEAD_DIM), lambda b: (0, 0)),
                  pl.BlockSpec((C_HEADS, 1, LANES), lambda b: (0, 0, 0)),
                  wide, nar, nar, nar, nar, wide, wide, wide, wide, wide, wide],
        out_specs=[wide] * 3,
        compiler_params=_params(("parallel",)),
        name="attn_context",
    )(sink, c_lambda, subln3, aq, ak, aks, av, avs, bq, bk, bv, cq, ck, cv)


def _merge_kernel(x_ref, gt_ref, ya_ref, yb_ref, yc_ref, ga_ref, gb_ref, gc_ref,
                  wa_ref, wb_ref, wc_ref, wo_ref, o_ref):
    m = (ga_ref[0].astype(F32) * _nn(ya_ref[0], wa_ref[...])
         + gb_ref[0].astype(F32) * _nn(yb_ref[0], wb_ref[...])
         + gc_ref[0].astype(F32) * _nn(yc_ref[0], wc_ref[...]))
    o_ref[0] = x_ref[0] + gt_ref[0] * _nn(m.astype(MXU_DTYPE), wo_ref[...])


def _merge(x, mod3, mod_row, ya, yb, yc, ga, gb, gc, wa, wb, wc, wo):
    bsz, n, d = x.shape
    tm = min(TM_PROJ, n)
    tok = lambda w: pl.BlockSpec((1, tm, w), lambda b, i: (b, i, 0))
    return pl.pallas_call(
        _merge_kernel,
        out_shape=jax.ShapeDtypeStruct(x.shape, F32),
        grid=(bsz, n // tm),
        in_specs=[tok(d), pl.BlockSpec((1, 1, d), lambda b, i: (mod_row(b), 0, 2)),
                  tok(512), tok(512), tok(512), tok(d), tok(d), tok(d),
                  _resident(wa.shape), _resident(wb.shape), _resident(wc.shape), _resident(wo.shape)],
        out_specs=tok(d),
        compiler_params=_params(("parallel", "parallel")),
        name="merge",
    )(x, mod3, ya, yb, yc, ga, gb, gc, wa, wb, wc, wo)


def _top_values(ref, row0, n_rows, count, out_ref, out_row0, floor):
    for k in range(count):
        cur = ref[row0:row0 + n_rows]
        m = jnp.max(cur, axis=0, keepdims=True)
        out_ref[out_row0 + k:out_row0 + k + 1] = m
        if k + 1 < count:
            ref[row0:row0 + n_rows] = jnp.where(cur == m, floor, cur)


def _batcher_network(n):
    pairs = []
    p = 1
    while p < n:
        k = p
        while k >= 1:
            for j in range(k % p, n - k, 2 * k):
                for i in range(min(k, n - j - k)):
                    if (i + j) // (2 * p) == (i + j + k) // (2 * p):
                        pairs.append((i + j, i + j + k))
            k //= 2
        p *= 2
    return tuple(pairs)


N_TOP = PEER_TOPK + 1
SUBLANES = 8
_SORT_NET = _batcher_network(PEER_KEYS // SUBLANES)
_CAND_COUNTS = tuple(N_TOP // (a + 1) for a in range(N_TOP))
_CAND_OFFS = tuple(int(v) for v in np.cumsum((0,) + _CAND_COUNTS))
N_CAND = -(-_CAND_OFFS[-1] // SUBLANES) * SUBLANES


def _top_sorted(s_ref, lanes, out_ref, slot):
    n_grp = PEER_KEYS // SUBLANES
    lists = [s_ref[r * SUBLANES:(r + 1) * SUBLANES, lanes] for r in range(n_grp)]
    for a, b in _SORT_NET:
        lists[a], lists[b] = jnp.maximum(lists[a], lists[b]), jnp.minimum(lists[a], lists[b])
    for t in range(N_TOP):
        head = lists[0]
        m = jnp.max(head, axis=0, keepdims=True)
        out_ref[slot, t:t + 1, lanes] = m
        remaining = N_TOP - 1 - t
        hit = head == m
        for k in range(remaining):
            below = lists[k + 1] if k + 1 < n_grp else -jnp.inf
            lists[k] = jnp.where(hit, below, lists[k])


def _route_kernel(x_ref, g_ref, sc_ref, sh_ref, wq_ref, keys_ref,
                  h_ref, e1_ref, e2_ref, thr_ref, s_ref, top_ref, cand_ref):
    tt = x_ref.shape[1]
    h2 = _rms_mod(x_ref[0], g_ref[...], sc_ref[0], sh_ref[0]).astype(MXU_DTYPE)
    h_ref[0] = h2
    q = _nn(h2, wq_ref[...]).astype(MXU_DTYPE)
    top_ref[:, N_TOP:, :] = jnp.full((3, top_ref.shape[1] - N_TOP, tt), -jnp.inf, F32)
    cand_ref[_CAND_OFFS[-1]:, :] = jnp.full((N_CAND - _CAND_OFFS[-1], tt), -1.0, F32)
    for h in range(PEER_HEADS):
        tops = []
        for p, e_ref in ((0, e1_ref), (1, e2_ref)):
            hp = 2 * h + p
            s_ref[...] = _nt(keys_ref[hp], q[:, hp * LANES:(hp + 1) * LANES])
            for tc in range(tt // LANES):
                _top_sorted(s_ref, slice(tc * LANES, (tc + 1) * LANES), top_ref, p)
            m = top_ref[p, 0:1]
            e_ref[0, h] = jnp.exp(s_ref[...] - m)
            tops.append(jnp.exp(top_ref[p] - m))
        e1_top, e2_top = tops
        for a in range(N_TOP):
            cand_ref[_CAND_OFFS[a]:_CAND_OFFS[a + 1]] = e1_top[a:a + 1] * e2_top[0:_CAND_COUNTS[a]]
        _top_values(cand_ref, 0, N_CAND, N_TOP, top_ref.at[2], 0, -1.0)
        best = top_ref[2]
        inv_z = SQRT_HALF / jnp.sum(best[0:PEER_TOPK], axis=0, keepdims=True)
        e1_ref[0, h] = e1_ref[0, h] * inv_z
        thr_ref[0, h:h + 1] = 0.5 * (best[PEER_TOPK - 1:PEER_TOPK] + best[PEER_TOPK:N_TOP]) * inv_z


def _route(x, g2, mod3, mod_row, wq, keys):
    bsz, n, d = x.shape
    tt = min(TT_PEER, n)
    gate_shape = jax.ShapeDtypeStruct((bsz, PEER_HEADS, PEER_KEYS, n), F32)
    gate_spec = pl.BlockSpec((1, PEER_HEADS, PEER_KEYS, tt), lambda b, i: (b, 0, 0, i))
    return pl.pallas_call(
        _route_kernel,
        out_shape=[jax.ShapeDtypeStruct((bsz, n, d), MXU_DTYPE), gate_shape, gate_shape,
                   jax.ShapeDtypeStruct((bsz, PEER_HEADS, n), F32)],
        grid=(bsz, n // tt),
        in_specs=[pl.BlockSpec((1, tt, d), lambda b, i: (b, i, 0)),
                  pl.BlockSpec((1, d), lambda b, i: (0, 0)),
                  pl.BlockSpec((1, 1, d), lambda b, i: (mod_row(b), 0, 4)),
                  pl.BlockSpec((1, 1, d), lambda b, i: (mod_row(b), 0, 3)),
                  _resident(wq.shape), _resident(keys.shape)],
        out_specs=[pl.BlockSpec((1, tt, d), lambda b, i: (b, i, 0)), gate_spec, gate_spec,
                   pl.BlockSpec((1, PEER_HEADS, tt), lambda b, i: (b, 0, i))],
        scratch_shapes=[pltpu.VMEM((PEER_KEYS, tt), F32),
                        pltpu.VMEM((3, 3 * SUBLANES, tt), F32),
                        pltpu.VMEM((N_CAND, tt), F32)],
        compiler_params=_params(("parallel", "parallel")),
        name="peer_route",
    )(x, g2, mod3, mod3, wq, keys)


def _expert_kernel(x_ref, gt_ref, gf_ref, h_ref, u_ref, v_ref, e1_ref, e2_ref, thr_ref, o_ref,
                   acc_ref, a_ref, w_ref, *, final_norm):
    eb = pl.program_id(2)
    tt = h_ref.shape[1]

    @pl.when(eb == 0)
    def _zero():
        acc_ref[...] = jnp.zeros_like(acc_ref)

    a_ref[...] = _nt(u_ref[...], h_ref[0])
    sub = 64
    n_i = TE_PEER // PEER_KEYS
    i0 = pl.multiple_of(eb * n_i, n_i)
    for il in range(n_i):
        for tc in range(tt // LANES):
            lanes = slice(tc * LANES, (tc + 1) * LANES)
            e1_rows = [e1_ref[0, h, pl.ds(i0, n_i), lanes][il:il + 1] for h in range(PEER_HEADS)]
            thr_rows = [thr_ref[0, h:h + 1, lanes] for h in range(PEER_HEADS)]
            for js in range(PEER_KEYS // sub):
                rows = slice(js * sub, (js + 1) * sub)
                gate = None
                for h in range(PEER_HEADS):
                    prod = e2_ref[0, h, rows, lanes] * e1_rows[h]
                    sel = jnp.where(prod >= thr_rows[h], prod, 0.0)
                    gate = sel if gate is None else gate + sel
                erows = slice(il * PEER_KEYS + js * sub, il * PEER_KEYS + (js + 1) * sub)
                t = a_ref[erows, lanes] * SQRT_HALF
                w_ref[erows, lanes] = (gate * t) * (1.0 + lax.erf(t))
    acc_ref[...] += _nn(w_ref[...].T.astype(MXU_DTYPE), v_ref[...])

    @pl.when(eb == pl.num_programs(2) - 1)
    def _finish():
        y = x_ref[0] + gt_ref[0] * acc_ref[...]
        if final_norm:
            y = (y * lax.rsqrt(jnp.mean(y * y, axis=-1, keepdims=True) + NORM_EPS)) * gf_ref[...]
        o_ref[0] = y


def _experts(x, mod3, mod_row, g_final, h2, u, v, e1, e2, thr, final_norm):
    bsz, n, d = x.shape
    tt = min(TT_PEER, n)
    n_exp = u.shape[0]
    gate_spec = pl.BlockSpec((1, PEER_HEADS, PEER_KEYS, tt), lambda b, i, e: (b, 0, 0, i))
    return pl.pallas_call(
        functools.partial(_expert_kernel, final_norm=final_norm),
        out_shape=jax.ShapeDtypeStruct(x.shape, F32),
        grid=(bsz, n // tt, n_exp // TE_PEER),
        in_specs=[pl.BlockSpec((1, tt, d), lambda b, i, e: (b, i, 0)),
                  pl.BlockSpec((1, 1, d), lambda b, i, e: (mod_row(b), 0, 5)),
                  pl.BlockSpec((1, d), lambda b, i, e: (0, 0)),
                  pl.BlockSpec((1, tt, d), lambda b, i, e: (b, i, 0)),
                  pl.BlockSpec((TE_PEER, d), lambda b, i, e: (e, 0)),
                  pl.BlockSpec((TE_PEER, d), lambda b, i, e: (e, 0)),
                  gate_spec, gate_spec,
                  pl.BlockSpec((1, PEER_HEADS, tt), lambda b, i, e: (b, 0, i))],
        out_specs=pl.BlockSpec((1, tt, d), lambda b, i, e: (b, i, 0)),
        scratch_shapes=[pltpu.VMEM((tt, d), F32),
                        pltpu.VMEM((TE_PEER, tt), F32),
                        pltpu.VMEM((TE_PEER, tt), F32)],
        compiler_params=_params(("parallel", "parallel", "arbitrary")),
        name="peer_experts",
    )(x, mod3, g_final, h2, u, v, e1, e2, thr)


def _rope_tables(n):
    t = jnp.arange(n, dtype=jnp.int32)
    row = (t // GRID_W).astype(F32)
    col = (t % GRID_W).astype(F32)
    n_freq = HEAD_DIM // 4
    inv = ROPE_THETA ** (-jnp.arange(n_freq, dtype=F32) / n_freq)
    ang = jnp.concatenate([row[:, None] * inv, col[:, None] * inv], axis=-1)
    cos, sin = jnp.cos(ang), jnp.sin(ang)
    return jnp.tile(cos, (1, 4)), jnp.tile(jnp.concatenate([-sin, sin], axis=-1), (1, 2))


def kernel(x, c, ctx, c_ctx, w_mod, b_mod, g_norm1, g_norm2, w_in, a_sink, b_rpb, c_lambda,
           c_subln, w_branch_a, w_branch_b, w_branch_c, w_out, peer_wq, peer_keys, peer_u,
           peer_v, g_final):
    bsz, n, d = x.shape
    depth = w_mod.shape[0]
    n_ctx = ctx.shape[1]
    assert d == D_MODEL and bsz + 1 <= 8
    assert n % (B_ROWS * GRID_W) == 0 and n // GRID_W >= 4 * B_KROWS
    assert n % (A_QBLOCKS * A_BLOCK) == 0 and (n + n_ctx) % LANES == 0

    cvec = jnp.zeros((8, d), F32).at[:bsz].set(c).at[bsz].set(c_ctx)
    mod = _modulation(cvec, w_mod, b_mod)
    lat_row = lambda b: b
    ctx_row = lambda b: bsz
    cos_t, sin_t = _rope_tables(n)
    ones_t, zeros_t = jnp.ones((n_ctx, LANES), F32), jnp.zeros((n_ctx, LANES), F32)
    cast = lambda w: w.astype(MXU_DTYPE)
    gfin = g_final.reshape(1, d)

    xc = ctx
    for l in range(depth):
        last = l == depth - 1
        lam_init = 0.8 - 0.6 * math.exp(-0.3 * l)
        mod3 = mod[l].reshape(8, 1, 6 * d)
        g1, g2 = g_norm1[l].reshape(1, d), g_norm2[l].reshape(1, d)
        w_in_l = cast(w_in[l])
        subln3 = c_subln[l].reshape(C_HEADS, 1, LANES)
        wa, wb, wc, wo = cast(w_branch_a[l]), cast(w_branch_b[l]), cast(w_branch_c[l]), cast(w_out[l])
        wq = cast(peer_wq[l])
        keys = cast(peer_keys[l].reshape(2 * PEER_HEADS, PEER_KEYS, LANES))
        u, v = cast(peer_u[l]), cast(peer_v[l])

        lat = _in_proj(x, g1, mod3, lat_row, w_in_l, cos_t, sin_t)
        con = _in_proj(xc, g1, mod3, ctx_row, w_in_l, ones_t, zeros_t)
        (aq, ak, aks, av, avs, bq, bk, bv, cq, ck, _, ga, gb, gc, cvt) = lat
        ya = _attn_a(a_sink[l], aq, ak, aks, av, avs, con[1], con[2], con[3], con[4])
        yb = _attn_b(b_rpb[l], bq, bk, bv, con[6], con[7])
        yc = _attn_c(c_lambda[l], subln3, cq, ck, cvt, con[9], con[14], lam_init)
        x = _merge(x, mod3, lat_row, ya, yb, yc, ga, gb, gc, wa, wb, wc, wo)
        h2, e1, e2, thr = _route(x, g2, mod3, lat_row, wq, keys)
        x = _experts(x, mod3, lat_row, gfin, h2, u, v, e1, e2, thr, final_norm=last)

        if not last:
            ya_c, yb_c, yc_c = _ctx_attn(a_sink[l], c_lambda[l], subln3, con, lam_init)
            xc = _merge(xc, mod3, ctx_row, ya_c, yb_c, yc_c, con[11], con[12], con[13],
                        wa, wb, wc, wo)
            h2c, e1c, e2c, thrc = _route(xc, g2, mod3, ctx_row, wq, keys)
            xc = _experts(xc, mod3, ctx_row, gfin, h2c, u, v, e1c, e2c, thrc, final_norm=False)
    return x
```

```python
import functools
import math

import jax
import jax.numpy as jnp
import numpy as np
from jax import lax
from jax.experimental import pallas as pl
from jax.experimental.pallas import tpu as pltpu

F32 = jnp.float32
BF16 = jnp.bfloat16
MXU_DTYPE = BF16

LANES = 128
MXU_WIDTH = 256
VMEM_LIMIT_BYTES = 56 * 1024 * 1024

HEAD_DIM = 64
GRID_W = 64
ROPE_THETA = 10000.0
NORM_EPS = 1e-6
NEG_INF = -1e30
LOG2_E = math.log2(math.e)
SQRT_HALF = math.sqrt(0.5)
A_HEADS, A_KV_HEADS, A_WINDOW, A_BLOCK = 8, 2, 128, 128
B_HEADS, NA_ROWS, NA_COLS = 8, 8, 16
C_HEADS = 4
PEER_HEADS, PEER_KEYS, PEER_TOPK = 8, 128, 16
D_MODEL = 1024
IN_SPLITS = (512, 128, 128, 512, 512, 512, 512, 512, 512, 1024, 1024, 1024)
IN_OFFS = tuple(int(v) for v in np.cumsum((0,) + IN_SPLITS))

TM_PROJ = 512
A_QBLOCKS = 4
TQ_C = 1024
TK_C = 768
B_ROWS = 8
B_KROWS = 4
TT_PEER = 512
TE_PEER = 1024
N_PT = 30


def _nt(a, b):
    return lax.dot_general(a, b, (((1,), (1,)), ((), ())), preferred_element_type=F32)


def _nn(a, b):
    return jnp.dot(a, b, preferred_element_type=F32)


def _params(sem):
    return pltpu.CompilerParams(dimension_semantics=sem, vmem_limit_bytes=VMEM_LIMIT_BYTES)


def _resident(shape):
    nd = len(shape)
    return pl.BlockSpec(shape, lambda *_: (0,) * nd, pipeline_mode=pl.Buffered(1))


def _mod_kernel(c_ref, w_ref, b_ref, o_ref):
    c = c_ref[...]
    s = c * jax.nn.sigmoid(c)
    w = w_ref[0]
    s_hi = s.astype(MXU_DTYPE)
    s_lo = (s - s_hi.astype(F32)).astype(MXU_DTYPE)
    w_hi = w.astype(MXU_DTYPE)
    w_lo = (w - w_hi.astype(F32)).astype(MXU_DTYPE)
    acc = _nn(s_hi, w_hi) + _nn(s_lo, w_hi) + _nn(s_hi, w_lo)
    o_ref[0] = acc + b_ref[0]


def _modulation(cvec, w_mod, b_mod):
    depth, d, n6 = w_mod.shape
    tn = 1536
    return pl.pallas_call(
        _mod_kernel,
        out_shape=jax.ShapeDtypeStruct((depth, 8, n6), F32),
        grid=(depth, n6 // tn),
        in_specs=[pl.BlockSpec((8, d), lambda l, j: (0, 0)),
                  pl.BlockSpec((1, d, tn), lambda l, j: (l, 0, j)),
                  pl.BlockSpec((1, 1, tn), lambda l, j: (l, 0, j))],
        out_specs=pl.BlockSpec((1, 8, tn), lambda l, j: (l, 0, j)),
        compiler_params=_params(("parallel", "parallel")),
        name="modulation",
    )(cvec, w_mod, b_mod.reshape(depth, 1, n6))


def _rms_mod(x, g, sc, sh):
    y = x * lax.rsqrt(jnp.mean(x * x, axis=-1, keepdims=True) + NORM_EPS)
    return (y * g) * (1.0 + sc) + sh


def _in_proj_kernel(x_ref, g_ref, sc_ref, sh_ref, w_ref, cos_ref, sin_ref,
                    aq_ref, ak_ref, aks_ref, av_ref, avs_ref, bq_ref, bk_ref, bv_ref,
                    cq_ref, ck_ref, cv_ref, ga_ref, gb_ref, gc_ref, cvt_ref):
    h = _rms_mod(x_ref[0], g_ref[...], sc_ref[0], sh_ref[0]).astype(MXU_DTYPE)
    cos = cos_ref[...]
    sin = sin_ref[...]
    lane = lax.broadcasted_iota(jnp.int32, (1, LANES), 1)
    first_half = (lane % HEAD_DIM) < (HEAD_DIM // 2)
    scale = HEAD_DIM ** -0.5

    def proj(seg, jp):
        c0 = IN_OFFS[seg] + jp * MXU_WIDTH
        t = _nn(h, w_ref[:, c0:c0 + MXU_WIDTH])
        return t[:, :LANES], t[:, LANES:]

    def rope(v):
        rot = jnp.where(first_half, pltpu.roll(v, LANES - 32, 1), pltpu.roll(v, 32, 1))
        return v * cos + rot * sin

    def emit(ref, seg, fn=lambda v: v):
        for jp in range(IN_SPLITS[seg] // MXU_WIDTH):
            for k, part in enumerate(proj(seg, jp)):
                j = 2 * jp + k
                ref[0, :, j * LANES:(j + 1) * LANES] = fn(part).astype(ref.dtype)

    emit(aq_ref, 0, lambda v: rope(v) * scale)
    emit(bq_ref, 3, lambda v: v * scale)
    emit(bk_ref, 4)
    emit(bv_ref, 5)
    emit(cq_ref, 6, lambda v: rope(v) * (scale * LOG2_E))
    emit(ck_ref, 7, rope)
    for jp in range(IN_SPLITS[8] // MXU_WIDTH):
        for k, cv in enumerate(proj(8, jp)):
            j = 2 * jp + k
            cv_ref[0, :, j * LANES:(j + 1) * LANES] = cv.astype(cv_ref.dtype)
            cvt_ref[0, j * LANES:(j + 1) * LANES, :] = cv.T.astype(cvt_ref.dtype)
    ak, av = proj(1, 0)
    ak = rope(ak)
    ak_ref[0] = ak.astype(ak_ref.dtype)
    aks_ref[0] = pltpu.roll(ak, HEAD_DIM, 1).astype(aks_ref.dtype)
    av_ref[0] = av.astype(av_ref.dtype)
    avs_ref[0] = pltpu.roll(av, HEAD_DIM, 1).astype(avs_ref.dtype)
    emit(ga_ref, 9, jax.nn.sigmoid)
    emit(gb_ref, 10, jax.nn.sigmoid)
    emit(gc_ref, 11, jax.nn.sigmoid)


def _in_proj(x, g1, mod3, mod_row, w_in, cos_t, sin_t):
    bsz, n, d = x.shape
    tm = min(TM_PROJ, n)
    tok = lambda w: pl.BlockSpec((1, tm, w), lambda b, i: (b, i, 0))
    widths = (512, 128, 128, 128, 128, 512, 512, 512, 512, 512, 512, 1024, 1024, 1024)
    return pl.pallas_call(
        _in_proj_kernel,
        out_shape=[jax.ShapeDtypeStruct((bsz, n, w), MXU_DTYPE) for w in widths]
        + [jax.ShapeDtypeStruct((bsz, 512, n), MXU_DTYPE)],
        grid=(bsz, n // tm),
        in_specs=[tok(d),
                  pl.BlockSpec((1, d), lambda b, i: (0, 0)),
                  pl.BlockSpec((1, 1, d), lambda b, i: (mod_row(b), 0, 1)),
                  pl.BlockSpec((1, 1, d), lambda b, i: (mod_row(b), 0, 0)),
                  _resident(w_in.shape),
                  pl.BlockSpec((tm, LANES), lambda b, i: (i, 0)),
                  pl.BlockSpec((tm, LANES), lambda b, i: (i, 0))],
        out_specs=[tok(w) for w in widths] + [pl.BlockSpec((1, 512, tm), lambda b, i: (b, 0, i))],
        compiler_params=_params(("parallel", "parallel")),
        name="in_proj",
    )(x, g1, mod3, mod3, w_in, cos_t, sin_t)


def _lane_lo():
    return lax.broadcasted_iota(jnp.int32, (1, LANES), 1) < HEAD_DIM


def _softmax_pv(s_list, v_list, extra_logit=None, exp=jnp.exp):
    m = functools.reduce(jnp.maximum, [jnp.max(s, axis=-1, keepdims=True) for s in s_list])
    if extra_logit is not None:
        m = jnp.maximum(m, extra_logit)
    l = 0.0 if extra_logit is None else exp(extra_logit - m)
    o = 0.0
    for s, v in zip(s_list, v_list):
        p = exp(s - m)
        l = l + jnp.sum(p, axis=-1, keepdims=True)
        o = o + _nn(p.astype(v.dtype), v)
    return o / l


def _attn_a_kernel(sink_ref, q_ref, kp_ref, kc_ref, kn_ref, ksp_ref, ksc_ref, ksn_ref,
                   vp_ref, vc_ref, vn_ref, vsp_ref, vsc_ref, vsn_ref,
                   xk_ref, xks_ref, xv_ref, xvs_ref, o_ref, *, n_tokens):
    i = pl.program_id(1)
    lo = _lane_lo()
    cat = lambda refs: jnp.concatenate([r[0] for r in refs], axis=0)
    k_loc, ks_loc = cat((kp_ref, kc_ref, kn_ref)), cat((ksp_ref, ksc_ref, ksn_ref))
    v_loc, vs_loc = cat((vp_ref, vc_ref, vn_ref)), cat((vsp_ref, vsc_ref, vsn_ref))
    zero = jnp.zeros((), k_loc.dtype)
    k_of = {(0, 0): (jnp.where(lo, k_loc, zero), jnp.where(lo, xk_ref[0], zero)),
            (0, 1): (jnp.where(lo, zero, ks_loc), jnp.where(lo, zero, xks_ref[0])),
            (1, 0): (jnp.where(lo, ks_loc, zero), jnp.where(lo, xks_ref[0], zero)),
            (1, 1): (jnp.where(lo, zero, k_loc), jnp.where(lo, zero, xk_ref[0]))}
    v_of = {(0, 0): (v_loc, xv_ref[0]), (0, 1): (vs_loc, xvs_ref[0]),
            (1, 0): (vs_loc, xvs_ref[0]), (1, 1): (v_loc, xv_ref[0])}
    chains = []
    for qb in range(A_QBLOCKS):
        blk = i * A_QBLOCKS + qb
        qpos = blk * A_BLOCK + lax.broadcasted_iota(jnp.int32, (A_BLOCK, 1), 0)
        kpos = (blk - 1) * A_BLOCK + lax.broadcasted_iota(jnp.int32, (1, 3 * A_BLOCK), 1)
        dist = qpos - kpos
        valid = (jnp.maximum(dist, -dist) <= A_WINDOW) & (kpos >= 0) & (kpos < n_tokens)
        qrows = slice(qb * A_BLOCK, (qb + 1) * A_BLOCK)
        krows = slice(qb * A_BLOCK, (qb + 3) * A_BLOCK)
        for hp in range(A_HEADS // 2):
            qp = q_ref[0, qrows, hp * LANES:(hp + 1) * LANES]
            for half in range(2):
                h = 2 * hp + half
                g = h // (A_HEADS // A_KV_HEADS)
                (kl, kx), (vl, vx) = k_of[(g, half)], v_of[(g, half)]
                s_loc = jnp.where(valid, _nt(qp, kl[krows]), NEG_INF)
                chains.append(([s_loc, _nt(qp, kx)], [vl[krows], vx], sink_ref[h]))
    outs = [_softmax_pv(s, v, extra_logit=sink) for s, v, sink in chains]
    for qb in range(A_QBLOCKS):
        qrows = slice(qb * A_BLOCK, (qb + 1) * A_BLOCK)
        for hp in range(A_HEADS // 2):
            even, odd = outs[(qb * (A_HEADS // 2) + hp) * 2:(qb * (A_HEADS // 2) + hp) * 2 + 2]
            o_ref[0, qrows, hp * LANES:(hp + 1) * LANES] = jnp.where(lo, even, odd).astype(o_ref.dtype)


def _attn_a(sink, aq, ak, aks, av, avs, xak, xaks, xav, xavs):
    bsz, n, _ = aq.shape
    nb = n // A_BLOCK
    tq = A_QBLOCKS * A_BLOCK
    ctx = xak.shape[1]
    prev = pl.BlockSpec((1, A_BLOCK, LANES),
                        lambda b, i: (b, jnp.maximum(i * A_QBLOCKS - 1, 0), 0))
    cur = pl.BlockSpec((1, tq, LANES), lambda b, i: (b, i, 0))
    nxt = pl.BlockSpec((1, A_BLOCK, LANES),
                       lambda b, i: (b, jnp.minimum((i + 1) * A_QBLOCKS, nb - 1), 0))
    cx = pl.BlockSpec((1, ctx, LANES), lambda b, i: (b, 0, 0))
    return pl.pallas_call(
        functools.partial(_attn_a_kernel, n_tokens=n),
        out_shape=jax.ShapeDtypeStruct(aq.shape, MXU_DTYPE),
        grid=(bsz, n // tq),
        in_specs=[pl.BlockSpec(memory_space=pltpu.SMEM),
                  pl.BlockSpec((1, tq, 512), lambda b, i: (b, i, 0)),
                  prev, cur, nxt, prev, cur, nxt, prev, cur, nxt, prev, cur, nxt, cx, cx, cx, cx],
        out_specs=pl.BlockSpec((1, tq, 512), lambda b, i: (b, i, 0)),
        compiler_params=_params(("parallel", "parallel")),
        name="attn_window",
    )(sink, aq, ak, ak, ak, aks, aks, aks, av, av, av, avs, avs, avs, xak, xaks, xav, xavs)


def _attn_b_kernel(rpb_ref, q_ref, k0_ref, k1_ref, k2_ref, k3_ref, v0_ref, v1_ref, v2_ref, v3_ref,
                   xk_ref, xv_ref, o_ref, pt_ref, *, n_rows):
    pair, i = pl.program_id(0), pl.program_id(2)
    n_a = 2 * NA_ROWS - 1
    n_b = 2 * NA_COLS - 1

    @pl.when((pl.program_id(1) == 0) & (i == 0))
    def _build_bias_tables():
        qc = lax.broadcasted_iota(jnp.int32, (GRID_W, LANES), 0)
        ln = lax.broadcasted_iota(jnp.int32, (GRID_W, LANES), 1)
        kc = ln % GRID_W
        hi = ln >= GRID_W
        cstart = jnp.clip(qc - NA_COLS // 2, 0, GRID_W - NA_COLS)
        col_ok = (kc >= cstart) & (kc < cstart + NA_COLS)
        d = kc - qc + (NA_COLS - 1)
        for hh in range(2):
            h = pair * 2 + hh

            def body(ai, carry):
                a = ai - 8
                a_ok, a1_ok = (a >= 0) & (a < n_a), (a + 1 >= 0) & (a + 1 < n_a)
                ra, ra1 = h * n_a + jnp.clip(a, 0, n_a - 1), h * n_a + jnp.clip(a + 1, 0, n_a - 1)
                t = jnp.full((GRID_W, LANES), NEG_INF, F32)
                for b in range(n_b):
                    va = jnp.where(a_ok, rpb_ref[ra, b], NEG_INF)
                    va1 = jnp.where(a1_ok, rpb_ref[ra1, b], NEG_INF)
                    t = jnp.where(d == b, jnp.where(hi, va1, va), t)
                pt_ref[hh, ai] = jnp.where(col_ok, t, NEG_INF)
                return carry

            lax.fori_loop(0, N_PT, body, 0)

    lo = _lane_lo()
    r0 = i * B_ROWS
    kb0 = jnp.clip(2 * i - 1, 0, n_rows // B_KROWS - 4)
    krow_lane = lax.broadcasted_iota(jnp.int32, (1, B_KROWS * GRID_W), 1) // GRID_W
    k_refs, v_refs = (k0_ref, k1_ref, k2_ref, k3_ref), (v0_ref, v1_ref, v2_ref, v3_ref)
    qp = q_ref[0]
    zero = jnp.zeros((), qp.dtype)
    raw = []
    for hh in range(2):
        keep = lambda t: jnp.where(lo, t, zero) if hh == 0 else jnp.where(lo, zero, t)
        raw.append([_nt(qp, keep(k_refs[j][0, 0])) for j in range(4)]
                   + [_nt(qp, keep(xk_ref[0]))])
    outs = []
    for hh in range(2):
        s_tiles = []
        for j in range(4):
            s = raw[hh][j]
            kr0 = (kb0 + j) * B_KROWS
            rows = []
            for qr in range(B_ROWS):
                r = r0 + qr
                rs = jnp.clip(r - NA_ROWS // 2, 0, n_rows - NA_ROWS)
                a0 = kr0 - r + (NA_ROWS - 1)
                bias = jnp.concatenate([pt_ref[hh, a0 + 8], pt_ref[hh, a0 + 10]], axis=1)
                row_ok = (kr0 + krow_lane >= rs) & (kr0 + krow_lane < rs + NA_ROWS)
                rows.append(jnp.where(row_ok, s[qr * GRID_W:(qr + 1) * GRID_W] + bias, NEG_INF))
            s_tiles.append(jnp.concatenate(rows, axis=0))
        s_tiles.append(raw[hh][4])
        outs.append(_softmax_pv(s_tiles, [v_refs[j][0, 0] for j in range(4)] + [xv_ref[0]]))
    o_ref[0] = jnp.where(lo, outs[0], outs[1]).astype(o_ref.dtype)


def _attn_b(rpb, bq, bk, bv, xbk, xbv):
    bsz, n, _ = bq.shape
    n_rows = n // GRID_W
    tq = B_ROWS * GRID_W
    tkb = B_KROWS * GRID_W
    ctx = xbk.shape[1]
    bk4 = bk.reshape(bsz, n // tkb, tkb, 512)
    bv4 = bv.reshape(bsz, n // tkb, tkb, 512)
    nkb = n // tkb

    def kspec(j):
        return pl.BlockSpec((1, 1, tkb, LANES),
                            lambda p, b, i: (b, jnp.clip(2 * i - 1, 0, nkb - 4) + j, 0, p))

    cx = pl.BlockSpec((1, ctx, LANES), lambda p, b, i: (b, 0, p))
    return pl.pallas_call(
        functools.partial(_attn_b_kernel, n_rows=n_rows),
        out_shape=jax.ShapeDtypeStruct(bq.shape, MXU_DTYPE),
        grid=(B_HEADS // 2, bsz, n // tq),
        in_specs=[pl.BlockSpec(memory_space=pltpu.SMEM),
                  pl.BlockSpec((1, tq, LANES), lambda p, b, i: (b, i, p)),
                  kspec(0), kspec(1), kspec(2), kspec(3), kspec(0), kspec(1), kspec(2), kspec(3),
                  cx, cx],
        out_specs=pl.BlockSpec((1, tq, LANES), lambda p, b, i: (b, i, p)),
        scratch_shapes=[pltpu.VMEM((2, N_PT, GRID_W, LANES), F32)],
        compiler_params=_params(("arbitrary", "arbitrary", "arbitrary")),
        name="attn_neighbourhood",
    )(rpb.reshape(B_HEADS * (2 * NA_ROWS - 1), 2 * NA_COLS - 1), bq,
      bk4, bk4, bk4, bk4, bv4, bv4, bv4, bv4, xbk, xbv)


def _diff_lambda(lam_ref, lam_init):
    lam = lam_ref[...]
    a = jnp.sum(lam[0:1] * lam[1:2], axis=-1, keepdims=True)
    b = jnp.sum(lam[2:3] * lam[3:4], axis=-1, keepdims=True)
    return jnp.exp(a) - jnp.exp(b) + lam_init


def _head_norm(o, g, lam_init):
    y = o * lax.rsqrt(jnp.mean(o * o, axis=-1, keepdims=True) + NORM_EPS)
    return (y * g) * (1.0 - lam_init)


def _attn_c_kernel(lam_ref, g_ref, q_ref, k_ref, vt_ref, xk_ref, xvt_ref, o_ref,
                   k1_ref, k2_ref, vall_ref, s_ref, p_ref, stat_ref, acc_ref,
                   *, lam_init, n_tokens, tk, n_chunks):
    lo = _lane_lo()

    @pl.when(pl.program_id(2) == 0)
    def _split_keys():
        zero = jnp.zeros((), k1_ref.dtype)
        k1_ref[0:n_tokens] = jnp.where(lo, k_ref[0], zero)
        k2_ref[0:n_tokens] = jnp.where(lo, zero, k_ref[0])
        k1_ref[n_tokens:] = jnp.where(lo, xk_ref[0], zero)
        k2_ref[n_tokens:] = jnp.where(lo, zero, xk_ref[0])

        vall_ref[:, 0:n_tokens] = vt_ref[0]
        vall_ref[:, n_tokens:] = xvt_ref[0]

    q = q_ref[0]
    tq = q.shape[0]
    key_refs = (k1_ref, k2_ref)
    n = n_chunks
    j = pl.program_id(2)
    n_q = pl.num_programs(2) - 1
    bp = j % 2

    slot_of = lambda c: 2 if c == 0 else c % 2

    def reset(par):
        for c in range(2):
            stat_ref[2 * par + c, 0:1, :] = jnp.full((1, tq), -jnp.inf, F32)
            stat_ref[2 * par + c, 1:2, :] = jnp.zeros((1, tq), F32)
            acc_ref[2 * par + c] = jnp.zeros(acc_ref.shape[1:], F32)

    def scores(par, t, slot):
        off = pl.multiple_of(t * tk, LANES)
        for c in range(2):
            s = _nt(key_refs[c][pl.ds(off, tk), :], q)
            s_ref[slot, c] = s
            stat_ref[2 * par + c, 5 + slot:6 + slot, :] = jnp.max(s, axis=0, keepdims=True)

    def softmax(par, slot):
        for c in range(2):
            st = stat_ref.at[2 * par + c]
            s = s_ref[slot, c]
            m_old = st[0:1, :]
            m_new = jnp.maximum(m_old, st[5 + slot:6 + slot, :])
            alpha = jnp.exp2(m_old - m_new)
            p = jnp.exp2(s - m_new)
            p_ref[slot, c] = p.astype(p_ref.dtype)
            st[0:1, :] = m_new
            st[1:2, :] = alpha * st[1:2, :] + jnp.sum(p, axis=0, keepdims=True)
            st[2 + slot:3 + slot, :] = alpha

    def values(par, t, slot):
        off = pl.multiple_of(t * tk, LANES)
        vt = vall_ref[:, pl.ds(off, tk)]
        for c in range(2):
            k = 2 * par + c
            acc_ref[k] = stat_ref[k, 2 + slot:3 + slot, :] * acc_ref[k] + _nn(vt, p_ref[slot, c])

    def finish(par):
        o1 = acc_ref[2 * par] * (1.0 / stat_ref[2 * par, 1:2, :])
        o2 = acc_ref[2 * par + 1] * (1.0 / stat_ref[2 * par + 1, 1:2, :])
        ot = o1 - _diff_lambda(lam_ref, lam_init) * o2
        o_ref[0] = _head_norm(ot.T, g_ref[0], lam_init).astype(o_ref.dtype)

    @pl.when(j == 0)
    def _first_fill():
        reset(bp)
        scores(bp, 0, slot_of(0))
        scores(bp, 1, slot_of(1))
        softmax(bp, slot_of(0))

    @pl.when((j > 0) & (j < n_q))
    def _fill_and_drain():
        reset(bp)
        scores(bp, 0, slot_of(0))
        values(1 - bp, n - 2, slot_of(n - 2))
        softmax(1 - bp, slot_of(n - 1))
        scores(bp, 1, slot_of(1))
        values(1 - bp, n - 1, slot_of(n - 1))
        softmax(bp, slot_of(0))
        finish(1 - bp)

    @pl.when(j == n_q)
    def _last_drain():
        values(1 - bp, n - 2, slot_of(n - 2))
        softmax(1 - bp, slot_of(n - 1))
        values(1 - bp, n - 1, slot_of(n - 1))
        finish(1 - bp)

    @pl.when(j < n_q)
    def _steady():
        def iteration(t, t_static_slot):
            scores(bp, t, t_static_slot)
            values(bp, t - 2, slot_of(t - 2) if isinstance(t, int) else t_static_slot)
            softmax(bp, 1 - t_static_slot if not isinstance(t, int) or t - 1 > 0 else slot_of(0))

        if n > 2:
            iteration(2, slot_of(2))

        def pair(k, carry):
            iteration(3 + 2 * k, 1)
            iteration(4 + 2 * k, 0)
            return carry

        n_pairs = max(n - 3, 0) // 2
        lax.fori_loop(0, n_pairs, pair, 0)
        for t in range(3 + 2 * n_pairs, n):
            iteration(t, slot_of(t))


def _key_chunk(n_keys):
    return max(c for c in range(LANES, TK_C + 1, LANES) if n_keys % c == 0)


def _attn_c(c_lambda, subln3, cq, ck, cvt, xck, xcvt, lam_init):
    bsz, n, _ = cq.shape
    ctx = xck.shape[1]
    tq = min(TQ_C, n)
    tk = _key_chunk(n + ctx)
    return pl.pallas_call(
        functools.partial(_attn_c_kernel, lam_init=lam_init, n_tokens=n, tk=tk,
                          n_chunks=(n + ctx) // tk),
        out_shape=jax.ShapeDtypeStruct(cq.shape, MXU_DTYPE),
        grid=(bsz, C_HEADS, n // tq + 1),
        in_specs=[pl.BlockSpec((4, HEAD_DIM), lambda b, h, i: (0, 0)),
                  pl.BlockSpec((1, 1, LANES), lambda b, h, i: (h, 0, 0)),
                  pl.BlockSpec((1, tq, LANES), lambda b, h, i: (b, jnp.minimum(i, n // tq - 1), h)),
                  pl.BlockSpec((1, n, LANES), lambda b, h, i: (b, 0, h)),
                  pl.BlockSpec((1, LANES, n), lambda b, h, i: (b, h, 0)),
                  pl.BlockSpec((1, ctx, LANES), lambda b, h, i: (b, 0, h)),
                  pl.BlockSpec((1, LANES, ctx), lambda b, h, i: (b, h, 0))],
        out_specs=pl.BlockSpec((1, tq, LANES), lambda b, h, i: (b, jnp.maximum(i - 1, 0), h)),
        scratch_shapes=[pltpu.VMEM((n + ctx, LANES), MXU_DTYPE),
                        pltpu.VMEM((n + ctx, LANES), MXU_DTYPE),
                        pltpu.VMEM((LANES, n + ctx), MXU_DTYPE),
                        pltpu.VMEM((3, 2, tk, tq), F32),
                        pltpu.VMEM((3, 2, tk, tq), MXU_DTYPE),
                        pltpu.VMEM((4, 8, tq), F32),
                        pltpu.VMEM((4, LANES, tq), F32)],
        compiler_params=_params(("arbitrary", "arbitrary", "arbitrary")),
        name="attn_differential",
    )(c_lambda, subln3, cq, ck, cvt, xck, xcvt)


def _ctx_attn_kernel(sink_ref, lam_ref, g_ref, aq_ref, ak_ref, aks_ref, av_ref, avs_ref,
                     bq_ref, bk_ref, bv_ref, cq_ref, ck_ref, cv_ref, ya_ref, yb_ref, yc_ref,
                     *, lam_init):
    lo = _lane_lo()
    zero = jnp.zeros((), ak_ref.dtype)
    sel = lambda t, half: jnp.where(lo, t, zero) if half == 0 else jnp.where(lo, zero, t)
    k_of = {(0, 0): sel(ak_ref[0], 0), (0, 1): sel(aks_ref[0], 1),
            (1, 0): sel(aks_ref[0], 0), (1, 1): sel(ak_ref[0], 1)}
    v_of = {(0, 0): av_ref[0], (0, 1): avs_ref[0], (1, 0): avs_ref[0], (1, 1): av_ref[0]}
    for hp in range(A_HEADS // 2):
        cols = slice(hp * LANES, (hp + 1) * LANES)
        qa, qb = aq_ref[0, :, cols], bq_ref[0, :, cols]
        kb, vb = bk_ref[0, :, cols], bv_ref[0, :, cols]
        oa, ob = [], []
        for half in range(2):
            h = 2 * hp + half
            g = h // (A_HEADS // A_KV_HEADS)
            oa.append(_softmax_pv([_nt(qa, k_of[(g, half)])], [v_of[(g, half)]],
                                  extra_logit=sink_ref[h]))
            ob.append(_softmax_pv([_nt(qb, sel(kb, half))], [vb]))
        ya_ref[0, :, cols] = jnp.where(lo, oa[0], oa[1]).astype(ya_ref.dtype)
        yb_ref[0, :, cols] = jnp.where(lo, ob[0], ob[1]).astype(yb_ref.dtype)
    lam = _diff_lambda(lam_ref, lam_init)
    for h in range(C_HEADS):
        cols = slice(h * LANES, (h + 1) * LANES)
        q, k, v = cq_ref[0, :, cols], ck_ref[0, :, cols], cv_ref[0, :, cols]
        o = (_softmax_pv([_nt(q, sel(k, 0))], [v], exp=jnp.exp2)
             - lam * _softmax_pv([_nt(q, sel(k, 1))], [v], exp=jnp.exp2))
        yc_ref[0, :, cols] = _head_norm(o, g_ref[h], lam_init).astype(yc_ref.dtype)


def _ctx_attn(sink, c_lambda, subln3, con, lam_init):
    aq, ak, aks, av, avs, bq, bk, bv, cq, ck, cv = con[:11]
    bsz, ctx, _ = aq.shape
    wide = pl.BlockSpec((1, ctx, 512), lambda b: (b, 0, 0))
    nar = pl.BlockSpec((1, ctx, LANES), lambda b: (b, 0, 0))
    return pl.pallas_call(
        functools.partial(_ctx_attn_kernel, lam_init=lam_init),
        out_shape=[jax.ShapeDtypeStruct(aq.shape, MXU_DTYPE)] * 3,
        grid=(bsz,),
        in_specs=[pl.BlockSpec(memory_space=pltpu.SMEM),
                  pl.BlockSpec((4, HEAD_DIM), lambda b: (0, 0)),
                  pl.BlockSpec((C_HEADS, 1, LANES), lambda b: (0, 0, 0)),
                  wide, nar, nar, nar, nar, wide, wide, wide, wide, wide, wide],
        out_specs=[wide] * 3,
        compiler_params=_params(("parallel",)),
        name="attn_context",
    )(sink, c_lambda, subln3, aq, ak, aks, av, avs, bq, bk, bv, cq, ck, cv)


def _top_values(ref, row0, n_rows, count, out_ref, out_row0, floor):
    for k in range(count):
        cur = ref[row0:row0 + n_rows]
        m = jnp.max(cur, axis=0, keepdims=True)
        out_ref[out_row0 + k:out_row0 + k + 1] = m
        if k + 1 < count:
            ref[row0:row0 + n_rows] = jnp.where(cur == m, floor, cur)


def _batcher_network(n):
    pairs = []
    p = 1
    while p < n:
        k = p
        while k >= 1:
            for j in range(k % p, n - k, 2 * k):
                for i in range(min(k, n - j - k)):
                    if (i + j) // (2 * p) == (i + j + k) // (2 * p):
                        pairs.append((i + j, i + j + k))
            k //= 2
        p *= 2
    return tuple(pairs)


N_TOP = PEER_TOPK + 1
SUBLANES = 8
_SORT_NET = _batcher_network(PEER_KEYS // SUBLANES)
_CAND_COUNTS = tuple(N_TOP // (a + 1) for a in range(N_TOP))
_CAND_OFFS = tuple(int(v) for v in np.cumsum((0,) + _CAND_COUNTS))
N_CAND = -(-_CAND_OFFS[-1] // SUBLANES) * SUBLANES


def _top_sorted(s_ref, lanes, out_ref, slot):
    n_grp = PEER_KEYS // SUBLANES
    lists = [s_ref[r * SUBLANES:(r + 1) * SUBLANES, lanes] for r in range(n_grp)]
    for a, b in _SORT_NET:
        lists[a], lists[b] = jnp.maximum(lists[a], lists[b]), jnp.minimum(lists[a], lists[b])
    for t in range(N_TOP):
        head = lists[0]
        m = jnp.max(head, axis=0, keepdims=True)
        out_ref[slot, t:t + 1, lanes] = m
        remaining = N_TOP - 1 - t
        hit = head == m
        for k in range(remaining):
            below = lists[k + 1] if k + 1 < n_grp else -jnp.inf
            lists[k] = jnp.where(hit, below, lists[k])


def _merge_route_kernel(x_ref, gt_ref, ya_ref, yb_ref, yc_ref, ga_ref, gb_ref, gc_ref,
                        wa_ref, wb_ref, wc_ref, wo_ref, g_ref, sc_ref, sh_ref, wq_ref, keys_ref,
                        o_ref, h_ref, e1_ref, e2_ref, thr_ref, s_ref, top_ref, cand_ref):
    m = (ga_ref[0].astype(F32) * _nn(ya_ref[0], wa_ref[...])
         + gb_ref[0].astype(F32) * _nn(yb_ref[0], wb_ref[...])
         + gc_ref[0].astype(F32) * _nn(yc_ref[0], wc_ref[...]))
    x1 = x_ref[0] + gt_ref[0] * _nn(m.astype(MXU_DTYPE), wo_ref[...])
    o_ref[0] = x1
    _route_block(x1, g_ref, sc_ref, sh_ref, wq_ref, keys_ref,
                 h_ref, e1_ref, e2_ref, thr_ref, s_ref, top_ref, cand_ref)


def _route_block(x, g_ref, sc_ref, sh_ref, wq_ref, keys_ref,
                 h_ref, e1_ref, e2_ref, thr_ref, s_ref, top_ref, cand_ref):
    tt = x.shape[0]
    h2 = _rms_mod(x, g_ref[...], sc_ref[0], sh_ref[0]).astype(MXU_DTYPE)
    h_ref[0] = h2
    q = _nn(h2, wq_ref[...]).astype(MXU_DTYPE)
    top_ref[:, N_TOP:, :] = jnp.full((3, top_ref.shape[1] - N_TOP, tt), -jnp.inf, F32)
    cand_ref[_CAND_OFFS[-1]:, :] = jnp.full((N_CAND - _CAND_OFFS[-1], tt), -1.0, F32)
    for h in range(PEER_HEADS):
        tops = []
        for p, e_ref in ((0, e1_ref), (1, e2_ref)):
            hp = 2 * h + p
            s_ref[...] = _nt(keys_ref[hp], q[:, hp * LANES:(hp + 1) * LANES])
            for tc in range(tt // LANES):
                _top_sorted(s_ref, slice(tc * LANES, (tc + 1) * LANES), top_ref, p)
            m = top_ref[p, 0:1]
            e_ref[0, h] = jnp.exp(s_ref[...] - m)
            tops.append(jnp.exp(top_ref[p] - m))
        e1_top, e2_top = tops
        for a in range(N_TOP):
            cand_ref[_CAND_OFFS[a]:_CAND_OFFS[a + 1]] = e1_top[a:a + 1] * e2_top[0:_CAND_COUNTS[a]]
        _top_values(cand_ref, 0, N_CAND, N_TOP, top_ref.at[2], 0, -1.0)
        best = top_ref[2]
        inv_z = SQRT_HALF / jnp.sum(best[0:PEER_TOPK], axis=0, keepdims=True)
        e1_ref[0, h] = e1_ref[0, h] * inv_z
        thr_ref[0, h:h + 1] = 0.5 * (best[PEER_TOPK - 1:PEER_TOPK] + best[PEER_TOPK:N_TOP]) * inv_z


def _merge_route(x, g2, mod3, mod_row, ya, yb, yc, ga, gb, gc, wa, wb, wc, wo, wq, keys):
    bsz, n, d = x.shape
    tt = min(TT_PEER, n)
    tok = lambda w: pl.BlockSpec((1, tt, w), lambda b, i: (b, i, 0))
    mod = lambda k: pl.BlockSpec((1, 1, d), lambda b, i: (mod_row(b), 0, k))
    gate_shape = jax.ShapeDtypeStruct((bsz, PEER_HEADS, PEER_KEYS, n), F32)
    gate_spec = pl.BlockSpec((1, PEER_HEADS, PEER_KEYS, tt), lambda b, i: (b, 0, 0, i))
    return pl.pallas_call(
        _merge_route_kernel,
        out_shape=[jax.ShapeDtypeStruct(x.shape, F32), jax.ShapeDtypeStruct((bsz, n, d), MXU_DTYPE),
                   gate_shape, gate_shape, jax.ShapeDtypeStruct((bsz, PEER_HEADS, n), F32)],
        grid=(bsz, n // tt),
        in_specs=[tok(d), mod(2),
                  tok(512), tok(512), tok(512), tok(d), tok(d), tok(d),
                  _resident(wa.shape), _resident(wb.shape), _resident(wc.shape), _resident(wo.shape),
                  pl.BlockSpec((1, d), lambda b, i: (0, 0)), mod(4), mod(3),
                  _resident(wq.shape), _resident(keys.shape)],
        out_specs=[tok(d), tok(d), gate_spec, gate_spec,
                   pl.BlockSpec((1, PEER_HEADS, tt), lambda b, i: (b, 0, i))],
        scratch_shapes=[pltpu.VMEM((PEER_KEYS, tt), F32),
                        pltpu.VMEM((3, 3 * SUBLANES, tt), F32),
                        pltpu.VMEM((N_CAND, tt), F32)],
        compiler_params=_params(("parallel", "parallel")),
        name="merge_route",
    )(x, mod3, ya, yb, yc, ga, gb, gc, wa, wb, wc, wo, g2, mod3, mod3, wq, keys)


def _expert_kernel(x_ref, gt_ref, gf_ref, h_ref, u_ref, v_ref, e1_ref, e2_ref, thr_ref, o_ref,
                   acc_ref, a_ref, w_ref, *, final_norm):
    eb = pl.program_id(2)
    tt = h_ref.shape[1]

    @pl.when(eb == 0)
    def _zero():
        acc_ref[...] = jnp.zeros_like(acc_ref)

    a_ref[...] = _nt(u_ref[...], h_ref[0])
    sub = 64
    n_i = TE_PEER // PEER_KEYS
    i0 = pl.multiple_of(eb * n_i, n_i)
    for il in range(n_i):
        for tc in range(tt // LANES):
            lanes = slice(tc * LANES, (tc + 1) * LANES)
            e1_rows = [e1_ref[0, h, pl.ds(i0, n_i), lanes][il:il + 1] for h in range(PEER_HEADS)]
            thr_rows = [thr_ref[0, h:h + 1, lanes] for h in range(PEER_HEADS)]
            for js in range(PEER_KEYS // sub):
                rows = slice(js * sub, (js + 1) * sub)
                gate = None
                for h in range(PEER_HEADS):
                    prod = e2_ref[0, h, rows, lanes] * e1_rows[h]
                    sel = jnp.where(prod >= thr_rows[h], prod, 0.0)
                    gate = sel if gate is None else gate + sel
                erows = slice(il * PEER_KEYS + js * sub, il * PEER_KEYS + (js + 1) * sub)
                t = a_ref[erows, lanes] * SQRT_HALF
                w_ref[erows, lanes] = (gate * t) * (1.0 + lax.erf(t))
    acc_ref[...] += _nn(w_ref[...].T.astype(MXU_DTYPE), v_ref[...])

    @pl.when(eb == pl.num_programs(2) - 1)
    def _finish():
        y = x_ref[0] + gt_ref[0] * acc_ref[...]
        if final_norm:
            y = (y * lax.rsqrt(jnp.mean(y * y, axis=-1, keepdims=True) + NORM_EPS)) * gf_ref[...]
        o_ref[0] = y


def _experts(x, mod3, mod_row, g_final, h2, u, v, e1, e2, thr, final_norm):
    bsz, n, d = x.shape
    tt = min(TT_PEER, n)
    n_exp = u.shape[0]
    gate_spec = pl.BlockSpec((1, PEER_HEADS, PEER_KEYS, tt), lambda b, i, e: (b, 0, 0, i))
    return pl.pallas_call(
        functools.partial(_expert_kernel, final_norm=final_norm),
        out_shape=jax.ShapeDtypeStruct(x.shape, F32),
        grid=(bsz, n // tt, n_exp // TE_PEER),
        in_specs=[pl.BlockSpec((1, tt, d), lambda b, i, e: (b, i, 0)),
                  pl.BlockSpec((1, 1, d), lambda b, i, e: (mod_row(b), 0, 5)),
                  pl.BlockSpec((1, d), lambda b, i, e: (0, 0)),
                  pl.BlockSpec((1, tt, d), lambda b, i, e: (b, i, 0)),
                  pl.BlockSpec((TE_PEER, d), lambda b, i, e: (e, 0)),
                  pl.BlockSpec((TE_PEER, d), lambda b, i, e: (e, 0)),
                  gate_spec, gate_spec,
                  pl.BlockSpec((1, PEER_HEADS, tt), lambda b, i, e: (b, 0, i))],
        out_specs=pl.BlockSpec((1, tt, d), lambda b, i, e: (b, i, 0)),
        scratch_shapes=[pltpu.VMEM((tt, d), F32),
                        pltpu.VMEM((TE_PEER, tt), F32),
                        pltpu.VMEM((TE_PEER, tt), F32)],
        compiler_params=_params(("parallel", "parallel", "arbitrary")),
        name="peer_experts",
    )(x, mod3, g_final, h2, u, v, e1, e2, thr)


def _rope_tables(n):
    t = jnp.arange(n, dtype=jnp.int32)
    row = (t // GRID_W).astype(F32)
    col = (t % GRID_W).astype(F32)
    n_freq = HEAD_DIM // 4
    inv = ROPE_THETA ** (-jnp.arange(n_freq, dtype=F32) / n_freq)
    ang = jnp.concatenate([row[:, None] * inv, col[:, None] * inv], axis=-1)
    cos, sin = jnp.cos(ang), jnp.sin(ang)
    return jnp.tile(cos, (1, 4)), jnp.tile(jnp.concatenate([-sin, sin], axis=-1), (1, 2))


def kernel(x, c, ctx, c_ctx, w_mod, b_mod, g_norm1, g_norm2, w_in, a_sink, b_rpb, c_lambda,
           c_subln, w_branch_a, w_branch_b, w_branch_c, w_out, peer_wq, peer_keys, peer_u,
           peer_v, g_final):
    bsz, n, d = x.shape
    depth = w_mod.shape[0]
    n_ctx = ctx.shape[1]
    assert d == D_MODEL and bsz + 1 <= 8
    assert n % (B_ROWS * GRID_W) == 0 and n // GRID_W >= 4 * B_KROWS
    assert n % (A_QBLOCKS * A_BLOCK) == 0 and (n + n_ctx) % LANES == 0

    cvec = jnp.zeros((8, d), F32).at[:bsz].set(c).at[bsz].set(c_ctx)
    mod = _modulation(cvec, w_mod, b_mod)
    lat_row = lambda b: b
    ctx_row = lambda b: bsz
    cos_t, sin_t = _rope_tables(n)
    ones_t, zeros_t = jnp.ones((n_ctx, LANES), F32), jnp.zeros((n_ctx, LANES), F32)
    cast = lambda w: w.astype(MXU_DTYPE)
    gfin = g_final.reshape(1, d)

    xc = ctx
    for l in range(depth):
        last = l == depth - 1
        lam_init = 0.8 - 0.6 * math.exp(-0.3 * l)
        mod3 = mod[l].reshape(8, 1, 6 * d)
        g1, g2 = g_norm1[l].reshape(1, d), g_norm2[l].reshape(1, d)
        w_in_l = cast(w_in[l])
        subln3 = c_subln[l].reshape(C_HEADS, 1, LANES)
        wa, wb, wc, wo = cast(w_branch_a[l]), cast(w_branch_b[l]), cast(w_branch_c[l]), cast(w_out[l])
        wq = cast(peer_wq[l])
        keys = cast(peer_keys[l].reshape(2 * PEER_HEADS, PEER_KEYS, LANES))
        u, v = cast(peer_u[l]), cast(peer_v[l])

        lat = _in_proj(x, g1, mod3, lat_row, w_in_l, cos_t, sin_t)
        con = _in_proj(xc, g1, mod3, ctx_row, w_in_l, ones_t, zeros_t)
        (aq, ak, aks, av, avs, bq, bk, bv, cq, ck, _, ga, gb, gc, cvt) = lat
        ya = _attn_a(a_sink[l], aq, ak, aks, av, avs, con[1], con[2], con[3], con[4])
        yb = _attn_b(b_rpb[l], bq, bk, bv, con[6], con[7])
        yc = _attn_c(c_lambda[l], subln3, cq, ck, cvt, con[9], con[14], lam_init)
        x, h2, e1, e2, thr = _merge_route(x, g2, mod3, lat_row, ya, yb, yc, ga, gb, gc,
                                          wa, wb, wc, wo, wq, keys)
        x = _experts(x, mod3, lat_row, gfin, h2, u, v, e1, e2, thr, final_norm=last)

        if not last:
            ya_c, yb_c, yc_c = _ctx_attn(a_sink[l], c_lambda[l], subln3, con, lam_init)
            xc, h2c, e1c, e2c, thrc = _merge_route(xc, g2, mod3, ctx_row, ya_c, yb_c, yc_c,
                                                   con[11], con[12], con[13], wa, wb, wc, wo, wq, keys)
            xc = _experts(xc, mod3, ctx_row, gfin, h2c, u, v, e1c, e2c, thrc, final_norm=False)
    return x
```
